```python
import math
import jax, jax.numpy as jnp
from jax import lax
import numpy as np

D_MODEL = 1024
BATCH = 2
SEQ = 16384
DEPTH = 4

N_MEM = 256
HEAD_DIM = 64
N_MEM_HEADS = 4
MEM_WIDTH = N_MEM_HEADS * HEAD_DIM
CONV_WIDTH = D_MODEL - MEM_WIDTH
CONV_K = 3
N_Q_HEADS = CONV_WIDTH // HEAD_DIM
N_KV_HEADS = 4
GROUP = N_Q_HEADS // N_KV_HEADS
Q_WIDTH = N_Q_HEADS * HEAD_DIM
KV_WIDTH = N_KV_HEADS * HEAD_DIM
A_PROJ = 3 * CONV_WIDTH + MEM_WIDTH
B_PROJ = Q_WIDTH + MEM_WIDTH
WINDOW = 128
BLOCK = 128
REL_BUCKETS = 32
REL_MAX_DIST = 128
D_FF = ((8 * D_MODEL + 3 * 256 - 1) // (3 * 256)) * 256
N_A = DEPTH // 2
N_B = DEPTH - N_A
EPS = 1e-5

kernel_name = 'yoco_shortconv_swa_sink_hybrid'


def rmsnorm(x, g):
    x32 = x.astype(jnp.float32)
    y = x32 * lax.rsqrt(jnp.mean(x32 * x32, axis=-1, keepdims=True) + EPS)
    return (y * g.astype(jnp.float32)).astype(x.dtype)


def _rel_bucket(dist):
    max_exact = REL_BUCKETS // 2
    d = jnp.maximum(dist, 1).astype(jnp.float32)
    large = max_exact + (jnp.log(d / max_exact) / math.log(REL_MAX_DIST / max_exact)
                         * (REL_BUCKETS - max_exact)).astype(jnp.int32)
    large = jnp.minimum(large, REL_BUCKETS - 1)
    return jnp.where(dist < max_exact, dist, large)


def _band_geometry(n_blocks):
    qi = jnp.arange(BLOCK, dtype=jnp.int32)[:, None]
    kj = jnp.arange(2 * BLOCK, dtype=jnp.int32)[None, :]
    dist = qi + BLOCK - kj
    in_window = (dist >= 0) & (dist < WINDOW)
    first = (jnp.arange(n_blocks) == 0)[:, None, None]
    mask = in_window[None] & ~(first & (kj[None] < BLOCK))
    bucket = _rel_bucket(jnp.maximum(dist, 0))
    return mask, bucket


def _band(t):
    bsz, s = t.shape[0], t.shape[1]
    tb = t.reshape(bsz, s // BLOCK, BLOCK, N_KV_HEADS, HEAD_DIM)
    prev = jnp.concatenate([jnp.zeros_like(tb[:, :1]), tb[:, :-1]], axis=1)
    return jnp.concatenate([prev, tb], axis=2)


def _short_conv(u, b_gate, c_gate, w):
    v = c_gate * u
    s = v.shape[1]
    vp = jnp.pad(v, ((0, 0), (CONV_K - 1, 0), (0, 0)))
    conv = w[0] * vp[:, 0:s] + w[1] * vp[:, 1:s + 1] + w[2] * vp[:, 2:s + 2]
    return b_gate * conv


def _swa_sinks(q, k_band, v_band, sinks, rel_bias, mask, bucket):
    bsz, s, _ = q.shape
    nb = s // BLOCK
    qb = q.reshape(bsz, nb, BLOCK, N_KV_HEADS, GROUP, HEAD_DIM)
    logits = jnp.einsum('bnqhgd,bnjhd->bnhgqj', qb, k_band).astype(jnp.float32) * (HEAD_DIM ** -0.5)
    bias = jnp.transpose(rel_bias.astype(jnp.float32)[bucket], (2, 0, 1))
    bias = bias.reshape(N_KV_HEADS, GROUP, BLOCK, 2 * BLOCK)
    logits = jnp.where(mask[None, :, None, None], logits + bias, -jnp.inf)
    sink = sinks.astype(jnp.float32).reshape(N_KV_HEADS, GROUP, 1, 1)
    m = jnp.maximum(jnp.max(logits, axis=-1, keepdims=True), sink)
    p = jnp.exp(logits - m)
    denom = jnp.sum(p, axis=-1, keepdims=True) + jnp.exp(sink - m)
    probs = (p / denom).astype(v_band.dtype)
    out = jnp.einsum('bnhgqj,bnjhd->bnqhgd', probs, v_band)
    return out.reshape(bsz, s, Q_WIDTH)


def _mem_attention(q_mem, mem_k, mem_v):
    bsz, s = q_mem.shape[0], q_mem.shape[1]
    logits = jnp.einsum('bshd,bmhd->bhsm', q_mem, mem_k).astype(jnp.float32) * (HEAD_DIM ** -0.5)
    probs = jax.nn.softmax(logits, axis=-1).astype(mem_v.dtype)
    return jnp.einsum('bhsm,bmhd->bshd', probs, mem_v).reshape(bsz, s, MEM_WIDTH)


def setup_inputs(seed: int = 0) -> dict:
    key = jax.random.key(seed)
    ks = jax.random.split(key, 20)

    def nrm(k, shape, scale):
        return jax.random.normal(k, shape, jnp.float32) * scale

    def gain(k, shape):
        return 1.0 + nrm(k, shape, 0.05)

    return {
        'x': nrm(ks[0], (BATCH, SEQ, D_MODEL), 1.0),
        'mem': nrm(ks[1], (BATCH, N_MEM, D_MODEL), 1.0),
        'norm_mix': gain(ks[2], (DEPTH, D_MODEL)),
        'norm_ffn': gain(ks[3], (DEPTH, D_MODEL)),
        'a_w_in': nrm(ks[4], (N_A, D_MODEL, A_PROJ), D_MODEL ** -0.5),
        'a_conv_w': nrm(ks[5], (N_A, CONV_K, CONV_WIDTH), CONV_K ** -0.5),
        'a_w_out': nrm(ks[6], (N_A, CONV_WIDTH + MEM_WIDTH, D_MODEL), (CONV_WIDTH + MEM_WIDTH) ** -0.5),
        'kv_norm': gain(ks[7], (D_MODEL,)),
        'w_kv': nrm(ks[8], (D_MODEL, 2 * KV_WIDTH), D_MODEL ** -0.5),
        'b_w_q': nrm(ks[9], (N_B, D_MODEL, B_PROJ), D_MODEL ** -0.5),
        'b_sinks': nrm(ks[10], (N_B, N_Q_HEADS), 0.5),
        'b_w_out': nrm(ks[11], (N_B, Q_WIDTH + MEM_WIDTH, D_MODEL), (Q_WIDTH + MEM_WIDTH) ** -0.5),
        'rel_bias': nrm(ks[12], (REL_BUCKETS, N_Q_HEADS), 0.5),
        'mem_norm': gain(ks[13], (D_MODEL,)),
        'w_mem_kv': nrm(ks[14], (DEPTH, D_MODEL, 2 * MEM_WIDTH), D_MODEL ** -0.5),
        'w_gate': nrm(ks[15], (DEPTH, D_MODEL, D_FF), D_MODEL ** -0.5),
        'w_up': nrm(ks[16], (DEPTH, D_MODEL, D_FF), D_MODEL ** -0.5),
        'w_down': nrm(ks[17], (DEPTH, D_FF, D_MODEL), D_FF ** -0.5),
        'final_norm': gain(ks[18], (D_MODEL,)),
    }


def reference(x, mem, norm_mix, norm_ffn, a_w_in, a_conv_w, a_w_out, kv_norm, w_kv,
              b_w_q, b_sinks, b_w_out, rel_bias, mem_norm, w_mem_kv, w_gate, w_up, w_down,
              final_norm):
    bsz, s, _ = x.shape
    mask, bucket = _band_geometry(s // BLOCK)
    mem_n = rmsnorm(mem, mem_norm)
    k_band = None
    v_band = None
    for i in range(DEPTH):
        if i == N_A:
            kv = rmsnorm(x, kv_norm) @ w_kv
            k, v = jnp.split(kv, 2, axis=-1)
            k_band = _band(k.reshape(bsz, s, N_KV_HEADS, HEAD_DIM))
            v_band = _band(v.reshape(bsz, s, N_KV_HEADS, HEAD_DIM))
        mem_kv = mem_n @ w_mem_kv[i]
        mk, mv = jnp.split(mem_kv, 2, axis=-1)
        mk = mk.reshape(bsz, N_MEM, N_MEM_HEADS, HEAD_DIM)
        mv = mv.reshape(bsz, N_MEM, N_MEM_HEADS, HEAD_DIM)
        h = rmsnorm(x, norm_mix[i])
        if i < N_A:
            proj = h @ a_w_in[i]
            u, b_gate, c_gate, q_mem = jnp.split(
                proj, [CONV_WIDTH, 2 * CONV_WIDTH, 3 * CONV_WIDTH], axis=-1)
            y_tok = _short_conv(u, b_gate, c_gate, a_conv_w[i])
            w_out = a_w_out[i]
        else:
            j = i - N_A
            proj = h @ b_w_q[j]
            q, q_mem = jnp.split(proj, [Q_WIDTH], axis=-1)
            y_tok = _swa_sinks(q, k_band, v_band, b_sinks[j], rel_bias, mask, bucket)
            w_out = b_w_out[j]
        y_mem = _mem_attention(q_mem.reshape(bsz, s, N_MEM_HEADS, HEAD_DIM), mk, mv)
        x = x + jnp.concatenate([y_tok, y_mem], axis=-1) @ w_out
        h = rmsnorm(x, norm_ffn[i])
        x = x + (jax.nn.silu(h @ w_gate[i]) * (h @ w_up[i])) @ w_down[i]
    return rmsnorm(x, final_norm)
```

```python
import functools
import math

import jax
import jax.numpy as jnp
from jax import lax
from jax.experimental import pallas as pl
from jax.experimental.pallas import tpu as pltpu

D_MODEL = 1024
DEPTH = 4
N_MEM = 256
HEAD_DIM = 64
N_MEM_HEADS = 4
MEM_WIDTH = N_MEM_HEADS * HEAD_DIM
CONV_WIDTH = D_MODEL - MEM_WIDTH
CONV_K = 3
N_Q_HEADS = CONV_WIDTH // HEAD_DIM
N_KV_HEADS = 4
GROUP = N_Q_HEADS // N_KV_HEADS
Q_WIDTH = N_Q_HEADS * HEAD_DIM
KV_WIDTH = N_KV_HEADS * HEAD_DIM
A_PROJ = 3 * CONV_WIDTH + MEM_WIDTH
B_PROJ = Q_WIDTH + MEM_WIDTH
WINDOW = 128
BLOCK = 128
REL_BUCKETS = 32
REL_MAX_DIST = 128
D_FF = ((8 * D_MODEL + 3 * 256 - 1) // (3 * 256)) * 256
N_A = DEPTH // 2
N_B = DEPTH - N_A
EPS = 1e-5

SCALE = HEAD_DIM ** -0.5
NEG = -1e30
LANES = 128
SUBLANES = 8
MXU_DIM = 256
TOKEN_TILE = 512
FF_CHUNK = MXU_DIM
VMEM_LIMIT_BYTES = 56 * 1024 * 1024

BF16 = jnp.bfloat16
F32 = jnp.float32


def _dot(a, b):
    return jnp.dot(a, b, preferred_element_type=F32)


def _dot_nt(a, b):
    return lax.dot_general(a, b, (((1,), (1,)), ((), ())), preferred_element_type=F32)


def _rms(x, g):
    ms = jnp.mean(x * x, axis=-1, keepdims=True)
    return x * lax.rsqrt(ms + EPS) * g


def _mem_kv_kernel(mem_ref, g_ref, w_ref, kt_ref, v_ref, *, batch):
    mem_n = _rms(mem_ref[...], g_ref[...]).astype(BF16)
    kv = _dot(mem_n, w_ref[...])
    row = lax.broadcasted_iota(jnp.int32, (MEM_WIDTH, N_MEM), 0)
    col = lax.broadcasted_iota(jnp.int32, (N_MEM, MEM_WIDTH), 1)
    for b in range(batch):
        kvb = kv[b * N_MEM:(b + 1) * N_MEM]
        kt = (kvb[:, :MEM_WIDTH] * SCALE).T
        v = kvb[:, MEM_WIDTH:]
        for h in range(N_MEM_HEADS):
            lo, hi = h * HEAD_DIM, (h + 1) * HEAD_DIM
            kt_ref[b, h] = jnp.where((row >= lo) & (row < hi), kt, 0.0).astype(BF16)
            v_ref[b, h] = jnp.where((col >= lo) & (col < hi), v, 0.0).astype(BF16)


def _mem_kv(mem, mem_norm, w_mem_kv_bf16):
    batch = mem.shape[0]
    out_sds = jax.ShapeDtypeStruct((DEPTH, batch, N_MEM_HEADS, MEM_WIDTH, N_MEM), BF16)
    out_spec = pl.BlockSpec((None, batch, N_MEM_HEADS, MEM_WIDTH, N_MEM),
                            lambda i: (i, 0, 0, 0, 0))
    return pl.pallas_call(
        functools.partial(_mem_kv_kernel, batch=batch),
        grid=(DEPTH,),
        in_specs=[
            pl.BlockSpec((batch * N_MEM, D_MODEL), lambda i: (0, 0)),
            pl.BlockSpec((1, D_MODEL), lambda i: (0, 0)),
            pl.BlockSpec((None, D_MODEL, 2 * MEM_WIDTH), lambda i: (i, 0, 0)),
        ],
        out_specs=[out_spec, out_spec],
        out_shape=[out_sds, out_sds],
        compiler_params=pltpu.CompilerParams(dimension_semantics=("arbitrary",)),
        name="mem_kv",
    )(mem.reshape(batch * N_MEM, D_MODEL), mem_norm.reshape(1, D_MODEL), w_mem_kv_bf16)


def _bias_kernel(bucket_ref, inwin_ref, table_ref, out_ref):
    bucket = bucket_ref[...]
    inwin = inwin_ref[...] > 0
    for h in range(N_Q_HEADS):
        def body(b, acc, h=h):
            return jnp.where(bucket == b, table_ref[b, h], acc)
        acc = lax.fori_loop(0, REL_BUCKETS, body, jnp.zeros((BLOCK, 2 * BLOCK), F32))
        out_ref[h] = jnp.where(inwin, acc, NEG)


def _rel_bucket(dist):
    max_exact = REL_BUCKETS // 2
    d = jnp.maximum(dist, 1).astype(F32)
    large = max_exact + (jnp.log(d / max_exact) / math.log(REL_MAX_DIST / max_exact)
                         * (REL_BUCKETS - max_exact)).astype(jnp.int32)
    large = jnp.minimum(large, REL_BUCKETS - 1)
    return jnp.where(dist < max_exact, dist, large)


def _band_bias(rel_bias):
    qi = jnp.arange(BLOCK, dtype=jnp.int32)[:, None]
    kj = jnp.arange(2 * BLOCK, dtype=jnp.int32)[None, :]
    dist = qi + BLOCK - kj
    inwin = ((dist >= 0) & (dist < WINDOW)).astype(jnp.int32)
    bucket = _rel_bucket(jnp.maximum(dist, 0)).astype(jnp.int32)
    return pl.pallas_call(
        _bias_kernel,
        in_specs=[
            pl.BlockSpec(memory_space=pltpu.VMEM),
            pl.BlockSpec(memory_space=pltpu.VMEM),
            pl.BlockSpec(memory_space=pltpu.SMEM),
        ],
        out_specs=pl.BlockSpec(memory_space=pltpu.VMEM),
        out_shape=jax.ShapeDtypeStruct((N_Q_HEADS, BLOCK, 2 * BLOCK), F32),
        name="band_bias",
    )(bucket, inwin, rel_bias.astype(F32))


def _mem_attention(qm, mkt_ref, mv_ref):
    out = None
    for h in range(N_MEM_HEADS):
        logits = _dot(qm, mkt_ref[h])
        m = jnp.max(logits, axis=-1, keepdims=True)
        p = jnp.exp(logits - m)
        s = jnp.sum(p, axis=-1, keepdims=True)
        o = _dot(p.astype(BF16), mv_ref[h]) * (1.0 / s)
        out = o if out is None else out + o
    return out


def _short_conv(u, b_gate, c_gate, w, vs_ref, first, tm):
    v = c_gate * u

    @pl.when(first)
    def _():
        vs_ref[0:SUBLANES, :] = jnp.zeros((SUBLANES, CONV_WIDTH), F32)

    vs_ref[SUBLANES:SUBLANES + tm, :] = v
    conv = (w[0:1] * vs_ref[SUBLANES - 2:SUBLANES - 2 + tm, :]
            + w[1:2] * vs_ref[SUBLANES - 1:SUBLANES - 1 + tm, :]
            + w[2:3] * v)
    vs_ref[0:SUBLANES, :] = vs_ref[tm:tm + SUBLANES, :]
    return b_gate * conv


def _swa(q, kv_prev_ref, kv_cur_ref, bias_ref, sinks_ref, layer, first, tm):
    col = lax.broadcasted_iota(jnp.int32, (BLOCK, 2 * BLOCK), 1)
    kill_prev = jnp.logical_and(first, col < BLOCK)
    zeros = jnp.zeros((HEAD_DIM, 2 * BLOCK), BF16)
    rows = []
    for n in range(tm // BLOCK):
        if n == 0:
            keys = jnp.concatenate([kv_prev_ref[...], kv_cur_ref[:, 0:BLOCK]], axis=1)
        else:
            keys = kv_cur_ref[:, (n - 1) * BLOCK:(n + 1) * BLOCK]
        outs = []
        for hk in range(N_KV_HEADS):
            kt = keys[hk * HEAD_DIM:(hk + 1) * HEAD_DIM, :]
            vt = keys[KV_WIDTH + hk * HEAD_DIM:KV_WIDTH + (hk + 1) * HEAD_DIM, :]
            k_lo = jnp.concatenate([kt, zeros], axis=0)
            k_hi = jnp.concatenate([zeros, kt], axis=0)
            ps, invs = [], []
            for g in range(GROUP):
                head = hk * GROUP + g
                grp, pos = head // 2, head % 2
                qg = q[n * BLOCK:(n + 1) * BLOCK, grp * LANES:(grp + 1) * LANES]
                bias = bias_ref[head]
                if n == 0:
                    bias = jnp.where(kill_prev, NEG, bias)
                logits = _dot(qg, k_hi if pos else k_lo) + bias
                sink = sinks_ref[layer, head]
                m = jnp.maximum(jnp.max(logits, axis=-1, keepdims=True), sink)
                p = jnp.exp(logits - m)
                s = jnp.sum(p, axis=-1, keepdims=True) + jnp.exp(sink - m)
                ps.append(p.astype(BF16))
                invs.append(1.0 / s)
            o3 = _dot_nt(jnp.concatenate(ps, axis=0), vt)
            for g in range(GROUP):
                outs.append(o3[g * BLOCK:(g + 1) * BLOCK] * invs[g])
        rows.append(jnp.concatenate(outs, axis=-1))
    return jnp.concatenate(rows, axis=0)


def _ffn(x, g, wg_ref, wu_ref, wd_ref):
    h = _rms(x, g).astype(BF16)
    acc = x
    for c in range(D_FF // FF_CHUNK):
        sl = slice(c * FF_CHUNK, (c + 1) * FF_CHUNK)
        gate = _dot(h, wg_ref[:, sl])
        up = _dot(h, wu_ref[:, sl])
        act = (gate * jax.nn.sigmoid(gate) * up).astype(BF16)
        acc = acc + _dot(act, wd_ref[sl, :])
    return acc


def _layer_kernel(*refs, mixer, epilogue, layer, tm):
    it = iter(refs)
    x_ref, gmix_ref, win_ref, mkt_ref, mv_ref, wout_ref = (next(it) for _ in range(6))
    gffn_ref, wg_ref, wu_ref, wd_ref = (next(it) for _ in range(4))
    if mixer == "conv":
        convw_ref = next(it)
    else:
        kv_prev_ref, kv_cur_ref, bias_ref, sinks_ref = (next(it) for _ in range(4))
    if epilogue == "kv":
        gkv_ref, wkvt_ref = next(it), next(it)
    elif epilogue == "final":
        gfin_ref = next(it)
    out_ref = next(it)
    if epilogue == "kv":
        kvt_ref = next(it)
    if mixer == "conv":
        vs_ref = next(it)

    first = pl.program_id(1) == 0
    x = x_ref[...]
    h = _rms(x, gmix_ref[...]).astype(BF16)
    proj = _dot(h, win_ref[...])
    if mixer == "conv":
        u = proj[:, :CONV_WIDTH]
        b_gate = proj[:, CONV_WIDTH:2 * CONV_WIDTH]
        c_gate = proj[:, 2 * CONV_WIDTH:3 * CONV_WIDTH]
        qm = proj[:, 3 * CONV_WIDTH:]
        y_tok = _short_conv(u, b_gate, c_gate, convw_ref[...], vs_ref, first, tm)
    else:
        q = proj[:, :Q_WIDTH].astype(BF16)
        qm = proj[:, Q_WIDTH:]
        y_tok = _swa(q, kv_prev_ref, kv_cur_ref, bias_ref, sinks_ref, layer, first, tm)
    y_mem = _mem_attention(qm.astype(BF16), mkt_ref, mv_ref)
    y = jnp.concatenate([y_tok, y_mem], axis=-1).astype(BF16)
    x = x + _dot(y, wout_ref[...])
    x = _ffn(x, gffn_ref[...], wg_ref, wu_ref, wd_ref)
    if epilogue == "kv":
        hk = _rms(x, gkv_ref[...]).astype(BF16)
        kvt = _dot_nt(wkvt_ref[...], hk)
        row = lax.broadcasted_iota(jnp.int32, (2 * KV_WIDTH, 1), 0)
        kvt_ref[...] = (kvt * jnp.where(row < KV_WIDTH, SCALE, 1.0)).astype(BF16)
    if epilogue == "final":
        x = _rms(x, gfin_ref[...])
    out_ref[...] = x


def _const_spec(shape, index):
    nd = len(index)
    return pl.BlockSpec(shape, lambda b, j: index, pipeline_mode=pl.Buffered(1))


def _layer(x, *, mixer, epilogue, layer, sub, p):
    batch, seq, _ = x.shape
    tm = TOKEN_TILE
    assert seq % tm == 0 and tm % BLOCK == 0
    grid = (batch, seq // tm)
    proj_w = A_PROJ if mixer == "conv" else B_PROJ

    x_spec = pl.BlockSpec((None, tm, D_MODEL), lambda b, j: (b, j, 0))
    mem_spec = pl.BlockSpec((None, None, N_MEM_HEADS, MEM_WIDTH, N_MEM),
                            lambda b, j: (layer, b, 0, 0, 0))
    in_specs = [
        x_spec,
        _const_spec((None, 1, D_MODEL), (layer, 0, 0)),
        _const_spec((None, D_MODEL, proj_w), (sub, 0, 0)),
        mem_spec, mem_spec,
        _const_spec((None, D_MODEL, D_MODEL), (sub, 0, 0)),
        _const_spec((None, 1, D_MODEL), (layer, 0, 0)),
        _const_spec((None, D_MODEL, D_FF), (layer, 0, 0)),
        _const_spec((None, D_MODEL, D_FF), (layer, 0, 0)),
        _const_spec((None, D_FF, D_MODEL), (layer, 0, 0)),
    ]
    args = [x, p["norm_mix"], p["w_in"][mixer], p["mkt"], p["mv"], p["w_out"][mixer],
            p["norm_ffn"], p["w_gate"], p["w_up"], p["w_down"]]
    scratch = []
    if mixer == "conv":
        in_specs.append(_const_spec((None, CONV_K, CONV_WIDTH), (sub, 0, 0)))
        args.append(p["conv_w"])
        scratch.append(pltpu.VMEM((tm + SUBLANES, CONV_WIDTH), F32))
    else:
        blocks_per_tile = tm // BLOCK
        in_specs += [
            pl.BlockSpec((None, 2 * KV_WIDTH, BLOCK),
                         lambda b, j: (b, 0, jnp.maximum(j * blocks_per_tile - 1, 0))),
            pl.BlockSpec((None, 2 * KV_WIDTH, tm), lambda b, j: (b, 0, j)),
            _const_spec((N_Q_HEADS, BLOCK, 2 * BLOCK), (0, 0, 0)),
            pl.BlockSpec(memory_space=pltpu.SMEM),
        ]
        args += [p["kvt"], p["kvt"], p["bias"], p["sinks"]]
    if epilogue == "kv":
        in_specs += [_const_spec((1, D_MODEL), (0, 0)),
                     _const_spec((2 * KV_WIDTH, D_MODEL), (0, 0))]
        args += [p["kv_norm"], p["w_kvt"]]
    elif epilogue == "final":
        in_specs.append(_const_spec((1, D_MODEL), (0, 0)))
        args.append(p["final_norm"])

    out_specs = [x_spec]
    out_shape = [jax.ShapeDtypeStruct(x.shape, F32)]
    if epilogue == "kv":
        out_specs.append(pl.BlockSpec((None, 2 * KV_WIDTH, tm), lambda b, j: (b, 0, j)))
        out_shape.append(jax.ShapeDtypeStruct((batch, 2 * KV_WIDTH, seq), BF16))

    outs = pl.pallas_call(
        functools.partial(_layer_kernel, mixer=mixer, epilogue=epilogue, layer=sub, tm=tm),
        grid=grid,
        in_specs=in_specs,
        out_specs=out_specs,
        out_shape=out_shape,
        scratch_shapes=scratch,
        compiler_params=pltpu.CompilerParams(
            dimension_semantics=("arbitrary", "arbitrary"),
            vmem_limit_bytes=VMEM_LIMIT_BYTES),
        name=f"layer{layer}_{mixer}",
    )(*args)
    return outs


def kernel(x, mem, norm_mix, norm_ffn, a_w_in, a_conv_w, a_w_out, kv_norm, w_kv, b_w_q, b_sinks,
           b_w_out, rel_bias, mem_norm, w_mem_kv, w_gate, w_up, w_down, final_norm):
    mkt, mv = _mem_kv(mem, mem_norm, w_mem_kv.astype(BF16))
    p = {
        "norm_mix": norm_mix.reshape(DEPTH, 1, D_MODEL),
        "norm_ffn": norm_ffn.reshape(DEPTH, 1, D_MODEL),
        "w_in": {"conv": a_w_in.astype(BF16), "swa": b_w_q.astype(BF16)},
        "w_out": {"conv": a_w_out.astype(BF16), "swa": b_w_out.astype(BF16)},
        "conv_w": a_conv_w,
        "mkt": mkt, "mv": mv,
        "w_gate": w_gate.astype(BF16), "w_up": w_up.astype(BF16), "w_down": w_down.astype(BF16),
        "kv_norm": kv_norm.reshape(1, D_MODEL),
        "w_kvt": w_kv.T.astype(BF16),
        "final_norm": final_norm.reshape(1, D_MODEL),
        "sinks": b_sinks.astype(F32),
        "bias": _band_bias(rel_bias),
    }
    for i in range(DEPTH):
        mixer = "conv" if i < N_A else "swa"
        sub = i if i < N_A else i - N_A
        epilogue = "kv" if i == N_A - 1 else ("final" if i == DEPTH - 1 else "none")
        outs = _layer(x, mixer=mixer, epilogue=epilogue, layer=i, sub=sub, p=p)
        x = outs[0]
        if epilogue == "kv":
            p["kvt"] = outs[1]
    return x
```

```python
import functools
import math

import jax
import jax.numpy as jnp
from jax import lax
from jax.experimental import pallas as pl
from jax.experimental.pallas import tpu as pltpu

D_MODEL = 1024
DEPTH = 4
N_MEM = 256
HEAD_DIM = 64
N_MEM_HEADS = 4
MEM_WIDTH = N_MEM_HEADS * HEAD_DIM
CONV_WIDTH = D_MODEL - MEM_WIDTH
CONV_K = 3
N_Q_HEADS = CONV_WIDTH // HEAD_DIM
N_KV_HEADS = 4
GROUP = N_Q_HEADS // N_KV_HEADS
Q_WIDTH = N_Q_HEADS * HEAD_DIM
KV_WIDTH = N_KV_HEADS * HEAD_DIM
A_PROJ = 3 * CONV_WIDTH + MEM_WIDTH
B_PROJ = Q_WIDTH + MEM_WIDTH
WINDOW = 128
BLOCK = 128
REL_BUCKETS = 32
REL_MAX_DIST = 128
D_FF = ((8 * D_MODEL + 3 * 256 - 1) // (3 * 256)) * 256
N_A = DEPTH // 2
N_B = DEPTH - N_A
EPS = 1e-5

SCALE = HEAD_DIM ** -0.5
NEG = -1e30
LANES = 128
SUBLANES = 8
MXU_DIM = 256
TOKEN_TILE = 512
VMEM_LIMIT_BYTES = 56 * 1024 * 1024

BF16 = jnp.bfloat16
F32 = jnp.float32


def _dot(a, b):
    return jnp.dot(a, b, preferred_element_type=F32)


def _dot_nt(a, b):
    return lax.dot_general(a, b, (((1,), (1,)), ((), ())), preferred_element_type=F32)


def _rms(x, g):
    ms = jnp.mean(x * x, axis=-1, keepdims=True)
    return x * lax.rsqrt(ms + EPS) * g


def _mem_kv_kernel(mem_ref, g_ref, w_ref, kt_ref, v_ref, *, batch):
    mem_n = _rms(mem_ref[...], g_ref[...]).astype(BF16)
    kv = _dot(mem_n, w_ref[...])
    row = lax.broadcasted_iota(jnp.int32, (MEM_WIDTH, N_MEM), 0)
    col = lax.broadcasted_iota(jnp.int32, (N_MEM, MEM_WIDTH), 1)
    for b in range(batch):
        kvb = kv[b * N_MEM:(b + 1) * N_MEM]
        kt = (kvb[:, :MEM_WIDTH] * SCALE).T
        v = kvb[:, MEM_WIDTH:]
        for h in range(N_MEM_HEADS):
            lo, hi = h * HEAD_DIM, (h + 1) * HEAD_DIM
            kt_ref[b, h] = jnp.where((row >= lo) & (row < hi), kt, 0.0).astype(BF16)
            v_ref[b, h] = jnp.where((col >= lo) & (col < hi), v, 0.0).astype(BF16)


def _mem_kv(mem, mem_norm, w_mem_kv_bf16):
    batch = mem.shape[0]
    out_sds = jax.ShapeDtypeStruct((DEPTH, batch, N_MEM_HEADS, MEM_WIDTH, N_MEM), BF16)
    out_spec = pl.BlockSpec((None, batch, N_MEM_HEADS, MEM_WIDTH, N_MEM),
                            lambda i: (i, 0, 0, 0, 0))
    return pl.pallas_call(
        functools.partial(_mem_kv_kernel, batch=batch),
        grid=(DEPTH,),
        in_specs=[
            pl.BlockSpec((batch * N_MEM, D_MODEL), lambda i: (0, 0)),
            pl.BlockSpec((1, D_MODEL), lambda i: (0, 0)),
            pl.BlockSpec((None, D_MODEL, 2 * MEM_WIDTH), lambda i: (i, 0, 0)),
        ],
        out_specs=[out_spec, out_spec],
        out_shape=[out_sds, out_sds],
        compiler_params=pltpu.CompilerParams(dimension_semantics=("arbitrary",)),
        name="mem_kv",
    )(mem.reshape(batch * N_MEM, D_MODEL), mem_norm.reshape(1, D_MODEL), w_mem_kv_bf16)


def _bias_kernel(bucket_ref, inwin_ref, table_ref, out_ref):
    bucket = bucket_ref[...]
    inwin = inwin_ref[...] > 0
    for h in range(N_Q_HEADS):
        def body(b, acc, h=h):
            return jnp.where(bucket == b, table_ref[b, h], acc)
        acc = lax.fori_loop(0, REL_BUCKETS, body, jnp.zeros((BLOCK, 2 * BLOCK), F32))
        out_ref[h] = jnp.where(inwin, acc, NEG)


def _rel_bucket(dist):
    max_exact = REL_BUCKETS // 2
    d = jnp.maximum(dist, 1).astype(F32)
    large = max_exact + (jnp.log(d / max_exact) / math.log(REL_MAX_DIST / max_exact)
                         * (REL_BUCKETS - max_exact)).astype(jnp.int32)
    large = jnp.minimum(large, REL_BUCKETS - 1)
    return jnp.where(dist < max_exact, dist, large)


def _band_bias(rel_bias):
    qi = jnp.arange(BLOCK, dtype=jnp.int32)[:, None]
    kj = jnp.arange(2 * BLOCK, dtype=jnp.int32)[None, :]
    dist = qi + BLOCK - kj
    inwin = ((dist >= 0) & (dist < WINDOW)).astype(jnp.int32)
    bucket = _rel_bucket(jnp.maximum(dist, 0)).astype(jnp.int32)
    return pl.pallas_call(
        _bias_kernel,
        in_specs=[
            pl.BlockSpec(memory_space=pltpu.VMEM),
            pl.BlockSpec(memory_space=pltpu.VMEM),
            pl.BlockSpec(memory_space=pltpu.SMEM),
        ],
        out_specs=pl.BlockSpec(memory_space=pltpu.VMEM),
        out_shape=jax.ShapeDtypeStruct((N_Q_HEADS, BLOCK, 2 * BLOCK), F32),
        name="band_bias",
    )(bucket, inwin, rel_bias.astype(F32))


def _mem_attention(qm, mkt_ref, mv_ref):
    logits = [_dot(qm, mkt_ref[h]) for h in range(N_MEM_HEADS)]
    yield
    out = None
    for h in range(N_MEM_HEADS):
        m = jnp.max(logits[h], axis=-1, keepdims=True)
        p = jnp.exp(logits[h] - m)
        s = jnp.sum(p, axis=-1, keepdims=True)
        o = _dot(p.astype(BF16), mv_ref[h]) * (1.0 / s)
        out = o if out is None else out + o
        yield
    return out


def _short_conv(u, b_gate, c_gate, w, vs_ref, rows):
    v = c_gate * u
    base = SUBLANES
    vs_ref[base:base + rows, :] = v
    conv = (w[0:1] * vs_ref[base - 2:base - 2 + rows, :]
            + w[1:2] * vs_ref[base - 1:base - 1 + rows, :]
            + w[2:3] * v)
    return b_gate * conv


def _swa(q, kv_prev_ref, kv_cur_ref, bias_ref, sinks_ref, layer, first):
    col = lax.broadcasted_iota(jnp.int32, (BLOCK, 2 * BLOCK), 1)
    kill_prev = jnp.logical_and(first, col < BLOCK)
    zeros = jnp.zeros((HEAD_DIM, 2 * BLOCK), BF16)

    def scores(n, hk):
        if n == 0:
            keys = jnp.concatenate([kv_prev_ref[...], kv_cur_ref[:, 0:BLOCK]], axis=1)
        else:
            keys = kv_cur_ref[:, (n - 1) * BLOCK:(n + 1) * BLOCK]
        kt = keys[hk * HEAD_DIM:(hk + 1) * HEAD_DIM, :]
        vt = keys[KV_WIDTH + hk * HEAD_DIM:KV_WIDTH + (hk + 1) * HEAD_DIM, :]
        k_lo = jnp.concatenate([kt, zeros], axis=0)
        k_hi = jnp.concatenate([zeros, kt], axis=0)
        logits = []
        for g in range(GROUP):
            head = hk * GROUP + g
            grp, pos = head // 2, head % 2
            qg = q[n * BLOCK:(n + 1) * BLOCK, grp * LANES:(grp + 1) * LANES]
            bias = bias_ref[head]
            if n == 0:
                bias = jnp.where(kill_prev, NEG, bias)
            logits.append(_dot(qg, k_hi if pos else k_lo) + bias)
        return logits, vt

    def attend(hk, logits, vt):
        ps, invs = [], []
        for g in range(GROUP):
            sink = sinks_ref[layer, hk * GROUP + g]
            m = jnp.maximum(jnp.max(logits[g], axis=-1, keepdims=True), sink)
            p = jnp.exp(logits[g] - m)
            s = jnp.sum(p, axis=-1, keepdims=True) + jnp.exp(sink - m)
            ps.append(p.astype(BF16))
            invs.append(1.0 / s)
        o3 = _dot_nt(jnp.concatenate(ps, axis=0), vt)
        return [o3[g * BLOCK:(g + 1) * BLOCK] * invs[g] for g in range(GROUP)]

    units = [(i, hk) for i in range(q.shape[0] // BLOCK) for hk in range(N_KV_HEADS)]
    outs = {i: [] for i, _ in units}
    nxt = scores(*units[0])
    for k, (i, hk) in enumerate(units):
        cur = nxt
        if k + 1 < len(units):
            nxt = scores(*units[k + 1])
        yield
        outs[i] += attend(hk, *cur)
    return jnp.concatenate([jnp.concatenate(outs[i], axis=-1) for i in sorted(outs)], axis=0)


def _ffn(x, h, wg_ref, wu_ref, wd_ref):
    acc = x
    pending = None
    for c in range(D_FF // MXU_DIM):
        sl = slice(c * MXU_DIM, (c + 1) * MXU_DIM)
        gate = _dot(h, wg_ref[:, sl])
        yield
        up = _dot(h, wu_ref[:, sl])
        yield
        if pending is not None:
            acc = acc + _dot(pending[0], wd_ref[pending[1], :])
        pending = ((gate * jax.nn.sigmoid(gate) * up).astype(BF16), sl)
        yield
    return acc + _dot(pending[0], wd_ref[pending[1], :])


def _interleave(ffn, n_ffn, mix, n_mix):
    live = {"ffn": ffn, "mix": mix}

    def step(name):
        if live[name] is not None:
            try:
                next(live[name])
            except StopIteration:
                live[name] = None

    done = 0
    for i in range(n_ffn - 1):
        step("ffn")
        while done * (n_ffn - 1) < (i + 1) * n_mix:
            step("mix")
            done += 1
    while live["ffn"] is not None or live["mix"] is not None:
        step("ffn")
        step("mix")


def _layer_kernel(*refs, mixer, epilogue, layer, tm, tiles_per_batch):
    it = iter(refs)
    x_ref, gmix_ref, win_ref, mkt_ref, mv_ref, wout_ref = (next(it) for _ in range(6))
    gffn_ref, wg_ref, wu_ref, wd_ref = (next(it) for _ in range(4))
    if mixer == "conv":
        convw_ref = next(it)
    else:
        kv_prev_ref, kv_cur_ref, bias_ref, sinks_ref = (next(it) for _ in range(4))
    if epilogue == "kv":
        gkv_ref, wkvt_ref = next(it), next(it)
    elif epilogue == "final":
        gfin_ref = next(it)
    out_ref = next(it)
    if epilogue == "kv":
        kvt_ref = next(it)
    x1_ref, h2_ref = next(it), next(it)
    if mixer == "conv":
        vs_ref = next(it)

    t = pl.program_id(0)
    n_tiles = pl.num_programs(0) - 1
    wslot = t % 2
    rslot = 1 - wslot
    first = (jnp.minimum(t, n_tiles - 1) % tiles_per_batch) == 0

    @pl.when(t == 0)
    def _():
        x1_ref[1] = jnp.zeros((tm, D_MODEL), F32)
        h2_ref[1] = jnp.zeros((tm, D_MODEL), BF16)

    if mixer == "conv":
        @pl.when(first)
        def _():
            vs_ref[0:SUBLANES, :] = jnp.zeros((SUBLANES, CONV_WIDTH), F32)

    def mix():
        x = x_ref[...]
        h = _rms(x, gmix_ref[...]).astype(BF16)
        yield
        widths = (CONV_WIDTH,) * 3 + (MEM_WIDTH,) if mixer == "conv" else (Q_WIDTH, MEM_WIDTH)
        fields, lo = [], 0
        for w in widths:
            fields.append(_dot(h, win_ref[:, lo:lo + w]))
            lo += w
            yield
        if mixer == "conv":
            u, b_gate, c_gate, qm = fields
            y_tok = _short_conv(u, b_gate, c_gate, convw_ref[...], vs_ref, tm)
            yield
        else:
            q, qm = fields
            y_tok = yield from _swa(q.astype(BF16), kv_prev_ref, kv_cur_ref, bias_ref, sinks_ref,
                                    layer, first)
        y_mem = yield from _mem_attention(qm.astype(BF16), mkt_ref, mv_ref)
        y = jnp.concatenate([y_tok, y_mem], axis=-1).astype(BF16)
        half = D_MODEL // 2
        lo_half = x[:, :half] + _dot(y, wout_ref[:, :half])
        yield
        hi_half = x[:, half:] + _dot(y, wout_ref[:, half:])
        yield
        x = jnp.concatenate([lo_half, hi_half], axis=-1)
        x1_ref[wslot] = x
        h2_ref[wslot] = _rms(x, gffn_ref[...]).astype(BF16)

    def ffn():
        x = yield from _ffn(x1_ref[rslot], h2_ref[rslot], wg_ref, wu_ref, wd_ref)
        if epilogue == "kv":
            hk = _rms(x, gkv_ref[...]).astype(BF16)
            kvt = _dot_nt(wkvt_ref[...], hk)
            row = lax.broadcasted_iota(jnp.int32, (2 * KV_WIDTH, 1), 0)
            kvt_ref[...] = (kvt * jnp.where(row < KV_WIDTH, SCALE, 1.0)).astype(BF16)
        if epilogue == "final":
            x = _rms(x, gfin_ref[...])
        out_ref[...] = x

    n_mix = 14 if mixer == "conv" else 11 + (tm // BLOCK) * N_KV_HEADS
    _interleave(ffn(), 3 * (D_FF // MXU_DIM) + 1, mix(), n_mix)

    if mixer == "conv":
        vs_ref[0:SUBLANES, :] = vs_ref[tm:tm + SUBLANES, :]


def _const_spec(shape, index):
    return pl.BlockSpec(shape, lambda t: index, pipeline_mode=pl.Buffered(1))


def _layer(x, *, mixer, epilogue, layer, sub, p):
    batch, seq, _ = x.shape
    tm = TOKEN_TILE
    assert seq % tm == 0 and tm % BLOCK == 0
    tpb = seq // tm
    n_tiles = batch * tpb
    proj_w = A_PROJ if mixer == "conv" else B_PROJ
    blocks_per_tile = tm // BLOCK

    def mix_tile(t):
        tt = jnp.minimum(t, n_tiles - 1)
        return tt // tpb, tt % tpb

    def ffn_tile(t):
        tt = jnp.maximum(t - 1, 0)
        return tt // tpb, tt % tpb

    def x_in_map(t):
        b, j = mix_tile(t)
        return b, j, 0

    def x_out_map(t):
        b, j = ffn_tile(t)
        return b, j, 0

    def mem_map(t):
        return layer, mix_tile(t)[0], 0, 0, 0

    def kv_prev_map(t):
        b, j = mix_tile(t)
        return b, 0, jnp.maximum(j * blocks_per_tile - 1, 0)

    def kv_cur_map(t):
        b, j = mix_tile(t)
        return b, 0, j

    def kv_out_map(t):
        b, j = ffn_tile(t)
        return b, 0, j

    mem_spec = pl.BlockSpec((None, None, N_MEM_HEADS, MEM_WIDTH, N_MEM), mem_map)
    in_specs = [
        pl.BlockSpec((None, tm, D_MODEL), x_in_map),
        _const_spec((None, 1, D_MODEL), (layer, 0, 0)),
        _const_spec((None, D_MODEL, proj_w), (sub, 0, 0)),
        mem_spec, mem_spec,
        _const_spec((None, D_MODEL, D_MODEL), (sub, 0, 0)),
        _const_spec((None, 1, D_MODEL), (layer, 0, 0)),
        _const_spec((None, D_MODEL, D_FF), (layer, 0, 0)),
        _const_spec((None, D_MODEL, D_FF), (layer, 0, 0)),
        _const_spec((None, D_FF, D_MODEL), (layer, 0, 0)),
    ]
    args = [x, p["norm_mix"], p["w_in"][mixer], p["mkt"], p["mv"], p["w_out"][mixer],
            p["norm_ffn"], p["w_gate"], p["w_up"], p["w_down"]]
    scratch = [pltpu.VMEM((2, tm, D_MODEL), F32), pltpu.VMEM((2, tm, D_MODEL), BF16)]
    if mixer == "conv":
        in_specs.append(_const_spec((None, CONV_K, CONV_WIDTH), (sub, 0, 0)))
        args.append(p["conv_w"])
        scratch.append(pltpu.VMEM((tm + SUBLANES, CONV_WIDTH), F32))
    else:
        in_specs += [
            pl.BlockSpec((None, 2 * KV_WIDTH, BLOCK), kv_prev_map),
            pl.BlockSpec((None, 2 * KV_WIDTH, tm), kv_cur_map),
            _const_spec((N_Q_HEADS, BLOCK, 2 * BLOCK), (0, 0, 0)),
            pl.BlockSpec(memory_space=pltpu.SMEM),
        ]
        args += [p["kvt"], p["kvt"], p["bias"], p["sinks"]]
    if epilogue == "kv":
        in_specs += [_const_spec((1, D_MODEL), (0, 0)),
                     _const_spec((2 * KV_WIDTH, D_MODEL), (0, 0))]
        args += [p["kv_norm"], p["w_kvt"]]
    elif epilogue == "final":
        in_specs.append(_const_spec((1, D_MODEL), (0, 0)))
        args.append(p["final_norm"])

    out_specs = [pl.BlockSpec((None, tm, D_MODEL), x_out_map)]
    out_shape = [jax.ShapeDtypeStruct(x.shape, F32)]
    if epilogue == "kv":
        out_specs.append(pl.BlockSpec((None, 2 * KV_WIDTH, tm), kv_out_map))
        out_shape.append(jax.ShapeDtypeStruct((batch, 2 * KV_WIDTH, seq), BF16))

    outs = pl.pallas_call(
        functools.partial(_layer_kernel, mixer=mixer, epilogue=epilogue, layer=sub, tm=tm,
                          tiles_per_batch=tpb),
        grid=(n_tiles + 1,),
        in_specs=in_specs,
        out_specs=out_specs,
        out_shape=out_shape,
        scratch_shapes=scratch,
        compiler_params=pltpu.CompilerParams(
            dimension_semantics=("arbitrary",),
            vmem_limit_bytes=VMEM_LIMIT_BYTES),
        name=f"layer{layer}_{mixer}",
    )(*args)
    return outs


def kernel(x, mem, norm_mix, norm_ffn, a_w_in, a_conv_w, a_w_out, kv_norm, w_kv, b_w_q, b_sinks,
           b_w_out, rel_bias, mem_norm, w_mem_kv, w_gate, w_up, w_down, final_norm):
    mkt, mv = _mem_kv(mem, mem_norm, w_mem_kv.astype(BF16))
    p = {
        "norm_mix": norm_mix.reshape(DEPTH, 1, D_MODEL),
        "norm_ffn": norm_ffn.reshape(DEPTH, 1, D_MODEL),
        "w_in": {"conv": a_w_in.astype(BF16), "swa": b_w_q.astype(BF16)},
        "w_out": {"conv": a_w_out.astype(BF16), "swa": b_w_out.astype(BF16)},
        "conv_w": a_conv_w,
        "mkt": mkt, "mv": mv,
        "w_gate": w_gate.astype(BF16), "w_up": w_up.astype(BF16), "w_down": w_down.astype(BF16),
        "kv_norm": kv_norm.reshape(1, D_MODEL),
        "w_kvt": w_kv.T.astype(BF16),
        "final_norm": final_norm.reshape(1, D_MODEL),
        "sinks": b_sinks.astype(F32),
        "bias": _band_bias(rel_bias),
    }
    for i in range(DEPTH):
        mixer = "conv" if i < N_A else "swa"
        sub = i if i < N_A else i - N_A
        epilogue = "kv" if i == N_A - 1 else ("final" if i == DEPTH - 1 else "none")
        outs = _layer(x, mixer=mixer, epilogue=epilogue, layer=i, sub=sub, p=p)
        x = outs[0]
        if epilogue == "kv":
            p["kvt"] = outs[1]
    return x
```

```python
import functools
import math

import jax
import jax.numpy as jnp
from jax import lax
from jax.experimental import pallas as pl
from jax.experimental.pallas import tpu as pltpu

D_MODEL = 1024
DEPTH = 4
N_MEM = 256
HEAD_DIM = 64
N_MEM_HEADS = 4
MEM_WIDTH = N_MEM_HEADS * HEAD_DIM
CONV_WIDTH = D_MODEL - MEM_WIDTH
CONV_K = 3
N_Q_HEADS = CONV_WIDTH // HEAD_DIM
N_KV_HEADS = 4
GROUP = N_Q_HEADS // N_KV_HEADS
Q_WIDTH = N_Q_HEADS * HEAD_DIM
KV_WIDTH = N_KV_HEADS * HEAD_DIM
A_PROJ = 3 * CONV_WIDTH + MEM_WIDTH
B_PROJ = Q_WIDTH + MEM_WIDTH
WINDOW = 128
BLOCK = 128
REL_BUCKETS = 32
REL_MAX_DIST = 128
D_FF = ((8 * D_MODEL + 3 * 256 - 1) // (3 * 256)) * 256
N_A = DEPTH // 2
N_B = DEPTH - N_A
EPS = 1e-5

SCALE = HEAD_DIM ** -0.5
NEG = -1e30
LANES = 128
SUBLANES = 8
MXU_DIM = 256
TOKEN_TILE = 512
VMEM_LIMIT_BYTES = 56 * 1024 * 1024

BF16 = jnp.bfloat16
F32 = jnp.float32


def _dot(a, b):
    return jnp.dot(a, b, preferred_element_type=F32)


def _dot_nt(a, b):
    return lax.dot_general(a, b, (((1,), (1,)), ((), ())), preferred_element_type=F32)


def _rms(x, g):
    ms = jnp.mean(x * x, axis=-1, keepdims=True)
    return x * lax.rsqrt(ms + EPS) * g


def _mem_kv_kernel(mem_ref, g_ref, w_ref, kt_ref, v_ref, *, batch):
    mem_n = _rms(mem_ref[...], g_ref[...]).astype(BF16)
    kv = _dot(mem_n, w_ref[...])
    row = lax.broadcasted_iota(jnp.int32, (MEM_WIDTH, N_MEM), 0)
    col = lax.broadcasted_iota(jnp.int32, (N_MEM, MEM_WIDTH), 1)
    for b in range(batch):
        kvb = kv[b * N_MEM:(b + 1) * N_MEM]
        kt = (kvb[:, :MEM_WIDTH] * SCALE).T
        v = kvb[:, MEM_WIDTH:]
        for h in range(N_MEM_HEADS):
            lo, hi = h * HEAD_DIM, (h + 1) * HEAD_DIM
            kt_ref[b, h] = jnp.where((row >= lo) & (row < hi), kt, 0.0).astype(BF16)
            v_ref[b, h] = jnp.where((col >= lo) & (col < hi), v, 0.0).astype(BF16)


def _mem_kv(mem, mem_norm, w_mem_kv_bf16):
    batch = mem.shape[0]
    out_sds = jax.ShapeDtypeStruct((DEPTH, batch, N_MEM_HEADS, MEM_WIDTH, N_MEM), BF16)
    out_spec = pl.BlockSpec((None, batch, N_MEM_HEADS, MEM_WIDTH, N_MEM),
                            lambda i: (i, 0, 0, 0, 0))
    return pl.pallas_call(
        functools.partial(_mem_kv_kernel, batch=batch),
        grid=(DEPTH,),
        in_specs=[
            pl.BlockSpec((batch * N_MEM, D_MODEL), lambda i: (0, 0)),
            pl.BlockSpec((1, D_MODEL), lambda i: (0, 0)),
            pl.BlockSpec((None, D_MODEL, 2 * MEM_WIDTH), lambda i: (i, 0, 0)),
        ],
        out_specs=[out_spec, out_spec],
        out_shape=[out_sds, out_sds],
        compiler_params=pltpu.CompilerParams(dimension_semantics=("arbitrary",)),
        name="mem_kv",
    )(mem.reshape(batch * N_MEM, D_MODEL), mem_norm.reshape(1, D_MODEL), w_mem_kv_bf16)


def _bias_kernel(bucket_ref, inwin_ref, table_ref, out_ref):
    bucket = bucket_ref[...]
    inwin = inwin_ref[...] > 0
    for h in range(N_Q_HEADS):
        def body(b, acc, h=h):
            return jnp.where(bucket == b, table_ref[b, h], acc)
        acc = lax.fori_loop(0, REL_BUCKETS, body, jnp.zeros((BLOCK, 2 * BLOCK), F32))
        out_ref[h] = jnp.where(inwin, acc, NEG)


def _rel_bucket(dist):
    max_exact = REL_BUCKETS // 2
    d = jnp.maximum(dist, 1).astype(F32)
    large = max_exact + (jnp.log(d / max_exact) / math.log(REL_MAX_DIST / max_exact)
                         * (REL_BUCKETS - max_exact)).astype(jnp.int32)
    large = jnp.minimum(large, REL_BUCKETS - 1)
    return jnp.where(dist < max_exact, dist, large)


def _band_bias(rel_bias):
    qi = jnp.arange(BLOCK, dtype=jnp.int32)[:, None]
    kj = jnp.arange(2 * BLOCK, dtype=jnp.int32)[None, :]
    dist = qi + BLOCK - kj
    inwin = ((dist >= 0) & (dist < WINDOW)).astype(jnp.int32)
    bucket = _rel_bucket(jnp.maximum(dist, 0)).astype(jnp.int32)
    return pl.pallas_call(
        _bias_kernel,
        in_specs=[
            pl.BlockSpec(memory_space=pltpu.VMEM),
            pl.BlockSpec(memory_space=pltpu.VMEM),
            pl.BlockSpec(memory_space=pltpu.SMEM),
        ],
        out_specs=pl.BlockSpec(memory_space=pltpu.VMEM),
        out_shape=jax.ShapeDtypeStruct((N_Q_HEADS, BLOCK, 2 * BLOCK), F32),
        name="band_bias",
    )(bucket, inwin, rel_bias.astype(F32))


def _mem_attention(qm, mkt_ref, mv_ref):
    logits = [_dot(qm, mkt_ref[h]) for h in range(N_MEM_HEADS)]
    yield
    out = None
    for h in range(N_MEM_HEADS):
        m = jnp.max(logits[h], axis=-1, keepdims=True)
        p = jnp.exp(logits[h] - m)
        s = jnp.sum(p, axis=-1, keepdims=True)
        o = _dot(p.astype(BF16), mv_ref[h]) * (1.0 / s)
        out = o if out is None else out + o
        yield
    return out


def _short_conv(u, b_gate, c_gate, w, vs_ref, rows):
    v = c_gate * u
    base = SUBLANES
    vs_ref[base:base + rows, :] = v
    conv = (w[0:1] * vs_ref[base - 2:base - 2 + rows, :]
            + w[1:2] * vs_ref[base - 1:base - 1 + rows, :]
            + w[2:3] * v)
    return b_gate * conv


def _swa(q, kv_prev_ref, kv_cur_ref, bias_ref, sinks_ref, layer, first):
    col = lax.broadcasted_iota(jnp.int32, (BLOCK, 2 * BLOCK), 1)
    kill_prev = jnp.logical_and(first, col < BLOCK)
    zeros = jnp.zeros((HEAD_DIM, 2 * BLOCK), BF16)

    def scores(n, hk):
        if n == 0:
            keys = jnp.concatenate([kv_prev_ref[...], kv_cur_ref[:, 0:BLOCK]], axis=1)
        else:
            keys = kv_cur_ref[:, (n - 1) * BLOCK:(n + 1) * BLOCK]
        kt = keys[hk * HEAD_DIM:(hk + 1) * HEAD_DIM, :]
        vt = keys[KV_WIDTH + hk * HEAD_DIM:KV_WIDTH + (hk + 1) * HEAD_DIM, :]
        k_lo = jnp.concatenate([kt, zeros], axis=0)
        k_hi = jnp.concatenate([zeros, kt], axis=0)
        logits = []
        for g in range(GROUP):
            head = hk * GROUP + g
            grp, pos = head // 2, head % 2
            qg = q[n * BLOCK:(n + 1) * BLOCK, grp * LANES:(grp + 1) * LANES]
            bias = bias_ref[head]
            if n == 0:
                bias = jnp.where(kill_prev, NEG, bias)
            logits.append(_dot(qg, k_hi if pos else k_lo) + bias)
        return logits, vt

    def attend(hk, logits, vt):
        ps, invs = [], []
        for g in range(GROUP):
            sink = sinks_ref[layer, hk * GROUP + g]
            m = jnp.maximum(jnp.max(logits[g], axis=-1, keepdims=True), sink)
            p = jnp.exp(logits[g] - m)
            s = jnp.sum(p, axis=-1, keepdims=True) + jnp.exp(sink - m)
            ps.append(p.astype(BF16))
            invs.append(1.0 / s)
        o3 = _dot_nt(jnp.concatenate(ps, axis=0), vt)
        return [o3[g * BLOCK:(g + 1) * BLOCK] * invs[g] for g in range(GROUP)]

    units = [(i, hk) for i in range(q.shape[0] // BLOCK) for hk in range(N_KV_HEADS)]
    outs = {i: [] for i, _ in units}
    nxt = scores(*units[0])
    for k, (i, hk) in enumerate(units):
        cur = nxt
        if k + 1 < len(units):
            nxt = scores(*units[k + 1])
        yield
        outs[i] += attend(hk, *cur)
    return jnp.concatenate([jnp.concatenate(outs[i], axis=-1) for i in sorted(outs)], axis=0)


def _ffn(x_ref, h_ref, xf_ref, wg_ref, wu_ref, wd_ref):
    h = h_ref[...]
    acts = []
    for c in range(D_FF // MXU_DIM):
        sl = slice(c * MXU_DIM, (c + 1) * MXU_DIM)
        gate = _dot(h, wg_ref[:, sl])
        yield
        up = _dot(h, wu_ref[:, sl])
        acts.append((gate * jax.nn.sigmoid(gate) * up).astype(BF16))
        if c < D_MODEL // MXU_DIM:
            xf_ref[:, sl] = x_ref[:, sl]
        yield
    yield "tail"
    act = jnp.concatenate(acts, axis=-1)
    for n in range(D_MODEL // MXU_DIM):
        cs = slice(n * MXU_DIM, (n + 1) * MXU_DIM)
        yield cs, xf_ref[:, cs] + _dot(act, wd_ref[:, cs])


def _interleave(ffn, n_ffn, mix, n_mix):
    def advance(gen):
        try:
            return next(gen)
        except StopIteration:
            return "end"

    f_state = m_state = None
    done = 0
    for i in range(n_ffn):
        if f_state is None:
            f_state = advance(ffn)
        while m_state is None and done * n_ffn < (i + 1) * n_mix:
            m_state = advance(mix)
            done += 1
    while f_state is None:
        f_state = advance(ffn)
    while m_state is None:
        m_state = advance(mix)
    assert (f_state, m_state) == ("tail", "tail")
    advance(ffn)
    assert advance(mix) == "end"
    while advance(ffn) != "end":
        pass


def _layer_kernel(*refs, mixer, epilogue, layer, tm, tiles_per_batch):
    it = iter(refs)
    x_ref, gmix_ref, win_ref, mkt_ref, mv_ref, wout_ref = (next(it) for _ in range(6))
    gffn_ref, wg_ref, wu_ref, wd_ref = (next(it) for _ in range(4))
    if mixer == "conv":
        convw_ref = next(it)
    else:
        kv_prev_ref, kv_cur_ref, bias_ref, sinks_ref = (next(it) for _ in range(4))
    if epilogue == "kv":
        gkv_ref, wkvt_ref = next(it), next(it)
    elif epilogue == "final":
        gfin_ref = next(it)
    out_ref = next(it)
    if epilogue == "kv":
        kvt_ref = next(it)
    x1_ref, h2_ref, xf_ref = next(it), next(it), next(it)
    if mixer == "conv":
        vs_ref = next(it)

    t = pl.program_id(0)
    n_tiles = pl.num_programs(0) - 1
    first = (jnp.minimum(t, n_tiles - 1) % tiles_per_batch) == 0

    @pl.when(t == 0)
    def _():
        x1_ref[...] = jnp.zeros((tm, D_MODEL), F32)
        h2_ref[...] = jnp.zeros((tm, D_MODEL), BF16)

    if mixer == "conv":
        @pl.when(first)
        def _():
            vs_ref[0:SUBLANES, :] = jnp.zeros((SUBLANES, CONV_WIDTH), F32)

    def mix():
        x = x_ref[...]
        h = _rms(x, gmix_ref[...]).astype(BF16)
        yield
        widths = (CONV_WIDTH,) * 3 + (MEM_WIDTH,) if mixer == "conv" else (Q_WIDTH, MEM_WIDTH)
        fields, lo = [], 0
        for w in widths:
            fields.append(_dot(h, win_ref[:, lo:lo + w]))
            lo += w
            yield
        if mixer == "conv":
            u, b_gate, c_gate, qm = fields
            y_tok = _short_conv(u, b_gate, c_gate, convw_ref[...], vs_ref, tm)
            yield
        else:
            q, qm = fields
            y_tok = yield from _swa(q.astype(BF16), kv_prev_ref, kv_cur_ref, bias_ref, sinks_ref,
                                    layer, first)
        y_mem = yield from _mem_attention(qm.astype(BF16), mkt_ref, mv_ref)
        y = jnp.concatenate([y_tok, y_mem], axis=-1).astype(BF16)
        half = D_MODEL // 2
        lo_half = x[:, :half] + _dot(y, wout_ref[:, :half])
        yield
        hi_half = x[:, half:] + _dot(y, wout_ref[:, half:])
        yield "tail"
        x = jnp.concatenate([lo_half, hi_half], axis=-1)
        x1_ref[...] = x
        h2_ref[...] = _rms(x, gffn_ref[...]).astype(BF16)

    def ffn():
        blocks = []
        for item in _ffn(x1_ref, h2_ref, xf_ref, wg_ref, wu_ref, wd_ref):
            if isinstance(item, tuple):
                cs, block = item
                if epilogue == "none":
                    out_ref[:, cs] = block
                blocks.append(block)
                yield
            else:
                yield item
        if epilogue == "none":
            return
        x = jnp.concatenate(blocks, axis=-1)
        if epilogue == "kv":
            hk = _rms(x, gkv_ref[...]).astype(BF16)
            kvt = _dot_nt(wkvt_ref[...], hk)
            row = lax.broadcasted_iota(jnp.int32, (2 * KV_WIDTH, 1), 0)
            kvt_ref[...] = (kvt * jnp.where(row < KV_WIDTH, SCALE, 1.0)).astype(BF16)
        if epilogue == "final":
            x = _rms(x, gfin_ref[...])
        out_ref[...] = x

    n_mix = 13 if mixer == "conv" else 10 + (tm // BLOCK) * N_KV_HEADS
    _interleave(ffn(), 2 * (D_FF // MXU_DIM), mix(), n_mix)

    if mixer == "conv":
        vs_ref[0:SUBLANES, :] = vs_ref[tm:tm + SUBLANES, :]


def _const_spec(shape, index):
    return pl.BlockSpec(shape, lambda t: index, pipeline_mode=pl.Buffered(1))


def _layer(x, *, mixer, epilogue, layer, sub, p):
    batch, seq, _ = x.shape
    tm = TOKEN_TILE
    assert seq % tm == 0 and tm % BLOCK == 0
    tpb = seq // tm
    n_tiles = batch * tpb
    proj_w = A_PROJ if mixer == "conv" else B_PROJ
    blocks_per_tile = tm // BLOCK

    def mix_tile(t):
        tt = jnp.minimum(t, n_tiles - 1)
        return tt // tpb, tt % tpb

    def ffn_tile(t):
        tt = jnp.maximum(t - 1, 0)
        return tt // tpb, tt % tpb

    def x_in_map(t):
        b, j = mix_tile(t)
        return b, j, 0

    def x_out_map(t):
        b, j = ffn_tile(t)
        return b, j, 0

    def mem_map(t):
        return layer, mix_tile(t)[0], 0, 0, 0

    def kv_prev_map(t):
        b, j = mix_tile(t)
        return b, 0, jnp.maximum(j * blocks_per_tile - 1, 0)

    def kv_cur_map(t):
        b, j = mix_tile(t)
        return b, 0, j

    def kv_out_map(t):
        b, j = ffn_tile(t)
        return b, 0, j

    mem_spec = pl.BlockSpec((None, None, N_MEM_HEADS, MEM_WIDTH, N_MEM), mem_map)
    in_specs = [
        pl.BlockSpec((None, tm, D_MODEL), x_in_map),
        _const_spec((None, 1, D_MODEL), (layer, 0, 0)),
        _const_spec((None, D_MODEL, proj_w), (sub, 0, 0)),
        mem_spec, mem_spec,
        _const_spec((None, D_MODEL, D_MODEL), (sub, 0, 0)),
        _const_spec((None, 1, D_MODEL), (layer, 0, 0)),
        _const_spec((None, D_MODEL, D_FF), (layer, 0, 0)),
        _const_spec((None, D_MODEL, D_FF), (layer, 0, 0)),
        _const_spec((None, D_FF, D_MODEL), (layer, 0, 0)),
    ]
    args = [x, p["norm_mix"], p["w_in"][mixer], p["mkt"], p["mv"], p["w_out"][mixer],
            p["norm_ffn"], p["w_gate"], p["w_up"], p["w_down"]]
    scratch = [pltpu.VMEM((tm, D_MODEL), F32), pltpu.VMEM((tm, D_MODEL), BF16),
               pltpu.VMEM((tm, D_MODEL), F32)]
    if mixer == "conv":
        in_specs.append(_const_spec((None, CONV_K, CONV_WIDTH), (sub, 0, 0)))
        args.append(p["conv_w"])
        scratch.append(pltpu.VMEM((tm + SUBLANES, CONV_WIDTH), F32))
    else:
        in_specs += [
            pl.BlockSpec((None, 2 * KV_WIDTH, BLOCK), kv_prev_map),
            pl.BlockSpec((None, 2 * KV_WIDTH, tm), kv_cur_map),
            _const_spec((N_Q_HEADS, BLOCK, 2 * BLOCK), (0, 0, 0)),
            pl.BlockSpec(memory_space=pltpu.SMEM),
        ]
        args += [p["kvt"], p["kvt"], p["bias"], p["sinks"]]
    if epilogue == "kv":
        in_specs += [_const_spec((1, D_MODEL), (0, 0)),
                     _const_spec((2 * KV_WIDTH, D_MODEL), (0, 0))]
        args += [p["kv_norm"], p["w_kvt"]]
    elif epilogue == "final":
        in_specs.append(_const_spec((1, D_MODEL), (0, 0)))
        args.append(p["final_norm"])

    out_specs = [pl.BlockSpec((None, tm, D_MODEL), x_out_map)]
    out_shape = [jax.ShapeDtypeStruct(x.shape, F32)]
    if epilogue == "kv":
        out_specs.append(pl.BlockSpec((None, 2 * KV_WIDTH, tm), kv_out_map))
        out_shape.append(jax.ShapeDtypeStruct((batch, 2 * KV_WIDTH, seq), BF16))

    outs = pl.pallas_call(
        functools.partial(_layer_kernel, mixer=mixer, epilogue=epilogue, layer=sub, tm=tm,
                          tiles_per_batch=tpb),
        grid=(n_tiles + 1,),
        in_specs=in_specs,
        out_specs=out_specs,
        out_shape=out_shape,
        scratch_shapes=scratch,
        compiler_params=pltpu.CompilerParams(
            dimension_semantics=("arbitrary",),
            vmem_limit_bytes=VMEM_LIMIT_BYTES),
        name=f"layer{layer}_{mixer}",
    )(*args)
    return outs


def kernel(x, mem, norm_mix, norm_ffn, a_w_in, a_conv_w, a_w_out, kv_norm, w_kv, b_w_q, b_sinks,
           b_w_out, rel_bias, mem_norm, w_mem_kv, w_gate, w_up, w_down, final_norm):
    mkt, mv = _mem_kv(mem, mem_norm, w_mem_kv.astype(BF16))
    p = {
        "norm_mix": norm_mix.reshape(DEPTH, 1, D_MODEL),
        "norm_ffn": norm_ffn.reshape(DEPTH, 1, D_MODEL),
        "w_in": {"conv": a_w_in.astype(BF16), "swa": b_w_q.astype(BF16)},
        "w_out": {"conv": a_w_out.astype(BF16), "swa": b_w_out.astype(BF16)},
        "conv_w": a_conv_w,
        "mkt": mkt, "mv": mv,
        "w_gate": w_gate.astype(BF16), "w_up": w_up.astype(BF16), "w_down": w_down.astype(BF16),
        "kv_norm": kv_norm.reshape(1, D_MODEL),
        "w_kvt": w_kv.T.astype(BF16),
        "final_norm": final_norm.reshape(1, D_MODEL),
        "sinks": b_sinks.astype(F32),
        "bias": _band_bias(rel_bias),
    }
    for i in range(DEPTH):
        mixer = "conv" if i < N_A else "swa"
        sub = i if i < N_A else i - N_A
        epilogue = "kv" if i == N_A - 1 else ("final" if i == DEPTH - 1 else "none")
        outs = _layer(x, mixer=mixer, epilogue=epilogue, layer=i, sub=sub, p=p)
        x = outs[0]
        if epilogue == "kv":
            p["kvt"] = outs[1]
    return x
```

```python
import functools
import math

import jax
import jax.numpy as jnp
from jax import lax
from jax.experimental import pallas as pl
from jax.experimental.pallas import tpu as pltpu

D_MODEL = 1024
DEPTH = 4
N_MEM = 256
HEAD_DIM = 64
N_MEM_HEADS = 4
MEM_WIDTH = N_MEM_HEADS * HEAD_DIM
CONV_WIDTH = D_MODEL - MEM_WIDTH
CONV_K = 3
N_Q_HEADS = CONV_WIDTH // HEAD_DIM
N_KV_HEADS = 4
GROUP = N_Q_HEADS // N_KV_HEADS
Q_WIDTH = N_Q_HEADS * HEAD_DIM
KV_WIDTH = N_KV_HEADS * HEAD_DIM
A_PROJ = 3 * CONV_WIDTH + MEM_WIDTH
B_PROJ = Q_WIDTH + MEM_WIDTH
WINDOW = 128
BLOCK = 128
REL_BUCKETS = 32
REL_MAX_DIST = 128
D_FF = ((8 * D_MODEL + 3 * 256 - 1) // (3 * 256)) * 256
N_A = DEPTH // 2
N_B = DEPTH - N_A
EPS = 1e-5

SCALE = HEAD_DIM ** -0.5
NEG = -1e30
LANES = 128
SUBLANES = 8
MXU_DIM = 256
TOKEN_TILE = 512
VMEM_LIMIT_BYTES = 56 * 1024 * 1024

BF16 = jnp.bfloat16
F32 = jnp.float32


def _dot(a, b):
    return jnp.dot(a, b, preferred_element_type=F32)


def _dot_nt(a, b):
    return lax.dot_general(a, b, (((1,), (1,)), ((), ())), preferred_element_type=F32)


def _rms(x, g):
    ms = jnp.mean(x * x, axis=-1, keepdims=True)
    return x * lax.rsqrt(ms + EPS) * g


def _mem_kv_kernel(mem_ref, g_ref, w_ref, kt_ref, v_ref, *, batch):
    mem_n = _rms(mem_ref[...], g_ref[...]).astype(BF16)
    kv = _dot(mem_n, w_ref[...].astype(BF16))
    row = lax.broadcasted_iota(jnp.int32, (MEM_WIDTH, N_MEM), 0)
    col = lax.broadcasted_iota(jnp.int32, (N_MEM, MEM_WIDTH), 1)
    for b in range(batch):
        kvb = kv[b * N_MEM:(b + 1) * N_MEM]
        kt = (kvb[:, :MEM_WIDTH] * SCALE).T
        v = kvb[:, MEM_WIDTH:]
        for h in range(N_MEM_HEADS):
            lo, hi = h * HEAD_DIM, (h + 1) * HEAD_DIM
            kt_ref[b, h] = jnp.where((row >= lo) & (row < hi), kt, 0.0).astype(BF16)
            v_ref[b, h] = jnp.where((col >= lo) & (col < hi), v, 0.0).astype(BF16)


def _mem_kv(mem, mem_norm, w_mem_kv):
    batch = mem.shape[0]
    out_sds = jax.ShapeDtypeStruct((DEPTH, batch, N_MEM_HEADS, MEM_WIDTH, N_MEM), BF16)
    out_spec = pl.BlockSpec((None, batch, N_MEM_HEADS, MEM_WIDTH, N_MEM),
                            lambda i: (i, 0, 0, 0, 0))
    return pl.pallas_call(
        functools.partial(_mem_kv_kernel, batch=batch),
        grid=(DEPTH,),
        in_specs=[
            pl.BlockSpec((batch * N_MEM, D_MODEL), lambda i: (0, 0)),
            pl.BlockSpec((1, D_MODEL), lambda i: (0, 0)),
            pl.BlockSpec((None, D_MODEL, 2 * MEM_WIDTH), lambda i: (i, 0, 0)),
        ],
        out_specs=[out_spec, out_spec],
        out_shape=[out_sds, out_sds],
        compiler_params=pltpu.CompilerParams(dimension_semantics=("arbitrary",)),
        name="mem_kv",
    )(mem.reshape(batch * N_MEM, D_MODEL), mem_norm.reshape(1, D_MODEL), w_mem_kv)


def _bias_kernel(bucket_ref, inwin_ref, table_ref, out_ref):
    bucket = bucket_ref[...]
    inwin = inwin_ref[...] > 0
    for h in range(N_Q_HEADS):
        def body(b, acc, h=h):
            return jnp.where(bucket == b, table_ref[b, h], acc)
        acc = lax.fori_loop(0, REL_BUCKETS, body, jnp.zeros((BLOCK, 2 * BLOCK), F32))
        out_ref[h] = jnp.where(inwin, acc, NEG)


def _rel_bucket(dist):
    max_exact = REL_BUCKETS // 2
    d = jnp.maximum(dist, 1).astype(F32)
    large = max_exact + (jnp.log(d / max_exact) / math.log(REL_MAX_DIST / max_exact)
                         * (REL_BUCKETS - max_exact)).astype(jnp.int32)
    large = jnp.minimum(large, REL_BUCKETS - 1)
    return jnp.where(dist < max_exact, dist, large)


def _band_bias(rel_bias):
    qi = jnp.arange(BLOCK, dtype=jnp.int32)[:, None]
    kj = jnp.arange(2 * BLOCK, dtype=jnp.int32)[None, :]
    dist = qi + BLOCK - kj
    inwin = ((dist >= 0) & (dist < WINDOW)).astype(jnp.int32)
    bucket = _rel_bucket(jnp.maximum(dist, 0)).astype(jnp.int32)
    return pl.pallas_call(
        _bias_kernel,
        in_specs=[
            pl.BlockSpec(memory_space=pltpu.VMEM),
            pl.BlockSpec(memory_space=pltpu.VMEM),
            pl.BlockSpec(memory_space=pltpu.SMEM),
        ],
        out_specs=pl.BlockSpec(memory_space=pltpu.VMEM),
        out_shape=jax.ShapeDtypeStruct((N_Q_HEADS, BLOCK, 2 * BLOCK), F32),
        name="band_bias",
    )(bucket, inwin, rel_bias.astype(F32))


def _mem_attention(qm, mkt_ref, mv_ref):
    logits = [_dot(qm, mkt_ref[h]) for h in range(N_MEM_HEADS)]
    yield
    out = None
    for h in range(N_MEM_HEADS):
        m = jnp.max(logits[h], axis=-1, keepdims=True)
        p = jnp.exp(logits[h] - m)
        s = jnp.sum(p, axis=-1, keepdims=True)
        o = _dot(p.astype(BF16), mv_ref[h]) * (1.0 / s)
        out = o if out is None else out + o
        yield
    return out


def _short_conv(u, b_gate, c_gate, w, vs_ref, rows):
    v = c_gate * u
    base = SUBLANES
    vs_ref[base:base + rows, :] = v
    conv = (w[0:1] * vs_ref[base - 2:base - 2 + rows, :]
            + w[1:2] * vs_ref[base - 1:base - 1 + rows, :]
            + w[2:3] * v)
    return b_gate * conv


def _swa(q, kv_prev_ref, kv_cur_ref, bias_ref, sinks_ref, layer, first):
    col = lax.broadcasted_iota(jnp.int32, (BLOCK, 2 * BLOCK), 1)
    kill_prev = jnp.logical_and(first, col < BLOCK)
    zeros = jnp.zeros((HEAD_DIM, 2 * BLOCK), BF16)

    def scores(n, hk):
        if n == 0:
            keys = jnp.concatenate([kv_prev_ref[...], kv_cur_ref[:, 0:BLOCK]], axis=1)
        else:
            keys = kv_cur_ref[:, (n - 1) * BLOCK:(n + 1) * BLOCK]
        kt = keys[hk * HEAD_DIM:(hk + 1) * HEAD_DIM, :]
        vt = keys[KV_WIDTH + hk * HEAD_DIM:KV_WIDTH + (hk + 1) * HEAD_DIM, :]
        k_lo = jnp.concatenate([kt, zeros], axis=0)
        k_hi = jnp.concatenate([zeros, kt], axis=0)
        logits = []
        for g in range(GROUP):
            head = hk * GROUP + g
            grp, pos = head // 2, head % 2
            qg = q[n * BLOCK:(n + 1) * BLOCK, grp * LANES:(grp + 1) * LANES]
            bias = bias_ref[head]
            if n == 0:
                bias = jnp.where(kill_prev, NEG, bias)
            logits.append(_dot(qg, k_hi if pos else k_lo) + bias)
        return logits, vt

    def attend(hk, logits, vt):
        ps, invs = [], []
        for g in range(GROUP):
            sink = sinks_ref[layer, hk * GROUP + g]
            m = jnp.maximum(jnp.max(logits[g], axis=-1, keepdims=True), sink)
            p = jnp.exp(logits[g] - m)
            s = jnp.sum(p, axis=-1, keepdims=True) + jnp.exp(sink - m)
            ps.append(p.astype(BF16))
            invs.append(1.0 / s)
        o3 = _dot_nt(jnp.concatenate(ps, axis=0), vt)
        return [o3[g * BLOCK:(g + 1) * BLOCK] * invs[g] for g in range(GROUP)]

    units = [(i, hk) for i in range(q.shape[0] // BLOCK) for hk in range(N_KV_HEADS)]
    outs = {i: [] for i, _ in units}
    nxt = scores(*units[0])
    for k, (i, hk) in enumerate(units):
        cur = nxt
        if k + 1 < len(units):
            nxt = scores(*units[k + 1])
        yield
        outs[i] += attend(hk, *cur)
    return jnp.concatenate([jnp.concatenate(outs[i], axis=-1) for i in sorted(outs)], axis=0)


def _ffn(x_ref, h_ref, xf_ref, wg_ref, wu_ref, wd_ref):
    h = h_ref[...]
    acts = []
    for c in range(D_FF // MXU_DIM):
        sl = slice(c * MXU_DIM, (c + 1) * MXU_DIM)
        gate = _dot(h, wg_ref[:, sl])
        yield
        up = _dot(h, wu_ref[:, sl])
        acts.append((gate * jax.nn.sigmoid(gate) * up).astype(BF16))
        if c < D_MODEL // MXU_DIM:
            xf_ref[:, sl] = x_ref[:, sl]
        yield
    yield "tail"
    act = jnp.concatenate(acts, axis=-1)
    for n in range(D_MODEL // MXU_DIM):
        cs = slice(n * MXU_DIM, (n + 1) * MXU_DIM)
        yield cs, xf_ref[:, cs] + _dot(act, wd_ref[:, cs])


def _interleave(ffn, n_ffn, mix, n_mix):
    def advance(gen):
        try:
            return next(gen)
        except StopIteration:
            return "end"

    f_state = m_state = None
    done = 0
    for i in range(n_ffn):
        if f_state is None:
            f_state = advance(ffn)
        while m_state is None and done * n_ffn < (i + 1) * n_mix:
            m_state = advance(mix)
            done += 1
    while f_state is None:
        f_state = advance(ffn)
    while m_state is None:
        m_state = advance(mix)
    assert (f_state, m_state) == ("tail", "tail")
    advance(ffn)
    assert advance(mix) == "end"
    while advance(ffn) != "end":
        pass


def _layer_kernel(*refs, mixer, epilogue, layer, tm, tiles_per_batch, n_cast):
    it = iter(refs)
    x_ref, gmix_ref, win_ref, mkt_ref, mv_ref, wout_ref = (next(it) for _ in range(6))
    gffn_ref, wg_ref, wu_ref, wd_ref = (next(it) for _ in range(4))
    if mixer == "conv":
        convw_ref = next(it)
    else:
        kv_prev_ref, kv_cur_ref, bias_ref, sinks_ref = (next(it) for _ in range(4))
    if epilogue == "kv":
        gkv_ref, wkvt_ref = next(it), next(it)
    elif epilogue == "final":
        gfin_ref = next(it)
    cast_in = [next(it) for _ in range(n_cast)]
    out_ref = next(it)
    if epilogue == "kv":
        kvt_ref = next(it)
    cast_out = [next(it) for _ in range(n_cast)]
    x1_ref, h2_ref, xf_ref = next(it), next(it), next(it)
    if mixer == "conv":
        vs_ref = next(it)

    t = pl.program_id(0)
    n_tiles = pl.num_programs(0) - 1
    first = (jnp.minimum(t, n_tiles - 1) % tiles_per_batch) == 0

    @pl.when(t == 0)
    def _():
        x1_ref[...] = jnp.zeros((tm, D_MODEL), F32)
        h2_ref[...] = jnp.zeros((tm, D_MODEL), BF16)

    if mixer == "conv":
        @pl.when(first)
        def _():
            vs_ref[0:SUBLANES, :] = jnp.zeros((SUBLANES, CONV_WIDTH), F32)

    def mix():
        x = x_ref[...]
        h = _rms(x, gmix_ref[...]).astype(BF16)
        yield
        widths = (CONV_WIDTH,) * 3 + (MEM_WIDTH,) if mixer == "conv" else (Q_WIDTH, MEM_WIDTH)
        fields, lo = [], 0
        for w in widths:
            fields.append(_dot(h, win_ref[:, lo:lo + w]))
            lo += w
            yield
        for src, dst in zip(cast_in, cast_out):
            dst[...] = src[...].astype(BF16)
        if mixer == "conv":
            u, b_gate, c_gate, qm = fields
            y_tok = _short_conv(u, b_gate, c_gate, convw_ref[...], vs_ref, tm)
            yield
        else:
            q, qm = fields
            y_tok = yield from _swa(q.astype(BF16), kv_prev_ref, kv_cur_ref, bias_ref, sinks_ref,
                                    layer, first)
        y_mem = yield from _mem_attention(qm.astype(BF16), mkt_ref, mv_ref)
        y = jnp.concatenate([y_tok, y_mem], axis=-1).astype(BF16)
        half = D_MODEL // 2
        lo_half = x[:, :half] + _dot(y, wout_ref[:, :half])
        yield
        hi_half = x[:, half:] + _dot(y, wout_ref[:, half:])
        yield "tail"
        x = jnp.concatenate([lo_half, hi_half], axis=-1)
        x1_ref[...] = x
        h2_ref[...] = _rms(x, gffn_ref[...]).astype(BF16)

    def ffn():
        blocks = []
        for item in _ffn(x1_ref, h2_ref, xf_ref, wg_ref, wu_ref, wd_ref):
            if isinstance(item, tuple):
                cs, block = item
                if epilogue == "none":
                    out_ref[:, cs] = block
                blocks.append(block)
                yield
            else:
                yield item
        if epilogue == "none":
            return
        x = jnp.concatenate(blocks, axis=-1)
        if epilogue == "kv":
            hk = _rms(x, gkv_ref[...]).astype(BF16)
            kvt = _dot_nt(wkvt_ref[...], hk)
            row = lax.broadcasted_iota(jnp.int32, (2 * KV_WIDTH, 1), 0)
            kvt_ref[...] = (kvt * jnp.where(row < KV_WIDTH, SCALE, 1.0)).astype(BF16)
        if epilogue == "final":
            x = _rms(x, gfin_ref[...])
        out_ref[...] = x

    n_mix = 13 if mixer == "conv" else 10 + (tm // BLOCK) * N_KV_HEADS
    _interleave(ffn(), 2 * (D_FF // MXU_DIM), mix(), n_mix)

    if mixer == "conv":
        vs_ref[0:SUBLANES, :] = vs_ref[tm:tm + SUBLANES, :]


def _const_spec(shape, index):
    return pl.BlockSpec(shape, lambda t: index, pipeline_mode=pl.Buffered(1))


def _cast_rows(n_rows, n_steps):
    tile = 2 * SUBLANES
    return min(r for r in range(tile, n_rows + 1, tile)
               if n_rows % r == 0 and n_rows // r <= n_steps)


def _layer(x, *, mixer, epilogue, layer, sub, w, cast_next, p):
    batch, seq, _ = x.shape
    tm = TOKEN_TILE
    assert seq % tm == 0 and tm % BLOCK == 0
    tpb = seq // tm
    n_tiles = batch * tpb
    proj_w = A_PROJ if mixer == "conv" else B_PROJ
    blocks_per_tile = tm // BLOCK

    def mix_tile(t):
        tt = jnp.minimum(t, n_tiles - 1)
        return tt // tpb, tt % tpb

    def ffn_tile(t):
        tt = jnp.maximum(t - 1, 0)
        return tt // tpb, tt % tpb

    def x_in_map(t):
        b, j = mix_tile(t)
        return b, j, 0

    def x_out_map(t):
        b, j = ffn_tile(t)
        return b, j, 0

    def mem_map(t):
        return layer, mix_tile(t)[0], 0, 0, 0

    def kv_prev_map(t):
        b, j = mix_tile(t)
        return b, 0, jnp.maximum(j * blocks_per_tile - 1, 0)

    def kv_cur_map(t):
        b, j = mix_tile(t)
        return b, 0, j

    def kv_out_map(t):
        b, j = ffn_tile(t)
        return b, 0, j

    mem_spec = pl.BlockSpec((None, None, N_MEM_HEADS, MEM_WIDTH, N_MEM), mem_map)
    in_specs = [
        pl.BlockSpec((None, tm, D_MODEL), x_in_map),
        _const_spec((None, 1, D_MODEL), (layer, 0, 0)),
        _const_spec((D_MODEL, proj_w), (0, 0)),
        mem_spec, mem_spec,
        _const_spec((D_MODEL, D_MODEL), (0, 0)),
        _const_spec((None, 1, D_MODEL), (layer, 0, 0)),
        _const_spec((D_MODEL, D_FF), (0, 0)),
        _const_spec((D_MODEL, D_FF), (0, 0)),
        _const_spec((D_FF, D_MODEL), (0, 0)),
    ]
    args = [x, p["norm_mix"], w["in"], p["mkt"], p["mv"], w["out"],
            p["norm_ffn"], w["gate"], w["up"], w["down"]]
    scratch = [pltpu.VMEM((tm, D_MODEL), F32), pltpu.VMEM((tm, D_MODEL), BF16),
               pltpu.VMEM((tm, D_MODEL), F32)]
    if mixer == "conv":
        in_specs.append(_const_spec((None, CONV_K, CONV_WIDTH), (sub, 0, 0)))
        args.append(p["conv_w"])
        scratch.append(pltpu.VMEM((tm + SUBLANES, CONV_WIDTH), F32))
    else:
        in_specs += [
            pl.BlockSpec((None, 2 * KV_WIDTH, BLOCK), kv_prev_map),
            pl.BlockSpec((None, 2 * KV_WIDTH, tm), kv_cur_map),
            _const_spec((N_Q_HEADS, BLOCK, 2 * BLOCK), (0, 0, 0)),
            pl.BlockSpec(memory_space=pltpu.SMEM),
        ]
        args += [p["kvt"], p["kvt"], p["bias"], p["sinks"]]
    if epilogue == "kv":
        in_specs += [_const_spec((1, D_MODEL), (0, 0)),
                     _const_spec((2 * KV_WIDTH, D_MODEL), (0, 0))]
        args += [p["kv_norm"], p["w_kvt"]]
    elif epilogue == "final":
        in_specs.append(_const_spec((1, D_MODEL), (0, 0)))
        args.append(p["final_norm"])

    out_specs = [pl.BlockSpec((None, tm, D_MODEL), x_out_map)]
    out_shape = [jax.ShapeDtypeStruct(x.shape, F32)]
    if epilogue == "kv":
        out_specs.append(pl.BlockSpec((None, 2 * KV_WIDTH, tm), kv_out_map))
        out_shape.append(jax.ShapeDtypeStruct((batch, 2 * KV_WIDTH, seq), BF16))
    for stack, idx in cast_next:
        _, n_rows, n_cols = stack.shape
        rows = _cast_rows(n_rows, n_tiles)
        last = n_rows // rows - 1
        in_specs.append(pl.BlockSpec((None, rows, n_cols),
                                     lambda t, idx=idx, last=last: (idx, jnp.minimum(t, last), 0)))
        args.append(stack)
        out_specs.append(pl.BlockSpec((rows, n_cols),
                                      lambda t, last=last: (jnp.minimum(t, last), 0)))
        out_shape.append(jax.ShapeDtypeStruct((n_rows, n_cols), BF16))

    outs = pl.pallas_call(
        functools.partial(_layer_kernel, mixer=mixer, epilogue=epilogue, layer=sub, tm=tm,
                          tiles_per_batch=tpb, n_cast=len(cast_next)),
        grid=(n_tiles + 1,),
        in_specs=in_specs,
        out_specs=out_specs,
        out_shape=out_shape,
        scratch_shapes=scratch,
        compiler_params=pltpu.CompilerParams(
            dimension_semantics=("arbitrary",),
            vmem_limit_bytes=VMEM_LIMIT_BYTES),
        name=f"layer{layer}_{mixer}",
    )(*args)
    return outs


def kernel(x, mem, norm_mix, norm_ffn, a_w_in, a_conv_w, a_w_out, kv_norm, w_kv, b_w_q, b_sinks,
           b_w_out, rel_bias, mem_norm, w_mem_kv, w_gate, w_up, w_down, final_norm):
    mkt, mv = _mem_kv(mem, mem_norm, w_mem_kv)
    p = {
        "norm_mix": norm_mix.reshape(DEPTH, 1, D_MODEL),
        "norm_ffn": norm_ffn.reshape(DEPTH, 1, D_MODEL),
        "conv_w": a_conv_w,
        "mkt": mkt, "mv": mv,
        "kv_norm": kv_norm.reshape(1, D_MODEL),
        "w_kvt": w_kv.T.astype(BF16),
        "final_norm": final_norm.reshape(1, D_MODEL),
        "sinks": b_sinks.astype(F32),
        "bias": _band_bias(rel_bias),
    }

    def f32_weights(i):
        stack_in, stack_out, sub = (a_w_in, a_w_out, i) if i < N_A else (b_w_q, b_w_out, i - N_A)
        return {"in": (stack_in, sub), "out": (stack_out, sub),
                "gate": (w_gate, i), "up": (w_up, i), "down": (w_down, i)}

    w = {k: stack[idx].astype(BF16) for k, (stack, idx) in f32_weights(0).items()}
    for i in range(DEPTH):
        mixer = "conv" if i < N_A else "swa"
        sub = i if i < N_A else i - N_A
        epilogue = "kv" if i == N_A - 1 else ("final" if i == DEPTH - 1 else "none")
        nxt = f32_weights(i + 1) if i + 1 < DEPTH else {}
        outs = _layer(x, mixer=mixer, epilogue=epilogue, layer=i, sub=sub, w=w,
                      cast_next=list(nxt.values()), p=p)
        x = outs[0]
        if epilogue == "kv":
            p["kvt"] = outs[1]
        w = dict(zip(nxt.keys(), outs[len(outs) - len(nxt):]))
    return x
```

```python
import functools
import math

import jax
import jax.numpy as jnp
from jax import lax
from jax.experimental import pallas as pl
from jax.experimental.pallas import tpu as pltpu

D_MODEL = 1024
DEPTH = 4
N_MEM = 256
HEAD_DIM = 64
N_MEM_HEADS = 4
MEM_WIDTH = N_MEM_HEADS * HEAD_DIM
CONV_WIDTH = D_MODEL - MEM_WIDTH
CONV_K = 3
N_Q_HEADS = CONV_WIDTH // HEAD_DIM
N_KV_HEADS = 4
GROUP = N_Q_HEADS // N_KV_HEADS
Q_WIDTH = N_Q_HEADS * HEAD_DIM
KV_WIDTH = N_KV_HEADS * HEAD_DIM
A_PROJ = 3 * CONV_WIDTH + MEM_WIDTH
B_PROJ = Q_WIDTH + MEM_WIDTH
WINDOW = 128
BLOCK = 128
REL_BUCKETS = 32
REL_MAX_DIST = 128
D_FF = ((8 * D_MODEL + 3 * 256 - 1) // (3 * 256)) * 256
N_A = DEPTH // 2
N_B = DEPTH - N_A
EPS = 1e-5

SCALE = HEAD_DIM ** -0.5
NEG = -1e30
LANES = 128
SUBLANES = 8
MXU_DIM = 256
TOKEN_TILE = 512
VMEM_LIMIT_BYTES = 56 * 1024 * 1024

BF16 = jnp.bfloat16
F32 = jnp.float32


def _dot(a, b):
    return jnp.dot(a, b, preferred_element_type=F32)


def _dot_nt(a, b):
    return lax.dot_general(a, b, (((1,), (1,)), ((), ())), preferred_element_type=F32)


def _rms(x, g):
    ms = jnp.mean(x * x, axis=-1, keepdims=True)
    return x * lax.rsqrt(ms + EPS) * g


def _mem_kv_kernel(mem_ref, g_ref, w_ref, kt_ref, v_ref, *, batch):
    mem_n = _rms(mem_ref[...], g_ref[...]).astype(BF16)
    kv = _dot(mem_n, w_ref[...].astype(BF16))
    row = lax.broadcasted_iota(jnp.int32, (MEM_WIDTH, N_MEM), 0)
    col = lax.broadcasted_iota(jnp.int32, (N_MEM, MEM_WIDTH), 1)
    for b in range(batch):
        kvb = kv[b * N_MEM:(b + 1) * N_MEM]
        kt = (kvb[:, :MEM_WIDTH] * SCALE).T
        v = kvb[:, MEM_WIDTH:]
        for h in range(N_MEM_HEADS):
            lo, hi = h * HEAD_DIM, (h + 1) * HEAD_DIM
            kt_ref[b, h] = jnp.where((row >= lo) & (row < hi), kt, 0.0).astype(BF16)
            v_ref[b, h] = jnp.where((col >= lo) & (col < hi), v, 0.0).astype(BF16)


def _mem_kv(mem, mem_norm, w_mem_kv):
    batch = mem.shape[0]
    out_sds = jax.ShapeDtypeStruct((DEPTH, batch, N_MEM_HEADS, MEM_WIDTH, N_MEM), BF16)
    out_spec = pl.BlockSpec((None, batch, N_MEM_HEADS, MEM_WIDTH, N_MEM),
                            lambda i: (i, 0, 0, 0, 0))
    return pl.pallas_call(
        functools.partial(_mem_kv_kernel, batch=batch),
        grid=(DEPTH,),
        in_specs=[
            pl.BlockSpec((batch * N_MEM, D_MODEL), lambda i: (0, 0)),
            pl.BlockSpec((1, D_MODEL), lambda i: (0, 0)),
            pl.BlockSpec((None, D_MODEL, 2 * MEM_WIDTH), lambda i: (i, 0, 0)),
        ],
        out_specs=[out_spec, out_spec],
        out_shape=[out_sds, out_sds],
        compiler_params=pltpu.CompilerParams(dimension_semantics=("arbitrary",)),
        name="mem_kv",
    )(mem.reshape(batch * N_MEM, D_MODEL), mem_norm.reshape(1, D_MODEL), w_mem_kv)


def _bias_kernel(bucket_ref, inwin_ref, table_ref, out_ref):
    bucket = bucket_ref[...]
    inwin = inwin_ref[...] > 0
    for h in range(N_Q_HEADS):
        def body(b, acc, h=h):
            return jnp.where(bucket == b, table_ref[b, h], acc)
        acc = lax.fori_loop(0, REL_BUCKETS, body, jnp.zeros((BLOCK, 2 * BLOCK), F32))
        out_ref[h] = jnp.where(inwin, acc, NEG)


def _rel_bucket(dist):
    max_exact = REL_BUCKETS // 2
    d = jnp.maximum(dist, 1).astype(F32)
    large = max_exact + (jnp.log(d / max_exact) / math.log(REL_MAX_DIST / max_exact)
                         * (REL_BUCKETS - max_exact)).astype(jnp.int32)
    large = jnp.minimum(large, REL_BUCKETS - 1)
    return jnp.where(dist < max_exact, dist, large)


def _band_bias(rel_bias):
    qi = jnp.arange(BLOCK, dtype=jnp.int32)[:, None]
    kj = jnp.arange(2 * BLOCK, dtype=jnp.int32)[None, :]
    dist = qi + BLOCK - kj
    inwin = ((dist >= 0) & (dist < WINDOW)).astype(jnp.int32)
    bucket = _rel_bucket(jnp.maximum(dist, 0)).astype(jnp.int32)
    return pl.pallas_call(
        _bias_kernel,
        in_specs=[
            pl.BlockSpec(memory_space=pltpu.VMEM),
            pl.BlockSpec(memory_space=pltpu.VMEM),
            pl.BlockSpec(memory_space=pltpu.SMEM),
        ],
        out_specs=pl.BlockSpec(memory_space=pltpu.VMEM),
        out_shape=jax.ShapeDtypeStruct((N_Q_HEADS, BLOCK, 2 * BLOCK), F32),
        name="band_bias",
    )(bucket, inwin, rel_bias.astype(F32))


def _mem_attention(qm, mkt_ref, mv_ref):
    logits = [_dot(qm, mkt_ref[h]) for h in range(N_MEM_HEADS)]
    yield
    out = None
    for h in range(N_MEM_HEADS):
        m = jnp.max(logits[h], axis=-1, keepdims=True)
        p = jnp.exp(logits[h] - m)
        s = jnp.sum(p, axis=-1, keepdims=True)
        o = _dot(p.astype(BF16), mv_ref[h]) * (1.0 / s)
        out = o if out is None else out + o
        yield
    return out


def _short_conv(u, b_gate, c_gate, w, vs_ref, rows):
    v = c_gate * u
    base = SUBLANES
    vs_ref[base:base + rows, :] = v
    conv = (w[0:1] * vs_ref[base - 2:base - 2 + rows, :]
            + w[1:2] * vs_ref[base - 1:base - 1 + rows, :]
            + w[2:3] * v)
    return b_gate * conv


def _swa(q, kv_prev_ref, kv_cur_ref, bias_ref, sinks_ref, layer, first):
    col = lax.broadcasted_iota(jnp.int32, (BLOCK, 2 * BLOCK), 1)
    kill_prev = jnp.logical_and(first, col < BLOCK)
    zeros = jnp.zeros((HEAD_DIM, 2 * BLOCK), BF16)

    def scores(n, hk):
        if n == 0:
            keys = jnp.concatenate([kv_prev_ref[...], kv_cur_ref[:, 0:BLOCK]], axis=1)
        else:
            keys = kv_cur_ref[:, (n - 1) * BLOCK:(n + 1) * BLOCK]
        kt = keys[hk * HEAD_DIM:(hk + 1) * HEAD_DIM, :]
        vt = keys[KV_WIDTH + hk * HEAD_DIM:KV_WIDTH + (hk + 1) * HEAD_DIM, :]
        k_lo = jnp.concatenate([kt, zeros], axis=0)
        k_hi = jnp.concatenate([zeros, kt], axis=0)
        logits = []
        for g in range(GROUP):
            head = hk * GROUP + g
            grp, pos = head // 2, head % 2
            qg = q[n * BLOCK:(n + 1) * BLOCK, grp * LANES:(grp + 1) * LANES]
            bias = bias_ref[head]
            if n == 0:
                bias = jnp.where(kill_prev, NEG, bias)
            logits.append(_dot(qg, k_hi if pos else k_lo) + bias)
        return logits, vt

    def attend(hk, logits, vt):
        ps, invs = [], []
        for g in range(GROUP):
            sink = sinks_ref[layer, hk * GROUP + g]
            m = jnp.maximum(jnp.max(logits[g], axis=-1, keepdims=True), sink)
            p = jnp.exp(logits[g] - m)
            s = jnp.sum(p, axis=-1, keepdims=True) + jnp.exp(sink - m)
            ps.append(p.astype(BF16))
            invs.append(1.0 / s)
        o3 = _dot_nt(jnp.concatenate(ps, axis=0), vt)
        return [o3[g * BLOCK:(g + 1) * BLOCK] * invs[g] for g in range(GROUP)]

    units = [(i, hk) for i in range(q.shape[0] // BLOCK) for hk in range(N_KV_HEADS)]
    outs = {i: [] for i, _ in units}
    nxt = scores(*units[0])
    for k, (i, hk) in enumerate(units):
        cur = nxt
        if k + 1 < len(units):
            nxt = scores(*units[k + 1])
        yield
        outs[i] += attend(hk, *cur)
    return jnp.concatenate([jnp.concatenate(outs[i], axis=-1) for i in sorted(outs)], axis=0)


def _ffn(x_ref, h_ref, xf_ref, wg_ref, wu_ref, wd_ref):
    h = h_ref[...]
    acts = []
    for c in range(D_FF // MXU_DIM):
        sl = slice(c * MXU_DIM, (c + 1) * MXU_DIM)
        gate = _dot(h, wg_ref[:, sl])
        yield
        up = _dot(h, wu_ref[:, sl])
        acts.append((gate * jax.nn.sigmoid(gate) * up).astype(BF16))
        if c < D_MODEL // MXU_DIM:
            xf_ref[:, sl] = x_ref[:, sl]
        yield
    yield "tail"
    act = jnp.concatenate(acts, axis=-1)
    for n in range(D_MODEL // MXU_DIM):
        cs = slice(n * MXU_DIM, (n + 1) * MXU_DIM)
        yield cs, xf_ref[:, cs] + _dot(act, wd_ref[:, cs])


def _advance(gen):
    try:
        return next(gen)
    except StopIteration:
        return "end"


def _run(gen):
    while _advance(gen) != "end":
        pass


def _interleave(ffn, n_ffn, mix, n_mix, ffn_blocks_first):
    advance = _advance
    f_state = m_state = None
    done = 0
    for i in range(n_ffn):
        if f_state is None:
            f_state = advance(ffn)
        while m_state is None and done * n_ffn < (i + 1) * n_mix:
            m_state = advance(mix)
            done += 1
    while f_state is None:
        f_state = advance(ffn)
    while m_state is None:
        m_state = advance(mix)
    assert (f_state, m_state) == ("tail", "tail")
    for _ in range(ffn_blocks_first):
        advance(ffn)
    assert advance(mix) == "end"
    _run(ffn)


def _layer_kernel(*refs, mixer, epilogue, layer, tm, tiles_per_batch, n_cast):
    it = iter(refs)
    x_ref, gmix_ref, win_ref, mkt_ref, mv_ref, wout_ref = (next(it) for _ in range(6))
    gffn_ref, wg_ref, wu_ref, wd_ref = (next(it) for _ in range(4))
    if mixer == "conv":
        convw_ref = next(it)
    else:
        kv_prev_ref, kv_cur_ref, bias_ref, sinks_ref = (next(it) for _ in range(4))
    if epilogue == "kv":
        gkv_ref, wkvt_ref = next(it), next(it)
    elif epilogue == "final":
        gfin_ref = next(it)
    cast_in = [next(it) for _ in range(n_cast)]
    out_ref = next(it)
    if epilogue == "kv":
        kvt_ref = next(it)
    cast_out = [next(it) for _ in range(n_cast)]
    x1_ref, h2_ref, xf_ref = next(it), next(it), next(it)
    if mixer == "conv":
        vs_ref = next(it)

    t = pl.program_id(0)
    n_tiles = pl.num_programs(0) - 1
    first = (t % tiles_per_batch) == 0

    if mixer == "conv":
        @pl.when(first)
        def _():
            vs_ref[0:SUBLANES, :] = jnp.zeros((SUBLANES, CONV_WIDTH), F32)

    def mix():
        x = x_ref[...]
        h = _rms(x, gmix_ref[...]).astype(BF16)
        yield
        widths = (CONV_WIDTH,) * 3 + (MEM_WIDTH,) if mixer == "conv" else (Q_WIDTH, MEM_WIDTH)
        fields, lo = [], 0
        for w in widths:
            fields.append(_dot(h, win_ref[:, lo:lo + w]))
            lo += w
            yield
        for src, dst in zip(cast_in, cast_out):
            dst[...] = src[...].astype(BF16)
        if mixer == "conv":
            u, b_gate, c_gate, qm = fields
            y_tok = _short_conv(u, b_gate, c_gate, convw_ref[...], vs_ref, tm)
            yield
        else:
            q, qm = fields
            y_tok = yield from _swa(q.astype(BF16), kv_prev_ref, kv_cur_ref, bias_ref, sinks_ref,
                                    layer, first)
        y_mem = yield from _mem_attention(qm.astype(BF16), mkt_ref, mv_ref)
        y = jnp.concatenate([y_tok, y_mem], axis=-1).astype(BF16)
        half = D_MODEL // 2
        lo_half = x[:, :half] + _dot(y, wout_ref[:, :half])
        yield "tail" if epilogue == "kv" else None
        hi_half = x[:, half:] + _dot(y, wout_ref[:, half:])
        if epilogue != "kv":
            yield "tail"
        x = jnp.concatenate([lo_half, hi_half], axis=-1)
        x1_ref[...] = x
        h2_ref[...] = _rms(x, gffn_ref[...]).astype(BF16)
        if mixer == "conv":
            vs_ref[0:SUBLANES, :] = vs_ref[tm:tm + SUBLANES, :]

    def ffn():
        blocks = []
        for item in _ffn(x1_ref, h2_ref, xf_ref, wg_ref, wu_ref, wd_ref):
            if isinstance(item, tuple):
                cs, block = item
                if epilogue == "none":
                    out_ref[:, cs] = block
                blocks.append(block)
                yield
            else:
                yield item
        if epilogue == "none":
            return
        x = jnp.concatenate(blocks, axis=-1)
        if epilogue == "kv":
            hk = _rms(x, gkv_ref[...]).astype(BF16)
            kvt = _dot_nt(wkvt_ref[...], hk)
            row = lax.broadcasted_iota(jnp.int32, (2 * KV_WIDTH, 1), 0)
            kvt_ref[...] = (kvt * jnp.where(row < KV_WIDTH, SCALE, 1.0)).astype(BF16)
        if epilogue == "final":
            x = _rms(x, gfin_ref[...])
        out_ref[...] = x

    @pl.when(t == 0)
    def _():
        _run(mix())

    @pl.when(jnp.logical_and(t > 0, t < n_tiles))
    def _():
        n_mix = 13 if mixer == "conv" else 10 + (tm // BLOCK) * N_KV_HEADS
        ffn_blocks_first = D_MODEL // MXU_DIM if epilogue == "kv" else 1
        _interleave(ffn(), 2 * (D_FF // MXU_DIM), mix(), n_mix, ffn_blocks_first)

    @pl.when(t == n_tiles)
    def _():
        _run(ffn())


def _const_spec(shape, index):
    return pl.BlockSpec(shape, lambda t: index, pipeline_mode=pl.Buffered(1))


def _cast_rows(n_rows, n_steps):
    tile = 2 * SUBLANES
    return min(r for r in range(tile, n_rows + 1, tile)
               if n_rows % r == 0 and n_rows // r <= n_steps)


def _layer(x, *, seq, mixer, epilogue, layer, sub, w, cast_next, p):
    tokens, _ = x.shape
    tm = TOKEN_TILE
    assert seq % tm == 0 and tm % BLOCK == 0
    tpb = seq // tm
    n_tiles = tokens // tm
    proj_w = A_PROJ if mixer == "conv" else B_PROJ
    blocks_per_tile = tm // BLOCK

    def mix_tile(t):
        return jnp.minimum(t, n_tiles - 1)

    def ffn_tile(t):
        return jnp.maximum(t - 1, 0)

    def x_in_map(t):
        return mix_tile(t), 0

    def x_out_map(t):
        return ffn_tile(t), 0

    def mem_map(t):
        return layer, mix_tile(t) // tpb, 0, 0, 0

    def kv_prev_map(t):
        return 0, jnp.maximum(mix_tile(t) * blocks_per_tile - 1, 0)

    def kv_cur_map(t):
        return 0, mix_tile(t)

    def kv_out_map(t):
        return 0, ffn_tile(t)

    mem_spec = pl.BlockSpec((None, None, N_MEM_HEADS, MEM_WIDTH, N_MEM), mem_map)
    in_specs = [
        pl.BlockSpec((tm, D_MODEL), x_in_map),
        _const_spec((None, 1, D_MODEL), (layer, 0, 0)),
        _const_spec((D_MODEL, proj_w), (0, 0)),
        mem_spec, mem_spec,
        _const_spec((D_MODEL, D_MODEL), (0, 0)),
        _const_spec((None, 1, D_MODEL), (layer, 0, 0)),
        _const_spec((D_MODEL, D_FF), (0, 0)),
        _const_spec((D_MODEL, D_FF), (0, 0)),
        _const_spec((D_FF, D_MODEL), (0, 0)),
    ]
    args = [x, p["norm_mix"], w["in"], p["mkt"], p["mv"], w["out"],
            p["norm_ffn"], w["gate"], w["up"], w["down"]]
    scratch = [pltpu.VMEM((tm, D_MODEL), F32), pltpu.VMEM((tm, D_MODEL), BF16),
               pltpu.VMEM((tm, D_MODEL), F32)]
    if mixer == "conv":
        in_specs.append(_const_spec((None, CONV_K, CONV_WIDTH), (sub, 0, 0)))
        args.append(p["conv_w"])
        scratch.append(pltpu.VMEM((tm + SUBLANES, CONV_WIDTH), F32))
    else:
        in_specs += [
            pl.BlockSpec((2 * KV_WIDTH, BLOCK), kv_prev_map),
            pl.BlockSpec((2 * KV_WIDTH, tm), kv_cur_map),
            _const_spec((N_Q_HEADS, BLOCK, 2 * BLOCK), (0, 0, 0)),
            pl.BlockSpec(memory_space=pltpu.SMEM),
        ]
        args += [p["kvt"], p["kvt"], p["bias"], p["sinks"]]
    if epilogue == "kv":
        in_specs += [_const_spec((1, D_MODEL), (0, 0)),
                     _const_spec((2 * KV_WIDTH, D_MODEL), (0, 0))]
        args += [p["kv_norm"], p["w_kvt"]]
    elif epilogue == "final":
        in_specs.append(_const_spec((1, D_MODEL), (0, 0)))
        args.append(p["final_norm"])

    out_specs = [pl.BlockSpec((tm, D_MODEL), x_out_map)]
    out_shape = [jax.ShapeDtypeStruct(x.shape, F32)]
    if epilogue == "kv":
        out_specs.append(pl.BlockSpec((2 * KV_WIDTH, tm), kv_out_map))
        out_shape.append(jax.ShapeDtypeStruct((2 * KV_WIDTH, tokens), BF16))
    for stack, idx in cast_next:
        _, n_rows, n_cols = stack.shape
        rows = _cast_rows(n_rows, n_tiles)
        last = n_rows // rows - 1
        in_specs.append(pl.BlockSpec((None, rows, n_cols),
                                     lambda t, idx=idx, last=last: (idx, jnp.minimum(t, last), 0)))
        args.append(stack)
        out_specs.append(pl.BlockSpec((rows, n_cols),
                                      lambda t, last=last: (jnp.minimum(t, last), 0)))
        out_shape.append(jax.ShapeDtypeStruct((n_rows, n_cols), BF16))

    outs = pl.pallas_call(
        functools.partial(_layer_kernel, mixer=mixer, epilogue=epilogue, layer=sub, tm=tm,
                          tiles_per_batch=tpb, n_cast=len(cast_next)),
        grid=(n_tiles + 1,),
        in_specs=in_specs,
        out_specs=out_specs,
        out_shape=out_shape,
        scratch_shapes=scratch,
        compiler_params=pltpu.CompilerParams(
            dimension_semantics=("arbitrary",),
            vmem_limit_bytes=VMEM_LIMIT_BYTES),
        name=f"layer{layer}_{mixer}",
    )(*args)
    return outs


def kernel(x, mem, norm_mix, norm_ffn, a_w_in, a_conv_w, a_w_out, kv_norm, w_kv, b_w_q, b_sinks,
           b_w_out, rel_bias, mem_norm, w_mem_kv, w_gate, w_up, w_down, final_norm):
    mkt, mv = _mem_kv(mem, mem_norm, w_mem_kv)
    p = {
        "norm_mix": norm_mix.reshape(DEPTH, 1, D_MODEL),
        "norm_ffn": norm_ffn.reshape(DEPTH, 1, D_MODEL),
        "conv_w": a_conv_w,
        "mkt": mkt, "mv": mv,
        "kv_norm": kv_norm.reshape(1, D_MODEL),
        "w_kvt": w_kv.T.astype(BF16),
        "final_norm": final_norm.reshape(1, D_MODEL),
        "sinks": b_sinks.astype(F32),
        "bias": _band_bias(rel_bias),
    }

    def f32_weights(i):
        stack_in, stack_out, sub = (a_w_in, a_w_out, i) if i < N_A else (b_w_q, b_w_out, i - N_A)
        return {"in": (stack_in, sub), "out": (stack_out, sub),
                "gate": (w_gate, i), "up": (w_up, i), "down": (w_down, i)}

    w = {k: stack[idx].astype(BF16) for k, (stack, idx) in f32_weights(0).items()}
    batch, seq, _ = x.shape
    x = x.reshape(batch * seq, D_MODEL)
    for i in range(DEPTH):
        mixer = "conv" if i < N_A else "swa"
        sub = i if i < N_A else i - N_A
        epilogue = "kv" if i == N_A - 1 else ("final" if i == DEPTH - 1 else "none")
        nxt = f32_weights(i + 1) if i + 1 < DEPTH else {}
        outs = _layer(x, seq=seq, mixer=mixer, epilogue=epilogue, layer=i, sub=sub, w=w,
                      cast_next=list(nxt.values()), p=p)
        x = outs[0]
        if epilogue == "kv":
            p["kvt"] = outs[1]
        w = dict(zip(nxt.keys(), outs[len(outs) - len(nxt):]))
    return x.reshape(batch, seq, D_MODEL)
```

```python
import functools
import math

import jax
import jax.numpy as jnp
from jax import lax
from jax.experimental import pallas as pl
from jax.experimental.pallas import tpu as pltpu

D_MODEL = 1024
DEPTH = 4
N_MEM = 256
HEAD_DIM = 64
N_MEM_HEADS = 4
MEM_WIDTH = N_MEM_HEADS * HEAD_DIM
CONV_WIDTH = D_MODEL - MEM_WIDTH
CONV_K = 3
N_Q_HEADS = CONV_WIDTH // HEAD_DIM
N_KV_HEADS = 4
GROUP = N_Q_HEADS // N_KV_HEADS
Q_WIDTH = N_Q_HEADS * HEAD_DIM
KV_WIDTH = N_KV_HEADS * HEAD_DIM
A_PROJ = 3 * CONV_WIDTH + MEM_WIDTH
B_PROJ = Q_WIDTH + MEM_WIDTH
WINDOW = 128
BLOCK = 128
REL_BUCKETS = 32
REL_MAX_DIST = 128
D_FF = ((8 * D_MODEL + 3 * 256 - 1) // (3 * 256)) * 256
N_A = DEPTH // 2
N_B = DEPTH - N_A
EPS = 1e-5

SCALE = HEAD_DIM ** -0.5
NEG = -1e30
LANES = 128
SUBLANES = 8
MXU_DIM = 256
TOKEN_TILE = 512
VMEM_LIMIT_BYTES = 56 * 1024 * 1024

BF16 = jnp.bfloat16
F32 = jnp.float32


def _dot(a, b):
    return jnp.dot(a, b, preferred_element_type=F32)


def _dot_nt(a, b):
    return lax.dot_general(a, b, (((1,), (1,)), ((), ())), preferred_element_type=F32)


def _rms(x, g):
    ms = jnp.mean(x * x, axis=-1, keepdims=True)
    return x * lax.rsqrt(ms + EPS) * g


def _mem_kv_kernel(mem_ref, g_ref, w_ref, kt_ref, v_ref, *, batch):
    mem_n = _rms(mem_ref[...], g_ref[...]).astype(BF16)
    kv = _dot(mem_n, w_ref[...].astype(BF16))
    row = lax.broadcasted_iota(jnp.int32, (MEM_WIDTH, N_MEM), 0)
    col = lax.broadcasted_iota(jnp.int32, (N_MEM, MEM_WIDTH), 1)
    for b in range(batch):
        kvb = kv[b * N_MEM:(b + 1) * N_MEM]
        kt = (kvb[:, :MEM_WIDTH] * SCALE).T
        v = kvb[:, MEM_WIDTH:]
        for h in range(N_MEM_HEADS):
            lo, hi = h * HEAD_DIM, (h + 1) * HEAD_DIM
            kt_ref[b, h] = jnp.where((row >= lo) & (row < hi), kt, 0.0).astype(BF16)
            v_ref[b, h] = jnp.where((col >= lo) & (col < hi), v, 0.0).astype(BF16)


def _mem_kv(mem, mem_norm, w_mem_kv):
    batch = mem.shape[0]
    out_sds = jax.ShapeDtypeStruct((DEPTH, batch, N_MEM_HEADS, MEM_WIDTH, N_MEM), BF16)
    out_spec = pl.BlockSpec((None, batch, N_MEM_HEADS, MEM_WIDTH, N_MEM),
                            lambda i: (i, 0, 0, 0, 0))
    return pl.pallas_call(
        functools.partial(_mem_kv_kernel, batch=batch),
        grid=(DEPTH,),
        in_specs=[
            pl.BlockSpec((batch * N_MEM, D_MODEL), lambda i: (0, 0)),
            pl.BlockSpec((1, D_MODEL), lambda i: (0, 0)),
            pl.BlockSpec((None, D_MODEL, 2 * MEM_WIDTH), lambda i: (i, 0, 0)),
        ],
        out_specs=[out_spec, out_spec],
        out_shape=[out_sds, out_sds],
        compiler_params=pltpu.CompilerParams(dimension_semantics=("arbitrary",)),
        name="mem_kv",
    )(mem.reshape(batch * N_MEM, D_MODEL), mem_norm.reshape(1, D_MODEL), w_mem_kv)


def _bias_kernel(bucket_ref, inwin_ref, table_ref, out_ref):
    bucket = bucket_ref[...]
    inwin = inwin_ref[...] > 0
    for h in range(N_Q_HEADS):
        def body(b, acc, h=h):
            return jnp.where(bucket == b, table_ref[b, h], acc)
        acc = lax.fori_loop(0, REL_BUCKETS, body, jnp.zeros((BLOCK, 2 * BLOCK), F32))
        out_ref[h] = jnp.where(inwin, acc, NEG)


def _rel_bucket(dist):
    max_exact = REL_BUCKETS // 2
    d = jnp.maximum(dist, 1).astype(F32)
    large = max_exact + (jnp.log(d / max_exact) / math.log(REL_MAX_DIST / max_exact)
                         * (REL_BUCKETS - max_exact)).astype(jnp.int32)
    large = jnp.minimum(large, REL_BUCKETS - 1)
    return jnp.where(dist < max_exact, dist, large)


def _band_bias(rel_bias):
    qi = jnp.arange(BLOCK, dtype=jnp.int32)[:, None]
    kj = jnp.arange(2 * BLOCK, dtype=jnp.int32)[None, :]
    dist = qi + BLOCK - kj
    inwin = ((dist >= 0) & (dist < WINDOW)).astype(jnp.int32)
    bucket = _rel_bucket(jnp.maximum(dist, 0)).astype(jnp.int32)
    return pl.pallas_call(
        _bias_kernel,
        in_specs=[
            pl.BlockSpec(memory_space=pltpu.VMEM),
            pl.BlockSpec(memory_space=pltpu.VMEM),
            pl.BlockSpec(memory_space=pltpu.SMEM),
        ],
        out_specs=pl.BlockSpec(memory_space=pltpu.VMEM),
        out_shape=jax.ShapeDtypeStruct((N_Q_HEADS, BLOCK, 2 * BLOCK), F32),
        name="band_bias",
    )(bucket, inwin, rel_bias.astype(F32))


def _mem_attention(qm, mkt_ref, mv_ref):
    logits = [_dot(qm, mkt_ref[h]) for h in range(N_MEM_HEADS)]
    yield
    out = None
    for h in range(N_MEM_HEADS):
        m = jnp.max(logits[h], axis=-1, keepdims=True)
        p = jnp.exp(logits[h] - m)
        s = jnp.sum(p, axis=-1, keepdims=True)
        o = _dot(p.astype(BF16), mv_ref[h]) * (1.0 / s)
        out = o if out is None else out + o
        yield
    return out


def _short_conv(u, b_gate, c_gate, w, vs_ref, rows):
    v = c_gate * u
    base = SUBLANES
    vs_ref[base:base + rows, :] = v
    conv = (w[0:1] * vs_ref[base - 2:base - 2 + rows, :]
            + w[1:2] * vs_ref[base - 1:base - 1 + rows, :]
            + w[2:3] * v)
    return b_gate * conv


def _swa(q, kt_prev_ref, kt_cur_ref, v_prev_ref, v_cur_ref, bias_ref, sinks_ref, layer, first):
    col = lax.broadcasted_iota(jnp.int32, (BLOCK, 2 * BLOCK), 1)
    kill_prev = jnp.logical_and(first, col < BLOCK)
    zeros = jnp.zeros((HEAD_DIM, 2 * BLOCK), BF16)

    def scores(n, hk):
        pair = slice((hk // 2) * LANES, (hk // 2 + 1) * LANES)
        if n == 0:
            kt = jnp.concatenate([kt_prev_ref[hk * HEAD_DIM:(hk + 1) * HEAD_DIM, :],
                                  kt_cur_ref[hk * HEAD_DIM:(hk + 1) * HEAD_DIM, 0:BLOCK]], axis=1)
            v2 = jnp.concatenate([v_prev_ref[:, pair], v_cur_ref[0:BLOCK, pair]], axis=0)
        else:
            keys = slice((n - 1) * BLOCK, (n + 1) * BLOCK)
            kt = kt_cur_ref[hk * HEAD_DIM:(hk + 1) * HEAD_DIM, keys]
            v2 = v_cur_ref[keys, pair]
        k_lo = jnp.concatenate([kt, zeros], axis=0)
        k_hi = jnp.concatenate([zeros, kt], axis=0)
        logits = []
        for g in range(GROUP):
            head = hk * GROUP + g
            grp, pos = head // 2, head % 2
            qg = q[n * BLOCK:(n + 1) * BLOCK, grp * LANES:(grp + 1) * LANES]
            bias = bias_ref[head]
            if n == 0:
                bias = jnp.where(kill_prev, NEG, bias)
            logits.append(_dot(qg, k_hi if pos else k_lo) + bias)
        return logits, v2

    def attend(hk, logits, v2):
        ps, invs = [], []
        for g in range(GROUP):
            sink = sinks_ref[layer, hk * GROUP + g]
            m = jnp.maximum(jnp.max(logits[g], axis=-1, keepdims=True), sink)
            p = jnp.exp(logits[g] - m)
            s = jnp.sum(p, axis=-1, keepdims=True) + jnp.exp(sink - m)
            ps.append(p.astype(BF16))
            invs.append(1.0 / s)
        o3 = _dot(jnp.concatenate(ps, axis=0), v2)
        half = slice((hk % 2) * HEAD_DIM, (hk % 2 + 1) * HEAD_DIM)
        return [o3[g * BLOCK:(g + 1) * BLOCK, half] * invs[g] for g in range(GROUP)]

    units = [(i, hk) for i in range(q.shape[0] // BLOCK) for hk in range(N_KV_HEADS)]
    outs = {i: [] for i, _ in units}
    nxt = scores(*units[0])
    for k, (i, hk) in enumerate(units):
        cur = nxt
        if k + 1 < len(units):
            nxt = scores(*units[k + 1])
        yield
        outs[i] += attend(hk, *cur)
    return jnp.concatenate([jnp.concatenate(outs[i], axis=-1) for i in sorted(outs)], axis=0)


def _ffn(x_ref, h_ref, xf_ref, wg_ref, wu_ref, wd_ref):
    h = h_ref[...]
    acts = []
    for c in range(D_FF // MXU_DIM):
        sl = slice(c * MXU_DIM, (c + 1) * MXU_DIM)
        gate = _dot(h, wg_ref[:, sl])
        yield
        up = _dot(h, wu_ref[:, sl])
        acts.append((gate * jax.nn.sigmoid(gate) * up).astype(BF16))
        if c < D_MODEL // MXU_DIM:
            xf_ref[:, sl] = x_ref[:, sl]
        yield
    yield "tail"
    act = jnp.concatenate(acts, axis=-1)
    for n in range(D_MODEL // MXU_DIM):
        cs = slice(n * MXU_DIM, (n + 1) * MXU_DIM)
        yield cs, xf_ref[:, cs] + _dot(act, wd_ref[:, cs])


def _advance(gen):
    try:
        return next(gen)
    except StopIteration:
        return "end"


def _run(gen):
    while _advance(gen) != "end":
        pass


def _interleave(ffn, n_ffn, mix, n_mix, ffn_blocks_first):
    advance = _advance
    f_state = m_state = None
    done = 0
    for i in range(n_ffn):
        if f_state is None:
            f_state = advance(ffn)
        while m_state is None and done * n_ffn < (i + 1) * n_mix:
            m_state = advance(mix)
            done += 1
    while f_state is None:
        f_state = advance(ffn)
    while m_state is None:
        m_state = advance(mix)
    assert (f_state, m_state) == ("tail", "tail")
    for _ in range(ffn_blocks_first):
        advance(ffn)
    assert advance(mix) == "end"
    _run(ffn)


def _layer_kernel(*refs, mixer, epilogue, layer, tm, tiles_per_batch, n_cast):
    it = iter(refs)
    x_ref, gmix_ref, win_ref, mkt_ref, mv_ref, wout_ref = (next(it) for _ in range(6))
    gffn_ref, wg_ref, wu_ref, wd_ref = (next(it) for _ in range(4))
    if mixer == "conv":
        convw_ref = next(it)
    else:
        kt_prev_ref, kt_cur_ref, v_prev_ref, v_cur_ref = (next(it) for _ in range(4))
        bias_ref, sinks_ref = next(it), next(it)
    if epilogue == "kv":
        gkv_ref, wkt_ref, wv_ref = next(it), next(it), next(it)
    elif epilogue == "final":
        gfin_ref = next(it)
    cast_in = [next(it) for _ in range(n_cast)]
    out_ref = next(it)
    if epilogue == "kv":
        kt_ref, v_ref = next(it), next(it)
    cast_out = [next(it) for _ in range(n_cast)]
    x1_ref, h2_ref, xf_ref = next(it), next(it), next(it)
    if mixer == "conv":
        vs_ref = next(it)

    t = pl.program_id(0)
    n_tiles = pl.num_programs(0) - 1
    first = (t % tiles_per_batch) == 0

    if mixer == "conv":
        @pl.when(first)
        def _():
            vs_ref[0:SUBLANES, :] = jnp.zeros((SUBLANES, CONV_WIDTH), F32)

    def mix():
        x = x_ref[...]
        h = _rms(x, gmix_ref[...]).astype(BF16)
        yield
        widths = (CONV_WIDTH,) * 3 + (MEM_WIDTH,) if mixer == "conv" else (Q_WIDTH, MEM_WIDTH)
        fields, lo = [], 0
        for w in widths:
            fields.append(_dot(h, win_ref[:, lo:lo + w]))
            lo += w
            yield
        for src, dst in zip(cast_in, cast_out):
            dst[...] = src[...].astype(BF16)
        if mixer == "conv":
            u, b_gate, c_gate, qm = fields
            y_tok = _short_conv(u, b_gate, c_gate, convw_ref[...], vs_ref, tm)
            yield
        else:
            q, qm = fields
            y_tok = yield from _swa(q.astype(BF16), kt_prev_ref, kt_cur_ref, v_prev_ref, v_cur_ref,
                                    bias_ref, sinks_ref, layer, first)
        y_mem = yield from _mem_attention(qm.astype(BF16), mkt_ref, mv_ref)
        y = jnp.concatenate([y_tok, y_mem], axis=-1).astype(BF16)
        half = D_MODEL // 2
        lo_half = x[:, :half] + _dot(y, wout_ref[:, :half])
        yield "tail" if epilogue == "kv" else None
        hi_half = x[:, half:] + _dot(y, wout_ref[:, half:])
        if epilogue != "kv":
            yield "tail"
        x = jnp.concatenate([lo_half, hi_half], axis=-1)
        x1_ref[...] = x
        h2_ref[...] = _rms(x, gffn_ref[...]).astype(BF16)
        if mixer == "conv":
            vs_ref[0:SUBLANES, :] = vs_ref[tm:tm + SUBLANES, :]

    def ffn():
        blocks = []
        for item in _ffn(x1_ref, h2_ref, xf_ref, wg_ref, wu_ref, wd_ref):
            if isinstance(item, tuple):
                cs, block = item
                if epilogue == "none":
                    out_ref[:, cs] = block
                blocks.append(block)
                yield
            else:
                yield item
        if epilogue == "none":
            return
        x = jnp.concatenate(blocks, axis=-1)
        if epilogue == "kv":
            hk = _rms(x, gkv_ref[...]).astype(BF16)
            kt_ref[...] = (_dot_nt(wkt_ref[...], hk) * SCALE).astype(BF16)
            v_ref[...] = _dot(hk, wv_ref[...]).astype(BF16)
        if epilogue == "final":
            x = _rms(x, gfin_ref[...])
        out_ref[...] = x

    @pl.when(t == 0)
    def _():
        _run(mix())

    @pl.when(jnp.logical_and(t > 0, t < n_tiles))
    def _():
        n_mix = 13 if mixer == "conv" else 10 + (tm // BLOCK) * N_KV_HEADS
        ffn_blocks_first = D_MODEL // MXU_DIM if epilogue == "kv" else 1
        _interleave(ffn(), 2 * (D_FF // MXU_DIM), mix(), n_mix, ffn_blocks_first)

    @pl.when(t == n_tiles)
    def _():
        _run(ffn())


def _const_spec(shape, index):
    return pl.BlockSpec(shape, lambda t: index, pipeline_mode=pl.Buffered(1))


def _cast_rows(n_rows, n_steps):
    tile = 2 * SUBLANES
    return min(r for r in range(tile, n_rows + 1, tile)
               if n_rows % r == 0 and n_rows // r <= n_steps)


def _layer(x, *, seq, mixer, epilogue, layer, sub, w, cast_next, p):
    tokens, _ = x.shape
    tm = TOKEN_TILE
    assert seq % tm == 0 and tm % BLOCK == 0
    tpb = seq // tm
    n_tiles = tokens // tm
    proj_w = A_PROJ if mixer == "conv" else B_PROJ
    blocks_per_tile = tm // BLOCK

    def mix_tile(t):
        return jnp.minimum(t, n_tiles - 1)

    def ffn_tile(t):
        return jnp.maximum(t - 1, 0)

    def x_in_map(t):
        return mix_tile(t), 0

    def x_out_map(t):
        return ffn_tile(t), 0

    def mem_map(t):
        return layer, mix_tile(t) // tpb, 0, 0, 0

    def kv_prev_map(t):
        return 0, jnp.maximum(mix_tile(t) * blocks_per_tile - 1, 0)

    def kv_cur_map(t):
        return 0, mix_tile(t)

    def kv_out_map(t):
        return 0, ffn_tile(t)

    mem_spec = pl.BlockSpec((None, None, N_MEM_HEADS, MEM_WIDTH, N_MEM), mem_map)
    in_specs = [
        pl.BlockSpec((tm, D_MODEL), x_in_map),
        _const_spec((None, 1, D_MODEL), (layer, 0, 0)),
        _const_spec((D_MODEL, proj_w), (0, 0)),
        mem_spec, mem_spec,
        _const_spec((D_MODEL, D_MODEL), (0, 0)),
        _const_spec((None, 1, D_MODEL), (layer, 0, 0)),
        _const_spec((D_MODEL, D_FF), (0, 0)),
        _const_spec((D_MODEL, D_FF), (0, 0)),
        _const_spec((D_FF, D_MODEL), (0, 0)),
    ]
    args = [x, p["norm_mix"], w["in"], p["mkt"], p["mv"], w["out"],
            p["norm_ffn"], w["gate"], w["up"], w["down"]]
    scratch = [pltpu.VMEM((tm, D_MODEL), F32), pltpu.VMEM((tm, D_MODEL), BF16),
               pltpu.VMEM((tm, D_MODEL), F32)]
    if mixer == "conv":
        in_specs.append(_const_spec((None, CONV_K, CONV_WIDTH), (sub, 0, 0)))
        args.append(p["conv_w"])
        scratch.append(pltpu.VMEM((tm + SUBLANES, CONV_WIDTH), F32))
    else:
        in_specs += [
            pl.BlockSpec((KV_WIDTH, BLOCK), kv_prev_map),
            pl.BlockSpec((KV_WIDTH, tm), kv_cur_map),
            pl.BlockSpec((BLOCK, KV_WIDTH), lambda t: kv_prev_map(t)[::-1]),
            pl.BlockSpec((tm, KV_WIDTH), lambda t: kv_cur_map(t)[::-1]),
            _const_spec((N_Q_HEADS, BLOCK, 2 * BLOCK), (0, 0, 0)),
            pl.BlockSpec(memory_space=pltpu.SMEM),
        ]
        args += [p["kt"], p["kt"], p["v"], p["v"], p["bias"], p["sinks"]]
    if epilogue == "kv":
        in_specs += [_const_spec((1, D_MODEL), (0, 0)),
                     _const_spec((KV_WIDTH, D_MODEL), (0, 0)),
                     _const_spec((D_MODEL, KV_WIDTH), (0, 0))]
        args += [p["kv_norm"], p["w_kt"], p["w_v"]]
    elif epilogue == "final":
        in_specs.append(_const_spec((1, D_MODEL), (0, 0)))
        args.append(p["final_norm"])

    out_specs = [pl.BlockSpec((tm, D_MODEL), x_out_map)]
    out_shape = [jax.ShapeDtypeStruct(x.shape, F32)]
    if epilogue == "kv":
        out_specs += [pl.BlockSpec((KV_WIDTH, tm), kv_out_map),
                      pl.BlockSpec((tm, KV_WIDTH), lambda t: kv_out_map(t)[::-1])]
        out_shape += [jax.ShapeDtypeStruct((KV_WIDTH, tokens), BF16),
                      jax.ShapeDtypeStruct((tokens, KV_WIDTH), BF16)]
    for stack, idx in cast_next:
        _, n_rows, n_cols = stack.shape
        rows = _cast_rows(n_rows, n_tiles)
        last = n_rows // rows - 1
        in_specs.append(pl.BlockSpec((None, rows, n_cols),
                                     lambda t, idx=idx, last=last: (idx, jnp.minimum(t, last), 0)))
        args.append(stack)
        out_specs.append(pl.BlockSpec((rows, n_cols),
                                      lambda t, last=last: (jnp.minimum(t, last), 0)))
        out_shape.append(jax.ShapeDtypeStruct((n_rows, n_cols), BF16))

    outs = pl.pallas_call(
        functools.partial(_layer_kernel, mixer=mixer, epilogue=epilogue, layer=sub, tm=tm,
                          tiles_per_batch=tpb, n_cast=len(cast_next)),
        grid=(n_tiles + 1,),
        in_specs=in_specs,
        out_specs=out_specs,
        out_shape=out_shape,
        scratch_shapes=scratch,
        compiler_params=pltpu.CompilerParams(
            dimension_semantics=("arbitrary",),
            vmem_limit_bytes=VMEM_LIMIT_BYTES),
        name=f"layer{layer}_{mixer}",
    )(*args)
    return outs


def kernel(x, mem, norm_mix, norm_ffn, a_w_in, a_conv_w, a_w_out, kv_norm, w_kv, b_w_q, b_sinks,
           b_w_out, rel_bias, mem_norm, w_mem_kv, w_gate, w_up, w_down, final_norm):
    mkt, mv = _mem_kv(mem, mem_norm, w_mem_kv)
    p = {
        "norm_mix": norm_mix.reshape(DEPTH, 1, D_MODEL),
        "norm_ffn": norm_ffn.reshape(DEPTH, 1, D_MODEL),
        "conv_w": a_conv_w,
        "mkt": mkt, "mv": mv,
        "kv_norm": kv_norm.reshape(1, D_MODEL),
        "w_kt": w_kv[:, :KV_WIDTH].T.astype(BF16),
        "w_v": w_kv[:, KV_WIDTH:].astype(BF16),
        "final_norm": final_norm.reshape(1, D_MODEL),
        "sinks": b_sinks.astype(F32),
        "bias": _band_bias(rel_bias),
    }

    def f32_weights(i):
        stack_in, stack_out, sub = (a_w_in, a_w_out, i) if i < N_A else (b_w_q, b_w_out, i - N_A)
        return {"in": (stack_in, sub), "out": (stack_out, sub),
                "gate": (w_gate, i), "up": (w_up, i), "down": (w_down, i)}

    w = {k: stack[idx].astype(BF16) for k, (stack, idx) in f32_weights(0).items()}
    batch, seq, _ = x.shape
    x = x.reshape(batch * seq, D_MODEL)
    for i in range(DEPTH):
        mixer = "conv" if i < N_A else "swa"
        sub = i if i < N_A else i - N_A
        epilogue = "kv" if i == N_A - 1 else ("final" if i == DEPTH - 1 else "none")
        nxt = f32_weights(i + 1) if i + 1 < DEPTH else {}
        outs = _layer(x, seq=seq, mixer=mixer, epilogue=epilogue, layer=i, sub=sub, w=w,
                      cast_next=list(nxt.values()), p=p)
        x = outs[0]
        if epilogue == "kv":
            p["kt"], p["v"] = outs[1], outs[2]
        w = dict(zip(nxt.keys(), outs[len(outs) - len(nxt):]))
    return x.reshape(batch, seq, D_MODEL)
```

```python
import functools
import math

import jax
import jax.numpy as jnp
from jax import lax
from jax.experimental import pallas as pl
from jax.experimental.pallas import tpu as pltpu

D_MODEL = 1024
DEPTH = 4
N_MEM = 256
HEAD_DIM = 64
N_MEM_HEADS = 4
MEM_WIDTH = N_MEM_HEADS * HEAD_DIM
CONV_WIDTH = D_MODEL - MEM_WIDTH
CONV_K = 3
N_Q_HEADS = CONV_WIDTH // HEAD_DIM
N_KV_HEADS = 4
GROUP = N_Q_HEADS // N_KV_HEADS
Q_WIDTH = N_Q_HEADS * HEAD_DIM
KV_WIDTH = N_KV_HEADS * HEAD_DIM
A_PROJ = 3 * CONV_WIDTH + MEM_WIDTH
B_PROJ = Q_WIDTH + MEM_WIDTH
WINDOW = 128
BLOCK = 128
REL_BUCKETS = 32
REL_MAX_DIST = 128
D_FF = ((8 * D_MODEL + 3 * 256 - 1) // (3 * 256)) * 256
N_A = DEPTH // 2
N_B = DEPTH - N_A
EPS = 1e-5

SCALE = HEAD_DIM ** -0.5
NEG = -1e30
LANES = 128
SUBLANES = 8
MXU_DIM = 256
TOKEN_TILE = 512
VMEM_LIMIT_BYTES = 56 * 1024 * 1024

BF16 = jnp.bfloat16
F32 = jnp.float32


def _dot(a, b):
    return jnp.dot(a, b, preferred_element_type=F32)


def _dot_nt(a, b):
    return lax.dot_general(a, b, (((1,), (1,)), ((), ())), preferred_element_type=F32)


def _rms(x, g):
    ms = jnp.mean(x * x, axis=-1, keepdims=True)
    return x * lax.rsqrt(ms + EPS) * g


def _mem_kv_kernel(mem_ref, g_ref, w_ref, kt_ref, v_ref, *, batch):
    mem_n = _rms(mem_ref[...], g_ref[...]).astype(BF16)
    kv = _dot(mem_n, w_ref[...].astype(BF16))
    row = lax.broadcasted_iota(jnp.int32, (MEM_WIDTH, N_MEM), 0)
    col = lax.broadcasted_iota(jnp.int32, (N_MEM, MEM_WIDTH), 1)
    for b in range(batch):
        kvb = kv[b * N_MEM:(b + 1) * N_MEM]
        kt = (kvb[:, :MEM_WIDTH] * SCALE).T
        v = kvb[:, MEM_WIDTH:]
        for h in range(N_MEM_HEADS):
            lo, hi = h * HEAD_DIM, (h + 1) * HEAD_DIM
            kt_ref[b, h] = jnp.where((row >= lo) & (row < hi), kt, 0.0).astype(BF16)
            ones_lo = (lo + LANES) % MEM_WIDTH
            fill = jnp.where((col >= ones_lo) & (col < ones_lo + HEAD_DIM), 1.0, 0.0)
            v_ref[b, h] = jnp.where((col >= lo) & (col < hi), v, fill).astype(BF16)


def _mem_kv(mem, mem_norm, w_mem_kv):
    batch = mem.shape[0]
    out_sds = jax.ShapeDtypeStruct((DEPTH, batch, N_MEM_HEADS, MEM_WIDTH, N_MEM), BF16)
    out_spec = pl.BlockSpec((None, batch, N_MEM_HEADS, MEM_WIDTH, N_MEM),
                            lambda i: (i, 0, 0, 0, 0))
    return pl.pallas_call(
        functools.partial(_mem_kv_kernel, batch=batch),
        grid=(DEPTH,),
        in_specs=[
            pl.BlockSpec((batch * N_MEM, D_MODEL), lambda i: (0, 0)),
            pl.BlockSpec((1, D_MODEL), lambda i: (0, 0)),
            pl.BlockSpec((None, D_MODEL, 2 * MEM_WIDTH), lambda i: (i, 0, 0)),
        ],
        out_specs=[out_spec, out_spec],
        out_shape=[out_sds, out_sds],
        compiler_params=pltpu.CompilerParams(dimension_semantics=("arbitrary",)),
        name="mem_kv",
    )(mem.reshape(batch * N_MEM, D_MODEL), mem_norm.reshape(1, D_MODEL), w_mem_kv)


def _bias_kernel(bucket_ref, inwin_ref, table_ref, out_ref):
    bucket = bucket_ref[...]
    inwin = inwin_ref[...] > 0
    for h in range(N_Q_HEADS):
        def body(b, acc, h=h):
            return jnp.where(bucket == b, table_ref[b, h], acc)
        acc = lax.fori_loop(0, REL_BUCKETS, body, jnp.zeros((BLOCK, 2 * BLOCK), F32))
        out_ref[h] = jnp.where(inwin, acc, NEG)


def _rel_bucket(dist):
    max_exact = REL_BUCKETS // 2
    d = jnp.maximum(dist, 1).astype(F32)
    large = max_exact + (jnp.log(d / max_exact) / math.log(REL_MAX_DIST / max_exact)
                         * (REL_BUCKETS - max_exact)).astype(jnp.int32)
    large = jnp.minimum(large, REL_BUCKETS - 1)
    return jnp.where(dist < max_exact, dist, large)


def _band_bias(rel_bias):
    qi = jnp.arange(BLOCK, dtype=jnp.int32)[:, None]
    kj = jnp.arange(2 * BLOCK, dtype=jnp.int32)[None, :]
    dist = qi + BLOCK - kj
    inwin = ((dist >= 0) & (dist < WINDOW)).astype(jnp.int32)
    bucket = _rel_bucket(jnp.maximum(dist, 0)).astype(jnp.int32)
    return pl.pallas_call(
        _bias_kernel,
        in_specs=[
            pl.BlockSpec(memory_space=pltpu.VMEM),
            pl.BlockSpec(memory_space=pltpu.VMEM),
            pl.BlockSpec(memory_space=pltpu.SMEM),
        ],
        out_specs=pl.BlockSpec(memory_space=pltpu.VMEM),
        out_shape=jax.ShapeDtypeStruct((N_Q_HEADS, BLOCK, 2 * BLOCK), F32),
        name="band_bias",
    )(bucket, inwin, rel_bias.astype(F32))


def _mem_attention(qm, mkt_ref, mv_ref):
    logits = [_dot(qm, mkt_ref[h]) for h in range(N_MEM_HEADS)]
    yield
    lane = lax.broadcasted_iota(jnp.int32, (qm.shape[0], LANES), 1)
    halves = [None, None]
    for h in range(N_MEM_HEADS):
        m = jnp.max(logits[h], axis=-1, keepdims=True)
        p = jnp.exp(logits[h] - m).astype(BF16)
        o = _dot(p, mv_ref[h])
        grp, pos = h // 2, h % 2
        val, s = o[:, grp * LANES:(grp + 1) * LANES], o[:, (1 - grp) * LANES:(2 - grp) * LANES]
        live = (lane >= pos * HEAD_DIM) & (lane < (pos + 1) * HEAD_DIM)
        part = jnp.where(live, val / jnp.where(live, s, 1.0), 0.0)
        halves[grp] = part if halves[grp] is None else halves[grp] + part
        yield
    return jnp.concatenate(halves, axis=-1)


def _short_conv(u, b_gate, c_gate, w, vs_ref, rows):
    v = c_gate * u
    base = SUBLANES
    vs_ref[base:base + rows, :] = v
    conv = (w[0:1] * vs_ref[base - 2:base - 2 + rows, :]
            + w[1:2] * vs_ref[base - 1:base - 1 + rows, :]
            + w[2:3] * v)
    return b_gate * conv


def _swa(q, kt_prev_ref, kt_cur_ref, v_prev_ref, v_cur_ref, bias_ref, sinks_ref, layer, first):
    col = lax.broadcasted_iota(jnp.int32, (BLOCK, 2 * BLOCK), 1)
    kill_prev = jnp.logical_and(first, col < BLOCK)
    zeros = jnp.zeros((HEAD_DIM, 2 * BLOCK), BF16)
    ones = jnp.ones((2 * BLOCK, LANES), BF16)

    def scores(n, hk):
        pair = slice((hk // 2) * LANES, (hk // 2 + 1) * LANES)
        if n == 0:
            kt = jnp.concatenate([kt_prev_ref[hk * HEAD_DIM:(hk + 1) * HEAD_DIM, :],
                                  kt_cur_ref[hk * HEAD_DIM:(hk + 1) * HEAD_DIM, 0:BLOCK]], axis=1)
            v2 = jnp.concatenate([v_prev_ref[:, pair], v_cur_ref[0:BLOCK, pair]], axis=0)
        else:
            keys = slice((n - 1) * BLOCK, (n + 1) * BLOCK)
            kt = kt_cur_ref[hk * HEAD_DIM:(hk + 1) * HEAD_DIM, keys]
            v2 = v_cur_ref[keys, pair]
        k_lo = jnp.concatenate([kt, zeros], axis=0)
        k_hi = jnp.concatenate([zeros, kt], axis=0)
        logits = []
        for g in range(GROUP):
            head = hk * GROUP + g
            grp, pos = head // 2, head % 2
            qg = q[n * BLOCK:(n + 1) * BLOCK, grp * LANES:(grp + 1) * LANES]
            bias = bias_ref[head]
            if n == 0:
                bias = jnp.where(kill_prev, NEG, bias)
            logits.append(_dot(qg, k_hi if pos else k_lo) + bias)
        return logits, v2

    def attend(hk, logits, v2):
        ps, sink_terms = [], []
        for g in range(GROUP):
            sink = sinks_ref[layer, hk * GROUP + g]
            m = jnp.maximum(jnp.max(logits[g], axis=-1, keepdims=True), sink)
            ps.append(jnp.exp(logits[g] - m).astype(BF16))
            sink_terms.append(jnp.exp(sink - m))
        o3 = _dot(jnp.concatenate(ps, axis=0), jnp.concatenate([v2, ones], axis=1))
        half = slice((hk % 2) * HEAD_DIM, (hk % 2 + 1) * HEAD_DIM)
        sums = slice(LANES + half.start, LANES + half.stop)
        outs = []
        for g in range(GROUP):
            rows = slice(g * BLOCK, (g + 1) * BLOCK)
            outs.append(o3[rows, half] / (o3[rows, sums] + sink_terms[g]))
        return outs

    units = [(i, hk) for i in range(q.shape[0] // BLOCK) for hk in range(N_KV_HEADS)]
    outs = {i: [] for i, _ in units}
    nxt = scores(*units[0])
    for k, (i, hk) in enumerate(units):
        cur = nxt
        if k + 1 < len(units):
            nxt = scores(*units[k + 1])
        yield
        outs[i] += attend(hk, *cur)
    return jnp.concatenate([jnp.concatenate(outs[i], axis=-1) for i in sorted(outs)], axis=0)


def _ffn(x_ref, h_ref, xf_ref, wg_ref, wu_ref, wd_ref):
    h = h_ref[...]
    acts = []
    for c in range(D_FF // MXU_DIM):
        sl = slice(c * MXU_DIM, (c + 1) * MXU_DIM)
        gate = _dot(h, wg_ref[:, sl])
        yield
        up = _dot(h, wu_ref[:, sl])
        acts.append((gate * jax.nn.sigmoid(gate) * up).astype(BF16))
        if c < D_MODEL // MXU_DIM:
            xf_ref[:, sl] = x_ref[:, sl]
        yield
    yield "tail"
    act = jnp.concatenate(acts, axis=-1)
    for n in range(D_MODEL // MXU_DIM):
        cs = slice(n * MXU_DIM, (n + 1) * MXU_DIM)
        yield cs, xf_ref[:, cs] + _dot(act, wd_ref[:, cs])


def _advance(gen):
    try:
        return next(gen)
    except StopIteration:
        return "end"


def _run(gen):
    while _advance(gen) != "end":
        pass


def _interleave(ffn, n_ffn, mix, n_mix, ffn_blocks_first):
    advance = _advance
    f_state = m_state = None
    done = 0
    for i in range(n_ffn):
        if f_state is None:
            f_state = advance(ffn)
        while m_state is None and done * n_ffn < (i + 1) * n_mix:
            m_state = advance(mix)
            done += 1
    while f_state is None:
        f_state = advance(ffn)
    while m_state is None:
        m_state = advance(mix)
    assert (f_state, m_state) == ("tail", "tail")
    for _ in range(ffn_blocks_first):
        advance(ffn)
    assert advance(mix) == "end"
    _run(ffn)


def _layer_kernel(*refs, mixer, epilogue, layer, tm, tiles_per_batch, n_cast):
    it = iter(refs)
    x_ref, gmix_ref, win_ref, mkt_ref, mv_ref, wout_ref = (next(it) for _ in range(6))
    gffn_ref, wg_ref, wu_ref, wd_ref = (next(it) for _ in range(4))
    if mixer == "conv":
        convw_ref = next(it)
    else:
        kt_prev_ref, kt_cur_ref, v_prev_ref, v_cur_ref = (next(it) for _ in range(4))
        bias_ref, sinks_ref = next(it), next(it)
    if epilogue == "kv":
        gkv_ref, wkt_ref, wv_ref = next(it), next(it), next(it)
    elif epilogue == "final":
        gfin_ref = next(it)
    cast_in = [next(it) for _ in range(n_cast)]
    out_ref = next(it)
    if epilogue == "kv":
        kt_ref, v_ref = next(it), next(it)
    cast_out = [next(it) for _ in range(n_cast)]
    x1_ref, h2_ref, xf_ref = next(it), next(it), next(it)
    if mixer == "conv":
        vs_ref = next(it)

    t = pl.program_id(0)
    n_tiles = pl.num_programs(0) - 1
    first = (t % tiles_per_batch) == 0

    if mixer == "conv":
        @pl.when(first)
        def _():
            vs_ref[0:SUBLANES, :] = jnp.zeros((SUBLANES, CONV_WIDTH), F32)

    def mix():
        x = x_ref[...]
        h = _rms(x, gmix_ref[...]).astype(BF16)
        yield
        widths = (CONV_WIDTH,) * 3 + (MEM_WIDTH,) if mixer == "conv" else (Q_WIDTH, MEM_WIDTH)
        fields, lo = [], 0
        for w in widths:
            fields.append(_dot(h, win_ref[:, lo:lo + w]))
            lo += w
            yield
        for src, dst in zip(cast_in, cast_out):
            dst[...] = src[...].astype(BF16)
        if mixer == "conv":
            u, b_gate, c_gate, qm = fields
            y_tok = _short_conv(u, b_gate, c_gate, convw_ref[...], vs_ref, tm)
            yield
        else:
            q, qm = fields
            y_tok = yield from _swa(q.astype(BF16), kt_prev_ref, kt_cur_ref, v_prev_ref, v_cur_ref,
                                    bias_ref, sinks_ref, layer, first)
        y_mem = yield from _mem_attention(qm.astype(BF16), mkt_ref, mv_ref)
        y = jnp.concatenate([y_tok, y_mem], axis=-1).astype(BF16)
        half = D_MODEL // 2
        lo_half = x[:, :half] + _dot(y, wout_ref[:, :half])
        yield "tail" if epilogue == "kv" else None
        hi_half = x[:, half:] + _dot(y, wout_ref[:, half:])
        if epilogue != "kv":
            yield "tail"
        x = jnp.concatenate([lo_half, hi_half], axis=-1)
        x1_ref[...] = x
        h2_ref[...] = _rms(x, gffn_ref[...]).astype(BF16)
        if mixer == "conv":
            vs_ref[0:SUBLANES, :] = vs_ref[tm:tm + SUBLANES, :]

    def ffn():
        blocks = []
        for item in _ffn(x1_ref, h2_ref, xf_ref, wg_ref, wu_ref, wd_ref):
            if isinstance(item, tuple):
                cs, block = item
                if epilogue == "none":
                    out_ref[:, cs] = block
                blocks.append(block)
                yield
            else:
                yield item
        if epilogue == "none":
            return
        x = jnp.concatenate(blocks, axis=-1)
        if epilogue == "kv":
            hk = _rms(x, gkv_ref[...]).astype(BF16)
            kt_ref[...] = (_dot_nt(wkt_ref[...], hk) * SCALE).astype(BF16)
            v_ref[...] = _dot(hk, wv_ref[...]).astype(BF16)
        if epilogue == "final":
            x = _rms(x, gfin_ref[...])
        out_ref[...] = x

    @pl.when(t == 0)
    def _():
        _run(mix())

    @pl.when(jnp.logical_and(t > 0, t < n_tiles))
    def _():
        n_mix = 13 if mixer == "conv" else 10 + (tm // BLOCK) * N_KV_HEADS
        ffn_blocks_first = D_MODEL // MXU_DIM if epilogue == "kv" else 1
        _interleave(ffn(), 2 * (D_FF // MXU_DIM), mix(), n_mix, ffn_blocks_first)

    @pl.when(t == n_tiles)
    def _():
        _run(ffn())


def _const_spec(shape, index):
    return pl.BlockSpec(shape, lambda t: index, pipeline_mode=pl.Buffered(1))


def _cast_rows(n_rows, n_steps):
    tile = 2 * SUBLANES
    return min(r for r in range(tile, n_rows + 1, tile)
               if n_rows % r == 0 and n_rows // r <= n_steps)


def _layer(x, *, seq, mixer, epilogue, layer, sub, w, cast_next, p):
    tokens, _ = x.shape
    tm = TOKEN_TILE
    assert seq % tm == 0 and tm % BLOCK == 0
    tpb = seq // tm
    n_tiles = tokens // tm
    proj_w = A_PROJ if mixer == "conv" else B_PROJ
    blocks_per_tile = tm // BLOCK

    def mix_tile(t):
        return jnp.minimum(t, n_tiles - 1)

    def ffn_tile(t):
        return jnp.maximum(t - 1, 0)

    def x_in_map(t):
        return mix_tile(t), 0

    def x_out_map(t):
        return ffn_tile(t), 0

    def mem_map(t):
        return layer, mix_tile(t) // tpb, 0, 0, 0

    def kv_prev_map(t):
        return 0, jnp.maximum(mix_tile(t) * blocks_per_tile - 1, 0)

    def kv_cur_map(t):
        return 0, mix_tile(t)

    def kv_out_map(t):
        return 0, ffn_tile(t)

    mem_spec = pl.BlockSpec((None, None, N_MEM_HEADS, MEM_WIDTH, N_MEM), mem_map)
    in_specs = [
        pl.BlockSpec((tm, D_MODEL), x_in_map),
        _const_spec((None, 1, D_MODEL), (layer, 0, 0)),
        _const_spec((D_MODEL, proj_w), (0, 0)),
        mem_spec, mem_spec,
        _const_spec((D_MODEL, D_MODEL), (0, 0)),
        _const_spec((None, 1, D_MODEL), (layer, 0, 0)),
        _const_spec((D_MODEL, D_FF), (0, 0)),
        _const_spec((D_MODEL, D_FF), (0, 0)),
        _const_spec((D_FF, D_MODEL), (0, 0)),
    ]
    args = [x, p["norm_mix"], w["in"], p["mkt"], p["mv"], w["out"],
            p["norm_ffn"], w["gate"], w["up"], w["down"]]
    scratch = [pltpu.VMEM((tm, D_MODEL), F32), pltpu.VMEM((tm, D_MODEL), BF16),
               pltpu.VMEM((tm, D_MODEL), F32)]
    if mixer == "conv":
        in_specs.append(_const_spec((None, CONV_K, CONV_WIDTH), (sub, 0, 0)))
        args.append(p["conv_w"])
        scratch.append(pltpu.VMEM((tm + SUBLANES, CONV_WIDTH), F32))
    else:
        in_specs += [
            pl.BlockSpec((KV_WIDTH, BLOCK), kv_prev_map),
            pl.BlockSpec((KV_WIDTH, tm), kv_cur_map),
            pl.BlockSpec((BLOCK, KV_WIDTH), lambda t: kv_prev_map(t)[::-1]),
            pl.BlockSpec((tm, KV_WIDTH), lambda t: kv_cur_map(t)[::-1]),
            _const_spec((N_Q_HEADS, BLOCK, 2 * BLOCK), (0, 0, 0)),
            pl.BlockSpec(memory_space=pltpu.SMEM),
        ]
        args += [p["kt"], p["kt"], p["v"], p["v"], p["bias"], p["sinks"]]
    if epilogue == "kv":
        in_specs += [_const_spec((1, D_MODEL), (0, 0)),
                     _const_spec((KV_WIDTH, D_MODEL), (0, 0)),
                     _const_spec((D_MODEL, KV_WIDTH), (0, 0))]
        args += [p["kv_norm"], p["w_kt"], p["w_v"]]
    elif epilogue == "final":
        in_specs.append(_const_spec((1, D_MODEL), (0, 0)))
        args.append(p["final_norm"])

    out_specs = [pl.BlockSpec((tm, D_MODEL), x_out_map)]
    out_shape = [jax.ShapeDtypeStruct(x.shape, F32)]
    if epilogue == "kv":
        out_specs += [pl.BlockSpec((KV_WIDTH, tm), kv_out_map),
                      pl.BlockSpec((tm, KV_WIDTH), lambda t: kv_out_map(t)[::-1])]
        out_shape += [jax.ShapeDtypeStruct((KV_WIDTH, tokens), BF16),
                      jax.ShapeDtypeStruct((tokens, KV_WIDTH), BF16)]
    for stack, idx in cast_next:
        _, n_rows, n_cols = stack.shape
        rows = _cast_rows(n_rows, n_tiles)
        last = n_rows // rows - 1
        in_specs.append(pl.BlockSpec((None, rows, n_cols),
                                     lambda t, idx=idx, last=last: (idx, jnp.minimum(t, last), 0)))
        args.append(stack)
        out_specs.append(pl.BlockSpec((rows, n_cols),
                                      lambda t, last=last: (jnp.minimum(t, last), 0)))
        out_shape.append(jax.ShapeDtypeStruct((n_rows, n_cols), BF16))

    outs = pl.pallas_call(
        functools.partial(_layer_kernel, mixer=mixer, epilogue=epilogue, layer=sub, tm=tm,
                          tiles_per_batch=tpb, n_cast=len(cast_next)),
        grid=(n_tiles + 1,),
        in_specs=in_specs,
        out_specs=out_specs,
        out_shape=out_shape,
        scratch_shapes=scratch,
        compiler_params=pltpu.CompilerParams(
            dimension_semantics=("arbitrary",),
            vmem_limit_bytes=VMEM_LIMIT_BYTES),
        name=f"layer{layer}_{mixer}",
    )(*args)
    return outs


def kernel(x, mem, norm_mix, norm_ffn, a_w_in, a_conv_w, a_w_out, kv_norm, w_kv, b_w_q, b_sinks,
           b_w_out, rel_bias, mem_norm, w_mem_kv, w_gate, w_up, w_down, final_norm):
    mkt, mv = _mem_kv(mem, mem_norm, w_mem_kv)
    p = {
        "norm_mix": norm_mix.reshape(DEPTH, 1, D_MODEL),
        "norm_ffn": norm_ffn.reshape(DEPTH, 1, D_MODEL),
        "conv_w": a_conv_w,
        "mkt": mkt, "mv": mv,
        "kv_norm": kv_norm.reshape(1, D_MODEL),
        "w_kt": w_kv[:, :KV_WIDTH].T.astype(BF16),
        "w_v": w_kv[:, KV_WIDTH:].astype(BF16),
        "final_norm": final_norm.reshape(1, D_MODEL),
        "sinks": b_sinks.astype(F32),
        "bias": _band_bias(rel_bias),
    }

    def f32_weights(i):
        stack_in, stack_out, sub = (a_w_in, a_w_out, i) if i < N_A else (b_w_q, b_w_out, i - N_A)
        return {"in": (stack_in, sub), "out": (stack_out, sub),
                "gate": (w_gate, i), "up": (w_up, i), "down": (w_down, i)}

    w = {k: stack[idx].astype(BF16) for k, (stack, idx) in f32_weights(0).items()}
    batch, seq, _ = x.shape
    x = x.reshape(batch * seq, D_MODEL)
    for i in range(DEPTH):
        mixer = "conv" if i < N_A else "swa"
        sub = i if i < N_A else i - N_A
        epilogue = "kv" if i == N_A - 1 else ("final" if i == DEPTH - 1 else "none")
        nxt = f32_weights(i + 1) if i + 1 < DEPTH else {}
        outs = _layer(x, seq=seq, mixer=mixer, epilogue=epilogue, layer=i, sub=sub, w=w,
                      cast_next=list(nxt.values()), p=p)
        x = outs[0]
        if epilogue == "kv":
            p["kt"], p["v"] = outs[1], outs[2]
        w = dict(zip(nxt.keys(), outs[len(outs) - len(nxt):]))
    return x.reshape(batch, seq, D_MODEL)
```

```python
import functools
import math

import jax
import jax.numpy as jnp
from jax import lax
from jax.experimental import pallas as pl
from jax.experimental.pallas import tpu as pltpu

D_MODEL = 1024
DEPTH = 4
N_MEM = 256
HEAD_DIM = 64
N_MEM_HEADS = 4
MEM_WIDTH = N_MEM_HEADS * HEAD_DIM
CONV_WIDTH = D_MODEL - MEM_WIDTH
CONV_K = 3
N_Q_HEADS = CONV_WIDTH // HEAD_DIM
N_KV_HEADS = 4
GROUP = N_Q_HEADS // N_KV_HEADS
Q_WIDTH = N_Q_HEADS * HEAD_DIM
KV_WIDTH = N_KV_HEADS * HEAD_DIM
A_PROJ = 3 * CONV_WIDTH + MEM_WIDTH
B_PROJ = Q_WIDTH + MEM_WIDTH
WINDOW = 128
BLOCK = 128
REL_BUCKETS = 32
REL_MAX_DIST = 128
D_FF = ((8 * D_MODEL + 3 * 256 - 1) // (3 * 256)) * 256
N_A = DEPTH // 2
N_B = DEPTH - N_A
EPS = 1e-5

SCALE = HEAD_DIM ** -0.5
NEG = -1e30
LANES = 128
SUBLANES = 8
MXU_DIM = 256
TOKEN_TILE = 512
VMEM_LIMIT_BYTES = 56 * 1024 * 1024

BF16 = jnp.bfloat16
F32 = jnp.float32


def _dot(a, b):
    return jnp.dot(a, b, preferred_element_type=F32)


def _dot_nt(a, b):
    return lax.dot_general(a, b, (((1,), (1,)), ((), ())), preferred_element_type=F32)


def _rms(x, g):
    ms = jnp.mean(x * x, axis=-1, keepdims=True)
    return x * lax.rsqrt(ms + EPS) * g


def _mem_kv_kernel(mem_ref, g_ref, w_ref, kt_ref, v_ref, *, batch):
    mem_n = _rms(mem_ref[...], g_ref[...]).astype(BF16)
    kv = _dot(mem_n, w_ref[...].astype(BF16))
    row = lax.broadcasted_iota(jnp.int32, (MEM_WIDTH, N_MEM), 0)
    col = lax.broadcasted_iota(jnp.int32, (N_MEM, MEM_WIDTH), 1)
    for b in range(batch):
        kvb = kv[b * N_MEM:(b + 1) * N_MEM]
        kt = (kvb[:, :MEM_WIDTH] * SCALE).T
        v = kvb[:, MEM_WIDTH:]
        for h in range(N_MEM_HEADS):
            lo, hi = h * HEAD_DIM, (h + 1) * HEAD_DIM
            kt_ref[b, h] = jnp.where((row >= lo) & (row < hi), kt, 0.0).astype(BF16)
            ones_lo = (lo + LANES) % MEM_WIDTH
            fill = jnp.where((col >= ones_lo) & (col < ones_lo + HEAD_DIM), 1.0, 0.0)
            v_ref[b, h] = jnp.where((col >= lo) & (col < hi), v, fill).astype(BF16)


def _mem_kv(mem, mem_norm, w_mem_kv):
    batch = mem.shape[0]
    out_sds = jax.ShapeDtypeStruct((DEPTH, batch, N_MEM_HEADS, MEM_WIDTH, N_MEM), BF16)
    out_spec = pl.BlockSpec((None, batch, N_MEM_HEADS, MEM_WIDTH, N_MEM),
                            lambda i: (i, 0, 0, 0, 0))
    return pl.pallas_call(
        functools.partial(_mem_kv_kernel, batch=batch),
        grid=(DEPTH,),
        in_specs=[
            pl.BlockSpec((batch * N_MEM, D_MODEL), lambda i: (0, 0)),
            pl.BlockSpec((1, D_MODEL), lambda i: (0, 0)),
            pl.BlockSpec((None, D_MODEL, 2 * MEM_WIDTH), lambda i: (i, 0, 0)),
        ],
        out_specs=[out_spec, out_spec],
        out_shape=[out_sds, out_sds],
        compiler_params=pltpu.CompilerParams(dimension_semantics=("arbitrary",)),
        name="mem_kv",
    )(mem.reshape(batch * N_MEM, D_MODEL), mem_norm.reshape(1, D_MODEL), w_mem_kv)


def _bias_kernel(bucket_ref, inwin_ref, table_ref, out_ref):
    bucket = bucket_ref[...]
    inwin = inwin_ref[...] > 0
    for h in range(N_Q_HEADS):
        def body(b, acc, h=h):
            return jnp.where(bucket == b, table_ref[b, h], acc)
        acc = lax.fori_loop(0, REL_BUCKETS, body, jnp.zeros((BLOCK, 2 * BLOCK), F32))
        out_ref[h] = jnp.where(inwin, acc, NEG)


def _rel_bucket(dist):
    max_exact = REL_BUCKETS // 2
    d = jnp.maximum(dist, 1).astype(F32)
    large = max_exact + (jnp.log(d / max_exact) / math.log(REL_MAX_DIST / max_exact)
                         * (REL_BUCKETS - max_exact)).astype(jnp.int32)
    large = jnp.minimum(large, REL_BUCKETS - 1)
    return jnp.where(dist < max_exact, dist, large)


def _band_bias(rel_bias):
    qi = jnp.arange(BLOCK, dtype=jnp.int32)[:, None]
    kj = jnp.arange(2 * BLOCK, dtype=jnp.int32)[None, :]
    dist = qi + BLOCK - kj
    inwin = ((dist >= 0) & (dist < WINDOW)).astype(jnp.int32)
    bucket = _rel_bucket(jnp.maximum(dist, 0)).astype(jnp.int32)
    return pl.pallas_call(
        _bias_kernel,
        in_specs=[
            pl.BlockSpec(memory_space=pltpu.VMEM),
            pl.BlockSpec(memory_space=pltpu.VMEM),
            pl.BlockSpec(memory_space=pltpu.SMEM),
        ],
        out_specs=pl.BlockSpec(memory_space=pltpu.VMEM),
        out_shape=jax.ShapeDtypeStruct((N_Q_HEADS, BLOCK, 2 * BLOCK), F32),
        name="band_bias",
    )(bucket, inwin, rel_bias.astype(F32))


def _mem_attention(qm, mkt_ref, mv_ref):
    logits = [_dot(qm, mkt_ref[h]) for h in range(N_MEM_HEADS)]
    yield
    lane = lax.broadcasted_iota(jnp.int32, (qm.shape[0], LANES), 1)
    halves = [None, None]
    for h in range(N_MEM_HEADS):
        m = jnp.max(logits[h], axis=-1, keepdims=True)
        p = jnp.exp(logits[h] - m).astype(BF16)
        o = _dot(p, mv_ref[h])
        grp, pos = h // 2, h % 2
        val, s = o[:, grp * LANES:(grp + 1) * LANES], o[:, (1 - grp) * LANES:(2 - grp) * LANES]
        live = (lane >= pos * HEAD_DIM) & (lane < (pos + 1) * HEAD_DIM)
        part = jnp.where(live, val / jnp.where(live, s, 1.0), 0.0)
        halves[grp] = part if halves[grp] is None else halves[grp] + part
        yield
    return jnp.concatenate(halves, axis=-1)


def _short_conv(u, b_gate, c_gate, w, vs_ref, rows):
    v = c_gate * u
    base = SUBLANES
    vs_ref[base:base + rows, :] = v
    conv = (w[0:1] * vs_ref[base - 2:base - 2 + rows, :]
            + w[1:2] * vs_ref[base - 1:base - 1 + rows, :]
            + w[2:3] * v)
    return b_gate * conv


def _swa(q, kt_prev_ref, kt_cur_ref, v_prev_ref, v_cur_ref, bias_ref, sinks_ref, layer, first):
    col = lax.broadcasted_iota(jnp.int32, (BLOCK, 2 * BLOCK), 1)
    kill_prev = jnp.logical_and(first, col < BLOCK)
    zeros = jnp.zeros((HEAD_DIM, 2 * BLOCK), BF16)
    ones = jnp.ones((2 * BLOCK, LANES), BF16)

    def scores(n, hk):
        pair = slice((hk // 2) * LANES, (hk // 2 + 1) * LANES)
        if n == 0:
            kt = jnp.concatenate([kt_prev_ref[hk * HEAD_DIM:(hk + 1) * HEAD_DIM, :],
                                  kt_cur_ref[hk * HEAD_DIM:(hk + 1) * HEAD_DIM, 0:BLOCK]], axis=1)
            v2 = jnp.concatenate([v_prev_ref[:, pair], v_cur_ref[0:BLOCK, pair]], axis=0)
        else:
            keys = slice((n - 1) * BLOCK, (n + 1) * BLOCK)
            kt = kt_cur_ref[hk * HEAD_DIM:(hk + 1) * HEAD_DIM, keys]
            v2 = v_cur_ref[keys, pair]
        k_lo = jnp.concatenate([kt, zeros], axis=0)
        k_hi = jnp.concatenate([zeros, kt], axis=0)
        logits = []
        for g in range(GROUP):
            head = hk * GROUP + g
            grp, pos = head // 2, head % 2
            qg = q[n * BLOCK:(n + 1) * BLOCK, grp * LANES:(grp + 1) * LANES]
            bias = bias_ref[head]
            if n == 0:
                bias = jnp.where(kill_prev, NEG, bias)
            logits.append(_dot(qg, k_hi if pos else k_lo) + bias)
        return logits, v2

    def attend(hk, logits, v2):
        ps, sink_terms = [], []
        for g in range(GROUP):
            sink = sinks_ref[layer, hk * GROUP + g]
            m = jnp.maximum(jnp.max(logits[g], axis=-1, keepdims=True), sink)
            ps.append(jnp.exp(logits[g] - m).astype(BF16))
            sink_terms.append(jnp.exp(sink - m))
        o3 = _dot(jnp.concatenate(ps, axis=0), jnp.concatenate([v2, ones], axis=1))
        half = slice((hk % 2) * HEAD_DIM, (hk % 2 + 1) * HEAD_DIM)
        sums = slice(LANES + half.start, LANES + half.stop)
        outs = []
        for g in range(GROUP):
            rows = slice(g * BLOCK, (g + 1) * BLOCK)
            outs.append(o3[rows, half] / (o3[rows, sums] + sink_terms[g]))
        return outs

    units = [(i, hk) for i in range(q.shape[0] // BLOCK) for hk in range(N_KV_HEADS)]
    outs = {i: [] for i, _ in units}
    nxt = scores(*units[0])
    for k, (i, hk) in enumerate(units):
        cur = nxt
        if k + 1 < len(units):
            nxt = scores(*units[k + 1])
        yield
        outs[i] += attend(hk, *cur)
    return jnp.concatenate([jnp.concatenate(outs[i], axis=-1) for i in sorted(outs)], axis=0)


def _ffn(x_ref, h_ref, xf_ref, wg_ref, wu_ref, wd_ref):
    h = h_ref[...]
    acts = []
    for c in range(D_FF // MXU_DIM):
        sl = slice(c * MXU_DIM, (c + 1) * MXU_DIM)
        gate = _dot(h, wg_ref[:, sl])
        yield
        up = _dot(h, wu_ref[:, sl])
        acts.append((gate * jax.nn.sigmoid(gate) * up).astype(BF16))
        if c < D_MODEL // MXU_DIM:
            xf_ref[:, sl] = x_ref[:, sl]
        yield
    yield "tail"
    act = jnp.concatenate(acts, axis=-1)
    for n in range(D_MODEL // MXU_DIM):
        cs = slice(n * MXU_DIM, (n + 1) * MXU_DIM)
        yield cs, xf_ref[:, cs] + _dot(act, wd_ref[:, cs])


def _advance(gen):
    try:
        return next(gen)
    except StopIteration:
        return "end"


def _run(gen):
    while _advance(gen) != "end":
        pass


def _interleave(ffn, n_ffn, mix, n_mix, ffn_blocks_first):
    advance = _advance
    f_state = m_state = None
    done = 0
    for i in range(n_ffn):
        if f_state is None:
            f_state = advance(ffn)
        while m_state is None and done * n_ffn < (i + 1) * n_mix:
            m_state = advance(mix)
            done += 1
    while f_state is None:
        f_state = advance(ffn)
    while m_state is None:
        m_state = advance(mix)
    assert (f_state, m_state) == ("tail", "tail")
    for _ in range(ffn_blocks_first):
        advance(ffn)
    assert advance(mix) == "end"
    _run(ffn)


def _layer_kernel(*refs, mixer, kv, epilogue, layer, tm, tiles_per_batch, n_cast):
    it = iter(refs)
    x_ref, gmix_ref, win_ref, mkt_ref, mv_ref, wout_ref = (next(it) for _ in range(6))
    gffn_ref, wg_ref, wu_ref, wd_ref = (next(it) for _ in range(4))
    if mixer == "conv":
        convw_ref = next(it)
    else:
        if kv == "compute":
            gkv_ref, wkt_ref, wv_ref = next(it), next(it), next(it)
        else:
            kt_in_ref, v_in_ref = next(it), next(it)
        bias_ref, sinks_ref = next(it), next(it)
    if epilogue == "final":
        gfin_ref = next(it)
    cast_in = [next(it) for _ in range(n_cast)]
    out_ref = next(it)
    if kv == "compute":
        kt_out_ref, v_out_ref = next(it), next(it)
    cast_out = [next(it) for _ in range(n_cast)]
    x1_ref, h2_ref, xf_ref = next(it), next(it), next(it)
    if mixer == "conv":
        vs_ref = next(it)
    else:
        kt_prev_ref, v_prev_ref = next(it), next(it)
        kt_cur_ref, v_cur_ref = (kt_out_ref, v_out_ref) if kv == "compute" else (kt_in_ref, v_in_ref)

    t = pl.program_id(0)
    n_tiles = pl.num_programs(0) - 1
    first = (t % tiles_per_batch) == 0

    @pl.when(first)
    def _():
        if mixer == "conv":
            vs_ref[0:SUBLANES, :] = jnp.zeros((SUBLANES, CONV_WIDTH), F32)
        else:
            kt_prev_ref[...] = jnp.zeros((KV_WIDTH, BLOCK), BF16)
            v_prev_ref[...] = jnp.zeros((BLOCK, KV_WIDTH), BF16)

    def mix():
        x = x_ref[...]
        h = _rms(x, gmix_ref[...]).astype(BF16)
        yield
        if kv == "compute":
            hkv = _rms(x, gkv_ref[...]).astype(BF16)
            kt_out_ref[...] = (_dot_nt(wkt_ref[...], hkv) * SCALE).astype(BF16)
            yield
            v_out_ref[...] = _dot(hkv, wv_ref[...]).astype(BF16)
            yield
        widths = (CONV_WIDTH,) * 3 + (MEM_WIDTH,) if mixer == "conv" else (Q_WIDTH, MEM_WIDTH)
        fields, lo = [], 0
        for w in widths:
            fields.append(_dot(h, win_ref[:, lo:lo + w]))
            lo += w
            yield
        for src, dst in zip(cast_in, cast_out):
            dst[...] = src[...].astype(BF16)
        if mixer == "conv":
            u, b_gate, c_gate, qm = fields
            y_tok = _short_conv(u, b_gate, c_gate, convw_ref[...], vs_ref, tm)
            yield
        else:
            q, qm = fields
            y_tok = yield from _swa(q.astype(BF16), kt_prev_ref, kt_cur_ref, v_prev_ref, v_cur_ref,
                                    bias_ref, sinks_ref, layer, first)
        y_mem = yield from _mem_attention(qm.astype(BF16), mkt_ref, mv_ref)
        y = jnp.concatenate([y_tok, y_mem], axis=-1).astype(BF16)
        half = D_MODEL // 2
        lo_half = x[:, :half] + _dot(y, wout_ref[:, :half])
        yield
        hi_half = x[:, half:] + _dot(y, wout_ref[:, half:])
        yield "tail"
        x = jnp.concatenate([lo_half, hi_half], axis=-1)
        x1_ref[...] = x
        h2_ref[...] = _rms(x, gffn_ref[...]).astype(BF16)
        if mixer == "conv":
            vs_ref[0:SUBLANES, :] = vs_ref[tm:tm + SUBLANES, :]
        else:
            kt_prev_ref[...] = kt_cur_ref[:, tm - BLOCK:tm]
            v_prev_ref[...] = v_cur_ref[tm - BLOCK:tm, :]

    def ffn():
        blocks = []
        for item in _ffn(x1_ref, h2_ref, xf_ref, wg_ref, wu_ref, wd_ref):
            if isinstance(item, tuple):
                cs, block = item
                if epilogue == "none":
                    out_ref[:, cs] = block
                blocks.append(block)
                yield
            else:
                yield item
        if epilogue == "final":
            out_ref[...] = _rms(jnp.concatenate(blocks, axis=-1), gfin_ref[...])

    @pl.when(t == 0)
    def _():
        _run(mix())

    @pl.when(jnp.logical_and(t > 0, t < n_tiles))
    def _():
        n_mix = {"conv": 13, "swa": 10 + (tm // BLOCK) * N_KV_HEADS}[mixer]
        n_mix += 2 if kv == "compute" else 0
        _interleave(ffn(), 2 * (D_FF // MXU_DIM), mix(), n_mix, ffn_blocks_first=1)

    @pl.when(t == n_tiles)
    def _():
        _run(ffn())


def _const_spec(shape, index):
    return pl.BlockSpec(shape, lambda t: index, pipeline_mode=pl.Buffered(1))


def _cast_rows(n_rows, n_steps):
    tile = 2 * SUBLANES
    return min(r for r in range(tile, n_rows + 1, tile)
               if n_rows % r == 0 and n_rows // r <= n_steps)


def _layer(x, *, seq, mixer, kv, epilogue, layer, sub, w, cast_next, p):
    tokens, _ = x.shape
    tm = TOKEN_TILE
    assert seq % tm == 0 and tm % BLOCK == 0
    tpb = seq // tm
    n_tiles = tokens // tm
    proj_w = A_PROJ if mixer == "conv" else B_PROJ

    def mix_tile(t):
        return jnp.minimum(t, n_tiles - 1)

    def ffn_tile(t):
        return jnp.maximum(t - 1, 0)

    def x_in_map(t):
        return mix_tile(t), 0

    def x_out_map(t):
        return ffn_tile(t), 0

    def mem_map(t):
        return layer, mix_tile(t) // tpb, 0, 0, 0

    def kt_map(t):
        return 0, mix_tile(t)

    def v_map(t):
        return mix_tile(t), 0

    mem_spec = pl.BlockSpec((None, None, N_MEM_HEADS, MEM_WIDTH, N_MEM), mem_map)
    in_specs = [
        pl.BlockSpec((tm, D_MODEL), x_in_map),
        _const_spec((None, 1, D_MODEL), (layer, 0, 0)),
        _const_spec((D_MODEL, proj_w), (0, 0)),
        mem_spec, mem_spec,
        _const_spec((D_MODEL, D_MODEL), (0, 0)),
        _const_spec((None, 1, D_MODEL), (layer, 0, 0)),
        _const_spec((D_MODEL, D_FF), (0, 0)),
        _const_spec((D_MODEL, D_FF), (0, 0)),
        _const_spec((D_FF, D_MODEL), (0, 0)),
    ]
    args = [x, p["norm_mix"], w["in"], p["mkt"], p["mv"], w["out"],
            p["norm_ffn"], w["gate"], w["up"], w["down"]]
    scratch = [pltpu.VMEM((tm, D_MODEL), F32), pltpu.VMEM((tm, D_MODEL), BF16),
               pltpu.VMEM((tm, D_MODEL), F32)]
    if mixer == "conv":
        in_specs.append(_const_spec((None, CONV_K, CONV_WIDTH), (sub, 0, 0)))
        args.append(p["conv_w"])
        scratch.append(pltpu.VMEM((tm + SUBLANES, CONV_WIDTH), F32))
    else:
        if kv == "compute":
            in_specs += [_const_spec((1, D_MODEL), (0, 0)),
                         _const_spec((KV_WIDTH, D_MODEL), (0, 0)),
                         _const_spec((D_MODEL, KV_WIDTH), (0, 0))]
            args += [p["kv_norm"], p["w_kt"], p["w_v"]]
        else:
            in_specs += [pl.BlockSpec((KV_WIDTH, tm), kt_map), pl.BlockSpec((tm, KV_WIDTH), v_map)]
            args += [p["kt"], p["v"]]
        in_specs += [_const_spec((N_Q_HEADS, BLOCK, 2 * BLOCK), (0, 0, 0)),
                     pl.BlockSpec(memory_space=pltpu.SMEM)]
        args += [p["bias"], p["sinks"]]
        scratch += [pltpu.VMEM((KV_WIDTH, BLOCK), BF16), pltpu.VMEM((BLOCK, KV_WIDTH), BF16)]
    if epilogue == "final":
        in_specs.append(_const_spec((1, D_MODEL), (0, 0)))
        args.append(p["final_norm"])

    out_specs = [pl.BlockSpec((tm, D_MODEL), x_out_map)]
    out_shape = [jax.ShapeDtypeStruct(x.shape, F32)]
    if kv == "compute":
        out_specs += [pl.BlockSpec((KV_WIDTH, tm), kt_map), pl.BlockSpec((tm, KV_WIDTH), v_map)]
        out_shape += [jax.ShapeDtypeStruct((KV_WIDTH, tokens), BF16),
                      jax.ShapeDtypeStruct((tokens, KV_WIDTH), BF16)]
    for stack, idx in cast_next:
        _, n_rows, n_cols = stack.shape
        rows = _cast_rows(n_rows, n_tiles)
        last = n_rows // rows - 1
        in_specs.append(pl.BlockSpec((None, rows, n_cols),
                                     lambda t, idx=idx, last=last: (idx, jnp.minimum(t, last), 0)))
        args.append(stack)
        out_specs.append(pl.BlockSpec((rows, n_cols),
                                      lambda t, last=last: (jnp.minimum(t, last), 0)))
        out_shape.append(jax.ShapeDtypeStruct((n_rows, n_cols), BF16))

    outs = pl.pallas_call(
        functools.partial(_layer_kernel, mixer=mixer, kv=kv, epilogue=epilogue, layer=sub, tm=tm,
                          tiles_per_batch=tpb, n_cast=len(cast_next)),
        grid=(n_tiles + 1,),
        in_specs=in_specs,
        out_specs=out_specs,
        out_shape=out_shape,
        scratch_shapes=scratch,
        compiler_params=pltpu.CompilerParams(
            dimension_semantics=("arbitrary",),
            vmem_limit_bytes=VMEM_LIMIT_BYTES),
        name=f"layer{layer}_{mixer}",
    )(*args)
    return outs


def kernel(x, mem, norm_mix, norm_ffn, a_w_in, a_conv_w, a_w_out, kv_norm, w_kv, b_w_q, b_sinks,
           b_w_out, rel_bias, mem_norm, w_mem_kv, w_gate, w_up, w_down, final_norm):
    mkt, mv = _mem_kv(mem, mem_norm, w_mem_kv)
    p = {
        "norm_mix": norm_mix.reshape(DEPTH, 1, D_MODEL),
        "norm_ffn": norm_ffn.reshape(DEPTH, 1, D_MODEL),
        "conv_w": a_conv_w,
        "mkt": mkt, "mv": mv,
        "kv_norm": kv_norm.reshape(1, D_MODEL),
        "w_kt": w_kv[:, :KV_WIDTH].T.astype(BF16),
        "w_v": w_kv[:, KV_WIDTH:].astype(BF16),
        "final_norm": final_norm.reshape(1, D_MODEL),
        "sinks": b_sinks.astype(F32),
        "bias": _band_bias(rel_bias),
    }

    def f32_weights(i):
        stack_in, stack_out, sub = (a_w_in, a_w_out, i) if i < N_A else (b_w_q, b_w_out, i - N_A)
        return {"in": (stack_in, sub), "out": (stack_out, sub),
                "gate": (w_gate, i), "up": (w_up, i), "down": (w_down, i)}

    w = {k: stack[idx].astype(BF16) for k, (stack, idx) in f32_weights(0).items()}
    batch, seq, _ = x.shape
    x = x.reshape(batch * seq, D_MODEL)
    for i in range(DEPTH):
        mixer = "conv" if i < N_A else "swa"
        sub = i if i < N_A else i - N_A
        kv = None if i < N_A else ("compute" if i == N_A else "load")
        epilogue = "final" if i == DEPTH - 1 else "none"
        nxt = f32_weights(i + 1) if i + 1 < DEPTH else {}
        outs = _layer(x, seq=seq, mixer=mixer, kv=kv, epilogue=epilogue, layer=i, sub=sub, w=w,
                      cast_next=list(nxt.values()), p=p)
        x = outs[0]
        if kv == "compute":
            p["kt"], p["v"] = outs[1], outs[2]
        w = dict(zip(nxt.keys(), outs[len(outs) - len(nxt):]))
    return x.reshape(batch, seq, D_MODEL)
```

```python
import functools
import math

import jax
import jax.numpy as jnp
from jax import lax
from jax.experimental import pallas as pl
from jax.experimental.pallas import tpu as pltpu

D_MODEL = 1024
DEPTH = 4
N_MEM = 256
HEAD_DIM = 64
N_MEM_HEADS = 4
MEM_WIDTH = N_MEM_HEADS * HEAD_DIM
CONV_WIDTH = D_MODEL - MEM_WIDTH
CONV_K = 3
N_Q_HEADS = CONV_WIDTH // HEAD_DIM
N_KV_HEADS = 4
GROUP = N_Q_HEADS // N_KV_HEADS
Q_WIDTH = N_Q_HEADS * HEAD_DIM
KV_WIDTH = N_KV_HEADS * HEAD_DIM
A_PROJ = 3 * CONV_WIDTH + MEM_WIDTH
B_PROJ = Q_WIDTH + MEM_WIDTH
WINDOW = 128
BLOCK = 128
REL_BUCKETS = 32
REL_MAX_DIST = 128
D_FF = ((8 * D_MODEL + 3 * 256 - 1) // (3 * 256)) * 256
N_A = DEPTH // 2
N_B = DEPTH - N_A
EPS = 1e-5

SCALE = HEAD_DIM ** -0.5
NEG = -1e30
LANES = 128
SUBLANES = 8
MXU_DIM = 256
TOKEN_TILE = 512
VMEM_LIMIT_BYTES = 56 * 1024 * 1024

BF16 = jnp.bfloat16
F32 = jnp.float32


def _dot(a, b):
    return jnp.dot(a, b, preferred_element_type=F32)


def _dot_nt(a, b):
    return lax.dot_general(a, b, (((1,), (1,)), ((), ())), preferred_element_type=F32)


def _rms(x, g):
    ms = jnp.mean(x * x, axis=-1, keepdims=True)
    return x * lax.rsqrt(ms + EPS) * g


def _mem_kv_kernel(mem_ref, g_ref, w_ref, kt_ref, v_ref, *, batch):
    mem_n = _rms(mem_ref[...], g_ref[...]).astype(BF16)
    kv = _dot(mem_n, w_ref[...].astype(BF16))
    row = lax.broadcasted_iota(jnp.int32, (MEM_WIDTH, N_MEM), 0)
    col = lax.broadcasted_iota(jnp.int32, (N_MEM, MEM_WIDTH), 1)
    for b in range(batch):
        kvb = kv[b * N_MEM:(b + 1) * N_MEM]
        kt = (kvb[:, :MEM_WIDTH] * SCALE).T
        v = kvb[:, MEM_WIDTH:]
        for h in range(N_MEM_HEADS):
            lo, hi = h * HEAD_DIM, (h + 1) * HEAD_DIM
            kt_ref[b, h] = jnp.where((row >= lo) & (row < hi), kt, 0.0).astype(BF16)
            ones_lo = (lo + LANES) % MEM_WIDTH
            fill = jnp.where((col >= ones_lo) & (col < ones_lo + HEAD_DIM), 1.0, 0.0)
            v_ref[b, h] = jnp.where((col >= lo) & (col < hi), v, fill).astype(BF16)


def _mem_kv(mem, mem_norm, w_mem_kv):
    batch = mem.shape[0]
    out_sds = jax.ShapeDtypeStruct((DEPTH, batch, N_MEM_HEADS, MEM_WIDTH, N_MEM), BF16)
    out_spec = pl.BlockSpec((None, batch, N_MEM_HEADS, MEM_WIDTH, N_MEM),
                            lambda i: (i, 0, 0, 0, 0))
    return pl.pallas_call(
        functools.partial(_mem_kv_kernel, batch=batch),
        grid=(DEPTH,),
        in_specs=[
            pl.BlockSpec((batch * N_MEM, D_MODEL), lambda i: (0, 0)),
            pl.BlockSpec((1, D_MODEL), lambda i: (0, 0)),
            pl.BlockSpec((None, D_MODEL, 2 * MEM_WIDTH), lambda i: (i, 0, 0)),
        ],
        out_specs=[out_spec, out_spec],
        out_shape=[out_sds, out_sds],
        compiler_params=pltpu.CompilerParams(dimension_semantics=("arbitrary",)),
        name="mem_kv",
    )(mem.reshape(batch * N_MEM, D_MODEL), mem_norm.reshape(1, D_MODEL), w_mem_kv)


def _bias_kernel(bucket_ref, inwin_ref, table_ref, out_ref):
    bucket = bucket_ref[...]
    inwin = inwin_ref[...] > 0
    for h in range(N_Q_HEADS):
        def body(b, acc, h=h):
            return jnp.where(bucket == b, table_ref[b, h], acc)
        acc = lax.fori_loop(0, REL_BUCKETS, body, jnp.zeros((BLOCK, 2 * BLOCK), F32))
        out_ref[h] = jnp.where(inwin, acc, NEG)


def _rel_bucket(dist):
    max_exact = REL_BUCKETS // 2
    d = jnp.maximum(dist, 1).astype(F32)
    large = max_exact + (jnp.log(d / max_exact) / math.log(REL_MAX_DIST / max_exact)
                         * (REL_BUCKETS - max_exact)).astype(jnp.int32)
    large = jnp.minimum(large, REL_BUCKETS - 1)
    return jnp.where(dist < max_exact, dist, large)


def _band_bias(rel_bias):
    qi = jnp.arange(BLOCK, dtype=jnp.int32)[:, None]
    kj = jnp.arange(2 * BLOCK, dtype=jnp.int32)[None, :]
    dist = qi + BLOCK - kj
    inwin = ((dist >= 0) & (dist < WINDOW)).astype(jnp.int32)
    bucket = _rel_bucket(jnp.maximum(dist, 0)).astype(jnp.int32)
    return pl.pallas_call(
        _bias_kernel,
        in_specs=[
            pl.BlockSpec(memory_space=pltpu.VMEM),
            pl.BlockSpec(memory_space=pltpu.VMEM),
            pl.BlockSpec(memory_space=pltpu.SMEM),
        ],
        out_specs=pl.BlockSpec(memory_space=pltpu.VMEM),
        out_shape=jax.ShapeDtypeStruct((N_Q_HEADS, BLOCK, 2 * BLOCK), F32),
        name="band_bias",
    )(bucket, inwin, rel_bias.astype(F32))


def _mem_attention(qm, mkt_ref, mv_ref):
    logits = [_dot(qm, mkt_ref[h]) for h in range(N_MEM_HEADS)]
    yield
    lane = lax.broadcasted_iota(jnp.int32, (qm.shape[0], LANES), 1)
    halves = [None, None]
    for h in range(N_MEM_HEADS):
        m = jnp.max(logits[h], axis=-1, keepdims=True)
        p = jnp.exp(logits[h] - m).astype(BF16)
        o = _dot(p, mv_ref[h])
        grp, pos = h // 2, h % 2
        val, s = o[:, grp * LANES:(grp + 1) * LANES], o[:, (1 - grp) * LANES:(2 - grp) * LANES]
        live = (lane >= pos * HEAD_DIM) & (lane < (pos + 1) * HEAD_DIM)
        part = jnp.where(live, val / jnp.where(live, s, 1.0), 0.0)
        halves[grp] = part if halves[grp] is None else halves[grp] + part
        yield
    return jnp.concatenate(halves, axis=-1)


def _short_conv(u, b_gate, c_gate, w, vs_ref, rows):
    v = c_gate * u
    base = SUBLANES
    vs_ref[base:base + rows, :] = v
    conv = (w[0:1] * vs_ref[base - 2:base - 2 + rows, :]
            + w[1:2] * vs_ref[base - 1:base - 1 + rows, :]
            + w[2:3] * v)
    return b_gate * conv


def _swa(q, kt_prev_ref, kt_cur_ref, v_prev_ref, v_cur_ref, bias_ref, sinks_ref, layer, first):
    col = lax.broadcasted_iota(jnp.int32, (BLOCK, 2 * BLOCK), 1)
    kill_prev = jnp.logical_and(first, col < BLOCK)
    zeros = jnp.zeros((HEAD_DIM, 2 * BLOCK), BF16)
    ones = jnp.ones((2 * BLOCK, LANES), BF16)

    def scores(n, hk):
        pair = slice((hk // 2) * LANES, (hk // 2 + 1) * LANES)
        if n == 0:
            kt = jnp.concatenate([kt_prev_ref[hk * HEAD_DIM:(hk + 1) * HEAD_DIM, :],
                                  kt_cur_ref[hk * HEAD_DIM:(hk + 1) * HEAD_DIM, 0:BLOCK]], axis=1)
            v2 = jnp.concatenate([v_prev_ref[:, pair], v_cur_ref[0:BLOCK, pair]], axis=0)
        else:
            keys = slice((n - 1) * BLOCK, (n + 1) * BLOCK)
            kt = kt_cur_ref[hk * HEAD_DIM:(hk + 1) * HEAD_DIM, keys]
            v2 = v_cur_ref[keys, pair]
        k_lo = jnp.concatenate([kt, zeros], axis=0)
        k_hi = jnp.concatenate([zeros, kt], axis=0)
        logits = []
        for g in range(GROUP):
            head = hk * GROUP + g
            grp, pos = head // 2, head % 2
            qg = q[n * BLOCK:(n + 1) * BLOCK, grp * LANES:(grp + 1) * LANES]
            bias = bias_ref[head]
            if n == 0:
                bias = jnp.where(kill_prev, NEG, bias)
            logits.append(_dot(qg, k_hi if pos else k_lo) + bias)
        return logits, v2

    def attend(hk, logits, v2):
        ps, sink_terms = [], []
        for g in range(GROUP):
            sink = sinks_ref[layer, hk * GROUP + g]
            m = jnp.maximum(jnp.max(logits[g], axis=-1, keepdims=True), sink)
            ps.append(jnp.exp(logits[g] - m).astype(BF16))
            sink_terms.append(jnp.exp(sink - m))
        o3 = _dot(jnp.concatenate(ps, axis=0), jnp.concatenate([v2, ones], axis=1))
        half = slice((hk % 2) * HEAD_DIM, (hk % 2 + 1) * HEAD_DIM)
        sums = slice(LANES + half.start, LANES + half.stop)
        outs = []
        for g in range(GROUP):
            rows = slice(g * BLOCK, (g + 1) * BLOCK)
            outs.append(o3[rows, half] / (o3[rows, sums] + sink_terms[g]))
        return outs

    units = [(i, hk) for i in range(q.shape[0] // BLOCK) for hk in range(N_KV_HEADS)]
    outs = {i: [] for i, _ in units}
    nxt = scores(*units[0])
    for k, (i, hk) in enumerate(units):
        cur = nxt
        if k + 1 < len(units):
            nxt = scores(*units[k + 1])
        yield
        outs[i] += attend(hk, *cur)
    return jnp.concatenate([jnp.concatenate(outs[i], axis=-1) for i in sorted(outs)], axis=0)


def _ffn(x_ref, h_ref, xf_ref, wg_ref, wu_ref, wd_ref):
    h = h_ref[...]
    acts = []
    for c in range(D_FF // MXU_DIM):
        sl = slice(c * MXU_DIM, (c + 1) * MXU_DIM)
        gate = _dot(h, wg_ref[:, sl])
        yield
        up = _dot(h, wu_ref[:, sl])
        acts.append((gate * jax.nn.sigmoid(gate) * up).astype(BF16))
        if c < D_MODEL // MXU_DIM:
            xf_ref[:, sl] = x_ref[:, sl]
        yield
    yield "tail"
    act = jnp.concatenate(acts, axis=-1)
    for n in range(D_MODEL // MXU_DIM):
        cs = slice(n * MXU_DIM, (n + 1) * MXU_DIM)
        yield cs, xf_ref[:, cs] + _dot(act, wd_ref[:, cs])


def _advance(gen):
    try:
        return next(gen)
    except StopIteration:
        return "end"


def _run(gen):
    while _advance(gen) != "end":
        pass


def _interleave(ffn, n_ffn, mix, n_mix, ffn_blocks_first):
    advance = _advance
    f_state = m_state = None
    done = 0
    for i in range(n_ffn):
        if f_state is None:
            f_state = advance(ffn)
        while m_state is None and done * n_ffn < (i + 1) * n_mix:
            m_state = advance(mix)
            done += 1
    while f_state is None:
        f_state = advance(ffn)
    while m_state is None:
        m_state = advance(mix)
    assert (f_state, m_state) == ("tail", "tail")
    for _ in range(ffn_blocks_first):
        advance(ffn)
    assert advance(mix) == "end"
    _run(ffn)


def _layer_kernel(*refs, mixer, kv, epilogue, layer, tm, tiles_per_batch, n_cast):
    it = iter(refs)
    x_ref, gmix_ref, win_ref, mkt_ref, mv_ref, wout_ref = (next(it) for _ in range(6))
    gffn_ref, wg_ref, wu_ref, wd_ref = (next(it) for _ in range(4))
    if mixer == "conv":
        convw_ref = next(it)
    else:
        if kv == "compute":
            gkv_ref, wk_ref, wv_ref = next(it), next(it), next(it)
        else:
            kt_in_ref, v_in_ref = next(it), next(it)
        bias_ref, sinks_ref = next(it), next(it)
    if epilogue == "final":
        gfin_ref = next(it)
    cast_in = [next(it) for _ in range(n_cast)]
    out_ref = next(it)
    if kv == "compute":
        kt_out_ref, v_out_ref = next(it), next(it)
    cast_out = [next(it) for _ in range(n_cast)]
    x1_ref, h2_ref, xf_ref = next(it), next(it), next(it)
    if mixer == "conv":
        vs_ref = next(it)
    else:
        kt_prev_ref, v_prev_ref = next(it), next(it)
        kt_cur_ref, v_cur_ref = (kt_out_ref, v_out_ref) if kv == "compute" else (kt_in_ref, v_in_ref)

    t = pl.program_id(0)
    n_tiles = pl.num_programs(0) - 1
    first = (t % tiles_per_batch) == 0

    @pl.when(first)
    def _():
        if mixer == "conv":
            vs_ref[0:SUBLANES, :] = jnp.zeros((SUBLANES, CONV_WIDTH), F32)
        else:
            kt_prev_ref[...] = jnp.zeros((KV_WIDTH, BLOCK), BF16)
            v_prev_ref[...] = jnp.zeros((BLOCK, KV_WIDTH), BF16)

    def mix():
        x = x_ref[...]
        h = _rms(x, gmix_ref[...]).astype(BF16)
        yield
        if kv == "compute":
            hkv = _rms(x, gkv_ref[...]).astype(BF16)
            kt_out_ref[...] = (_dot(hkv, wk_ref[...]) * SCALE).T.astype(BF16)
            yield
            v_out_ref[...] = _dot(hkv, wv_ref[...]).astype(BF16)
            yield
        widths = (CONV_WIDTH,) * 3 + (MEM_WIDTH,) if mixer == "conv" else (Q_WIDTH, MEM_WIDTH)
        fields, lo = [], 0
        for w in widths:
            fields.append(_dot(h, win_ref[:, lo:lo + w]))
            lo += w
            yield
        for src, dst in zip(cast_in, cast_out):
            dst[...] = src[...].astype(BF16)
        if mixer == "conv":
            u, b_gate, c_gate, qm = fields
            y_tok = _short_conv(u, b_gate, c_gate, convw_ref[...], vs_ref, tm)
            yield
        else:
            q, qm = fields
            y_tok = yield from _swa(q.astype(BF16), kt_prev_ref, kt_cur_ref, v_prev_ref, v_cur_ref,
                                    bias_ref, sinks_ref, layer, first)
        y_mem = yield from _mem_attention(qm.astype(BF16), mkt_ref, mv_ref)
        y = jnp.concatenate([y_tok, y_mem], axis=-1).astype(BF16)
        half = D_MODEL // 2
        lo_half = x[:, :half] + _dot(y, wout_ref[:, :half])
        yield
        hi_half = x[:, half:] + _dot(y, wout_ref[:, half:])
        yield "tail"
        x = jnp.concatenate([lo_half, hi_half], axis=-1)
        x1_ref[...] = x
        h2_ref[...] = _rms(x, gffn_ref[...]).astype(BF16)
        if mixer == "conv":
            vs_ref[0:SUBLANES, :] = vs_ref[tm:tm + SUBLANES, :]
        else:
            kt_prev_ref[...] = kt_cur_ref[:, tm - BLOCK:tm]
            v_prev_ref[...] = v_cur_ref[tm - BLOCK:tm, :]

    def ffn():
        blocks = []
        for item in _ffn(x1_ref, h2_ref, xf_ref, wg_ref, wu_ref, wd_ref):
            if isinstance(item, tuple):
                cs, block = item
                if epilogue == "none":
                    out_ref[:, cs] = block
                blocks.append(block)
                yield
            else:
                yield item
        if epilogue == "final":
            out_ref[...] = _rms(jnp.concatenate(blocks, axis=-1), gfin_ref[...])

    @pl.when(t == 0)
    def _():
        _run(mix())

    @pl.when(jnp.logical_and(t > 0, t < n_tiles))
    def _():
        n_mix = {"conv": 13, "swa": 10 + (tm // BLOCK) * N_KV_HEADS}[mixer]
        n_mix += 2 if kv == "compute" else 0
        _interleave(ffn(), 2 * (D_FF // MXU_DIM), mix(), n_mix, ffn_blocks_first=1)

    @pl.when(t == n_tiles)
    def _():
        _run(ffn())


def _const_spec(shape, index):
    return pl.BlockSpec(shape, lambda t: index, pipeline_mode=pl.Buffered(1))


def _cast_rows(n_rows, n_steps):
    tile = 2 * SUBLANES
    return min(r for r in range(tile, n_rows + 1, tile)
               if n_rows % r == 0 and n_rows // r <= n_steps)


def _layer(x, *, seq, mixer, kv, epilogue, layer, sub, w, cast_next, p):
    tokens, _ = x.shape
    tm = TOKEN_TILE
    assert seq % tm == 0 and tm % BLOCK == 0
    tpb = seq // tm
    n_tiles = tokens // tm
    proj_w = A_PROJ if mixer == "conv" else B_PROJ

    def mix_tile(t):
        return jnp.minimum(t, n_tiles - 1)

    def ffn_tile(t):
        return jnp.maximum(t - 1, 0)

    def x_in_map(t):
        return mix_tile(t), 0

    def x_out_map(t):
        return ffn_tile(t), 0

    def mem_map(t):
        return layer, mix_tile(t) // tpb, 0, 0, 0

    def kt_map(t):
        return 0, mix_tile(t)

    def v_map(t):
        return mix_tile(t), 0

    mem_spec = pl.BlockSpec((None, None, N_MEM_HEADS, MEM_WIDTH, N_MEM), mem_map)
    in_specs = [
        pl.BlockSpec((tm, D_MODEL), x_in_map),
        _const_spec((None, 1, D_MODEL), (layer, 0, 0)),
        _const_spec((D_MODEL, proj_w), (0, 0)),
        mem_spec, mem_spec,
        _const_spec((D_MODEL, D_MODEL), (0, 0)),
        _const_spec((None, 1, D_MODEL), (layer, 0, 0)),
        _const_spec((D_MODEL, D_FF), (0, 0)),
        _const_spec((D_MODEL, D_FF), (0, 0)),
        _const_spec((D_FF, D_MODEL), (0, 0)),
    ]
    args = [x, p["norm_mix"], w["in"], p["mkt"], p["mv"], w["out"],
            p["norm_ffn"], w["gate"], w["up"], w["down"]]
    scratch = [pltpu.VMEM((tm, D_MODEL), F32), pltpu.VMEM((tm, D_MODEL), BF16),
               pltpu.VMEM((tm, D_MODEL), F32)]
    if mixer == "conv":
        in_specs.append(_const_spec((None, CONV_K, CONV_WIDTH), (sub, 0, 0)))
        args.append(p["conv_w"])
        scratch.append(pltpu.VMEM((tm + SUBLANES, CONV_WIDTH), F32))
    else:
        if kv == "compute":
            in_specs += [_const_spec((1, D_MODEL), (0, 0)),
                         _const_spec((D_MODEL, KV_WIDTH), (0, 0)),
                         _const_spec((D_MODEL, KV_WIDTH), (0, 0))]
            args += [p["kv_norm"], p["w_k"], p["w_v"]]
        else:
            in_specs += [pl.BlockSpec((KV_WIDTH, tm), kt_map), pl.BlockSpec((tm, KV_WIDTH), v_map)]
            args += [p["kt"], p["v"]]
        in_specs += [_const_spec((N_Q_HEADS, BLOCK, 2 * BLOCK), (0, 0, 0)),
                     pl.BlockSpec(memory_space=pltpu.SMEM)]
        args += [p["bias"], p["sinks"]]
        scratch += [pltpu.VMEM((KV_WIDTH, BLOCK), BF16), pltpu.VMEM((BLOCK, KV_WIDTH), BF16)]
    if epilogue == "final":
        in_specs.append(_const_spec((1, D_MODEL), (0, 0)))
        args.append(p["final_norm"])

    out_specs = [pl.BlockSpec((tm, D_MODEL), x_out_map)]
    out_shape = [jax.ShapeDtypeStruct(x.shape, F32)]
    if kv == "compute":
        out_specs += [pl.BlockSpec((KV_WIDTH, tm), kt_map), pl.BlockSpec((tm, KV_WIDTH), v_map)]
        out_shape += [jax.ShapeDtypeStruct((KV_WIDTH, tokens), BF16),
                      jax.ShapeDtypeStruct((tokens, KV_WIDTH), BF16)]
    for stack, idx in cast_next:
        _, n_rows, n_cols = stack.shape
        rows = _cast_rows(n_rows, n_tiles)
        last = n_rows // rows - 1
        in_specs.append(pl.BlockSpec((None, rows, n_cols),
                                     lambda t, idx=idx, last=last: (idx, jnp.minimum(t, last), 0)))
        args.append(stack)
        out_specs.append(pl.BlockSpec((rows, n_cols),
                                      lambda t, last=last: (jnp.minimum(t, last), 0)))
        out_shape.append(jax.ShapeDtypeStruct((n_rows, n_cols), BF16))

    outs = pl.pallas_call(
        functools.partial(_layer_kernel, mixer=mixer, kv=kv, epilogue=epilogue, layer=sub, tm=tm,
                          tiles_per_batch=tpb, n_cast=len(cast_next)),
        grid=(n_tiles + 1,),
        in_specs=in_specs,
        out_specs=out_specs,
        out_shape=out_shape,
        scratch_shapes=scratch,
        compiler_params=pltpu.CompilerParams(
            dimension_semantics=("arbitrary",),
            vmem_limit_bytes=VMEM_LIMIT_BYTES),
        name=f"layer{layer}_{mixer}",
    )(*args)
    return outs


def kernel(x, mem, norm_mix, norm_ffn, a_w_in, a_conv_w, a_w_out, kv_norm, w_kv, b_w_q, b_sinks,
           b_w_out, rel_bias, mem_norm, w_mem_kv, w_gate, w_up, w_down, final_norm):
    mkt, mv = _mem_kv(mem, mem_norm, w_mem_kv)
    p = {
        "norm_mix": norm_mix.reshape(DEPTH, 1, D_MODEL),
        "norm_ffn": norm_ffn.reshape(DEPTH, 1, D_MODEL),
        "conv_w": a_conv_w,
        "mkt": mkt, "mv": mv,
        "kv_norm": kv_norm.reshape(1, D_MODEL),
        "w_k": w_kv[:, :KV_WIDTH].astype(BF16),
        "w_v": w_kv[:, KV_WIDTH:].astype(BF16),
        "final_norm": final_norm.reshape(1, D_MODEL),
        "sinks": b_sinks.astype(F32),
        "bias": _band_bias(rel_bias),
    }

    def f32_weights(i):
        stack_in, stack_out, sub = (a_w_in, a_w_out, i) if i < N_A else (b_w_q, b_w_out, i - N_A)
        return {"in": (stack_in, sub), "out": (stack_out, sub),
                "gate": (w_gate, i), "up": (w_up, i), "down": (w_down, i)}

    w = {k: stack[idx].astype(BF16) for k, (stack, idx) in f32_weights(0).items()}
    batch, seq, _ = x.shape
    x = x.reshape(batch * seq, D_MODEL)
    for i in range(DEPTH):
        mixer = "conv" if i < N_A else "swa"
        sub = i if i < N_A else i - N_A
        kv = None if i < N_A else ("compute" if i == N_A else "load")
        epilogue = "final" if i == DEPTH - 1 else "none"
        nxt = f32_weights(i + 1) if i + 1 < DEPTH else {}
        outs = _layer(x, seq=seq, mixer=mixer, kv=kv, epilogue=epilogue, layer=i, sub=sub, w=w,
                      cast_next=list(nxt.values()), p=p)
        x = outs[0]
        if kv == "compute":
            p["kt"], p["v"] = outs[1], outs[2]
        w = dict(zip(nxt.keys(), outs[len(outs) - len(nxt):]))
    return x.reshape(batch, seq, D_MODEL)
```

```python
import functools
import math

import jax
import jax.numpy as jnp
from jax import lax
from jax.experimental import pallas as pl
from jax.experimental.pallas import tpu as pltpu

D_MODEL = 1024
DEPTH = 4
N_MEM = 256
HEAD_DIM = 64
N_MEM_HEADS = 4
MEM_WIDTH = N_MEM_HEADS * HEAD_DIM
CONV_WIDTH = D_MODEL - MEM_WIDTH
CONV_K = 3
N_Q_HEADS = CONV_WIDTH // HEAD_DIM
N_KV_HEADS = 4
GROUP = N_Q_HEADS // N_KV_HEADS
Q_WIDTH = N_Q_HEADS * HEAD_DIM
KV_WIDTH = N_KV_HEADS * HEAD_DIM
A_PROJ = 3 * CONV_WIDTH + MEM_WIDTH
B_PROJ = Q_WIDTH + MEM_WIDTH
WINDOW = 128
BLOCK = 128
REL_BUCKETS = 32
REL_MAX_DIST = 128
D_FF = ((8 * D_MODEL + 3 * 256 - 1) // (3 * 256)) * 256
N_A = DEPTH // 2
N_B = DEPTH - N_A
EPS = 1e-5

SCALE = HEAD_DIM ** -0.5
NEG = -1e30
LANES = 128
SUBLANES = 8
MXU_DIM = 256
TOKEN_TILE = 512
V7X_VMEM_BYTES = 64 * 1024 * 1024
VMEM_LIMIT_BYTES = V7X_VMEM_BYTES - V7X_VMEM_BYTES // 8

BF16 = jnp.bfloat16
F32 = jnp.float32


def _dot(a, b):
    return jnp.dot(a, b, preferred_element_type=F32)


def _dot_nt(a, b):
    return lax.dot_general(a, b, (((1,), (1,)), ((), ())), preferred_element_type=F32)


def _rms(x, g):
    ms = jnp.mean(x * x, axis=-1, keepdims=True)
    return x * lax.rsqrt(ms + EPS) * g


def _mem_kv_kernel(mem_ref, g_ref, w_ref, kt_ref, v_ref, *, batch):
    mem_n = _rms(mem_ref[...], g_ref[...]).astype(BF16)
    kv = _dot(mem_n, w_ref[...].astype(BF16))
    row = lax.broadcasted_iota(jnp.int32, (MEM_WIDTH, N_MEM), 0)
    col = lax.broadcasted_iota(jnp.int32, (N_MEM, MEM_WIDTH), 1)
    for b in range(batch):
        kvb = kv[b * N_MEM:(b + 1) * N_MEM]
        kt = (kvb[:, :MEM_WIDTH] * SCALE).T
        v = kvb[:, MEM_WIDTH:]
        for h in range(N_MEM_HEADS):
            lo, hi = h * HEAD_DIM, (h + 1) * HEAD_DIM
            kt_ref[b, h] = jnp.where((row >= lo) & (row < hi), kt, 0.0).astype(BF16)
            ones_lo = (lo + LANES) % MEM_WIDTH
            fill = jnp.where((col >= ones_lo) & (col < ones_lo + HEAD_DIM), 1.0, 0.0)
            v_ref[b, h] = jnp.where((col >= lo) & (col < hi), v, fill).astype(BF16)


def _mem_kv(mem, mem_norm, w_mem_kv):
    batch = mem.shape[0]
    out_sds = jax.ShapeDtypeStruct((DEPTH, batch, N_MEM_HEADS, MEM_WIDTH, N_MEM), BF16)
    out_spec = pl.BlockSpec((None, batch, N_MEM_HEADS, MEM_WIDTH, N_MEM),
                            lambda i: (i, 0, 0, 0, 0))
    return pl.pallas_call(
        functools.partial(_mem_kv_kernel, batch=batch),
        grid=(DEPTH,),
        in_specs=[
            pl.BlockSpec((batch * N_MEM, D_MODEL), lambda i: (0, 0)),
            pl.BlockSpec((1, D_MODEL), lambda i: (0, 0)),
            pl.BlockSpec((None, D_MODEL, 2 * MEM_WIDTH), lambda i: (i, 0, 0)),
        ],
        out_specs=[out_spec, out_spec],
        out_shape=[out_sds, out_sds],
        compiler_params=pltpu.CompilerParams(dimension_semantics=("arbitrary",)),
        name="mem_kv",
    )(mem.reshape(batch * N_MEM, D_MODEL), mem_norm.reshape(1, D_MODEL), w_mem_kv)


def _bias_kernel(bucket_ref, inwin_ref, table_ref, out_ref):
    bucket = bucket_ref[...]
    inwin = inwin_ref[...] > 0
    for h in range(N_Q_HEADS):
        def body(b, acc, h=h):
            return jnp.where(bucket == b, table_ref[b, h], acc)
        acc = lax.fori_loop(0, REL_BUCKETS, body, jnp.zeros((BLOCK, 2 * BLOCK), F32))
        out_ref[h] = jnp.where(inwin, acc, NEG)


def _rel_bucket(dist):
    max_exact = REL_BUCKETS // 2
    d = jnp.maximum(dist, 1).astype(F32)
    large = max_exact + (jnp.log(d / max_exact) / math.log(REL_MAX_DIST / max_exact)
                         * (REL_BUCKETS - max_exact)).astype(jnp.int32)
    large = jnp.minimum(large, REL_BUCKETS - 1)
    return jnp.where(dist < max_exact, dist, large)


def _band_bias(rel_bias):
    qi = jnp.arange(BLOCK, dtype=jnp.int32)[:, None]
    kj = jnp.arange(2 * BLOCK, dtype=jnp.int32)[None, :]
    dist = qi + BLOCK - kj
    inwin = ((dist >= 0) & (dist < WINDOW)).astype(jnp.int32)
    bucket = _rel_bucket(jnp.maximum(dist, 0)).astype(jnp.int32)
    return pl.pallas_call(
        _bias_kernel,
        in_specs=[
            pl.BlockSpec(memory_space=pltpu.VMEM),
            pl.BlockSpec(memory_space=pltpu.VMEM),
            pl.BlockSpec(memory_space=pltpu.SMEM),
        ],
        out_specs=pl.BlockSpec(memory_space=pltpu.VMEM),
        out_shape=jax.ShapeDtypeStruct((N_Q_HEADS, BLOCK, 2 * BLOCK), F32),
        name="band_bias",
    )(bucket, inwin, rel_bias.astype(F32))


def _mem_attention(qm, mkt_ref, mv_ref):
    logits = [_dot(qm, mkt_ref[h]) for h in range(N_MEM_HEADS)]
    yield
    lane = lax.broadcasted_iota(jnp.int32, (qm.shape[0], LANES), 1)
    halves = [None, None]
    for h in range(N_MEM_HEADS):
        m = jnp.max(logits[h], axis=-1, keepdims=True)
        p = jnp.exp(logits[h] - m).astype(BF16)
        o = _dot(p, mv_ref[h])
        grp, pos = h // 2, h % 2
        val, s = o[:, grp * LANES:(grp + 1) * LANES], o[:, (1 - grp) * LANES:(2 - grp) * LANES]
        live = (lane >= pos * HEAD_DIM) & (lane < (pos + 1) * HEAD_DIM)
        part = jnp.where(live, val / jnp.where(live, s, 1.0), 0.0)
        halves[grp] = part if halves[grp] is None else halves[grp] + part
        yield
    return jnp.concatenate(halves, axis=-1)


def _short_conv(u, b_gate, c_gate, w, vs_ref, rows):
    v = c_gate * u
    base = SUBLANES
    vs_ref[base:base + rows, :] = v
    conv = (w[0:1] * vs_ref[base - 2:base - 2 + rows, :]
            + w[1:2] * vs_ref[base - 1:base - 1 + rows, :]
            + w[2:3] * v)
    return b_gate * conv


def _swa(q, kt_prev_ref, kt_cur_ref, v_prev_ref, v_cur_ref, bias_ref, sinks_ref, layer, first):
    col = lax.broadcasted_iota(jnp.int32, (BLOCK, 2 * BLOCK), 1)
    kill_prev = jnp.logical_and(first, col < BLOCK)
    zeros = jnp.zeros((HEAD_DIM, 2 * BLOCK), BF16)
    ones = jnp.ones((2 * BLOCK, LANES), BF16)

    def scores(n, hk):
        pair = slice((hk // 2) * LANES, (hk // 2 + 1) * LANES)
        if n == 0:
            kt = jnp.concatenate([kt_prev_ref[hk * HEAD_DIM:(hk + 1) * HEAD_DIM, :],
                                  kt_cur_ref[hk * HEAD_DIM:(hk + 1) * HEAD_DIM, 0:BLOCK]], axis=1)
            v2 = jnp.concatenate([v_prev_ref[:, pair], v_cur_ref[0:BLOCK, pair]], axis=0)
        else:
            keys = slice((n - 1) * BLOCK, (n + 1) * BLOCK)
            kt = kt_cur_ref[hk * HEAD_DIM:(hk + 1) * HEAD_DIM, keys]
            v2 = v_cur_ref[keys, pair]
        k_lo = jnp.concatenate([kt, zeros], axis=0)
        k_hi = jnp.concatenate([zeros, kt], axis=0)
        logits = []
        for g in range(GROUP):
            head = hk * GROUP + g
            grp, pos = head // 2, head % 2
            qg = q[n * BLOCK:(n + 1) * BLOCK, grp * LANES:(grp + 1) * LANES]
            bias = bias_ref[head]
            if n == 0:
                bias = jnp.where(kill_prev, NEG, bias)
            logits.append(_dot(qg, k_hi if pos else k_lo) + bias)
        return logits, v2

    def attend(hk, logits, v2):
        ps, sink_terms = [], []
        for g in range(GROUP):
            sink = sinks_ref[layer, hk * GROUP + g]
            m = jnp.maximum(jnp.max(logits[g], axis=-1, keepdims=True), sink)
            ps.append(jnp.exp(logits[g] - m).astype(BF16))
            sink_terms.append(jnp.exp(sink - m))
        o3 = _dot(jnp.concatenate(ps, axis=0), jnp.concatenate([v2, ones], axis=1))
        half = slice((hk % 2) * HEAD_DIM, (hk % 2 + 1) * HEAD_DIM)
        sums = slice(LANES + half.start, LANES + half.stop)
        outs = []
        for g in range(GROUP):
            rows = slice(g * BLOCK, (g + 1) * BLOCK)
            outs.append(o3[rows, half] / (o3[rows, sums] + sink_terms[g]))
        return outs

    units = [(i, hk) for i in range(q.shape[0] // BLOCK) for hk in range(N_KV_HEADS)]
    outs = {i: [] for i, _ in units}
    nxt = scores(*units[0])
    for k, (i, hk) in enumerate(units):
        cur = nxt
        if k + 1 < len(units):
            nxt = scores(*units[k + 1])
        yield
        outs[i] += attend(hk, *cur)
    return jnp.concatenate([jnp.concatenate(outs[i], axis=-1) for i in sorted(outs)], axis=0)


def _ffn(x_ref, h_ref, xf_ref, wg_ref, wu_ref, wd_ref):
    h = h_ref[...]
    acts = []
    for c in range(D_FF // MXU_DIM):
        sl = slice(c * MXU_DIM, (c + 1) * MXU_DIM)
        gate = _dot(h, wg_ref[:, sl])
        yield
        up = _dot(h, wu_ref[:, sl])
        acts.append((gate * jax.nn.sigmoid(gate) * up).astype(BF16))
        if c < D_MODEL // MXU_DIM:
            xf_ref[:, sl] = x_ref[:, sl]
        yield
    yield "tail"
    act = jnp.concatenate(acts, axis=-1)
    for n in range(D_MODEL // MXU_DIM):
        cs = slice(n * MXU_DIM, (n + 1) * MXU_DIM)
        yield cs, xf_ref[:, cs] + _dot(act, wd_ref[:, cs])


def _advance(gen):
    try:
        return next(gen)
    except StopIteration:
        return "end"


def _run(gen):
    while _advance(gen) != "end":
        pass


def _interleave(ffn, n_ffn, mix, n_mix, ffn_blocks_first):
    advance = _advance
    f_state = m_state = None
    done = 0
    for i in range(n_ffn):
        if f_state is None:
            f_state = advance(ffn)
        while m_state is None and done * n_ffn < (i + 1) * n_mix:
            m_state = advance(mix)
            done += 1
    while f_state is None:
        f_state = advance(ffn)
    while m_state is None:
        m_state = advance(mix)
    assert (f_state, m_state) == ("tail", "tail")
    for _ in range(ffn_blocks_first):
        advance(ffn)
    assert advance(mix) == "end"
    _run(ffn)


def _layer_kernel(*refs, mixer, kv, epilogue, layer, tm, tiles_per_batch, n_cast):
    it = iter(refs)
    x_ref, gmix_ref, win_ref, mkt_ref, mv_ref, wout_ref = (next(it) for _ in range(6))
    gffn_ref, wg_ref, wu_ref, wd_ref = (next(it) for _ in range(4))
    if mixer == "conv":
        convw_ref = next(it)
    else:
        if kv == "compute":
            xnext_ref, gkv_ref, wkt_ref, wv_ref = (next(it) for _ in range(4))
        else:
            kt_in_ref, v_in_ref = next(it), next(it)
        bias_ref, sinks_ref = next(it), next(it)
    if epilogue == "final":
        gfin_ref = next(it)
    cast_in = [next(it) for _ in range(n_cast)]
    out_ref = next(it)
    if kv == "compute":
        kt_out_ref, v_out_ref = next(it), next(it)
    cast_out = [next(it) for _ in range(n_cast)]
    x1_ref, h2_ref, xf_ref = next(it), next(it), next(it)
    if mixer == "conv":
        vs_ref = next(it)
    else:
        kt_prev_ref, v_prev_ref = next(it), next(it)
        if kv == "compute":
            kt_cur_ref, v_cur_ref = next(it), next(it)
        else:
            kt_cur_ref, v_cur_ref = kt_in_ref, v_in_ref

    t = pl.program_id(0)
    n_tiles = pl.num_programs(0) - 1
    first = (t % tiles_per_batch) == 0

    @pl.when(first)
    def _():
        if mixer == "conv":
            vs_ref[0:SUBLANES, :] = jnp.zeros((SUBLANES, CONV_WIDTH), F32)
        else:
            kt_prev_ref[...] = jnp.zeros((KV_WIDTH, BLOCK), BF16)
            v_prev_ref[...] = jnp.zeros((BLOCK, KV_WIDTH), BF16)

    def project_kv(src_ref):
        hkv = _rms(src_ref[...], gkv_ref[...]).astype(BF16)
        kt_cur_ref[...] = (_dot_nt(wkt_ref[...], hkv) * SCALE).astype(BF16)
        yield
        v_cur_ref[...] = _dot(hkv, wv_ref[...]).astype(BF16)
        yield

    def mix(prime=False):
        x = x_ref[...]
        h = _rms(x, gmix_ref[...]).astype(BF16)
        yield
        if kv == "compute":
            if prime:
                yield from project_kv(x_ref)
            kt_out_ref[...] = kt_cur_ref[...]
            v_out_ref[...] = v_cur_ref[...]
        widths = (CONV_WIDTH,) * 3 + (MEM_WIDTH,) if mixer == "conv" else (Q_WIDTH, MEM_WIDTH)
        fields, lo = [], 0
        for w in widths:
            fields.append(_dot(h, win_ref[:, lo:lo + w]))
            lo += w
            yield
        for src, dst in zip(cast_in, cast_out):
            dst[...] = src[...].astype(BF16)
        if mixer == "conv":
            u, b_gate, c_gate, qm = fields
            y_tok = _short_conv(u, b_gate, c_gate, convw_ref[...], vs_ref, tm)
            yield
        else:
            q, qm = fields
            y_tok = yield from _swa(q.astype(BF16), kt_prev_ref, kt_cur_ref, v_prev_ref, v_cur_ref,
                                    bias_ref, sinks_ref, layer, first)
            kt_prev_ref[...] = kt_cur_ref[:, tm - BLOCK:tm]
            v_prev_ref[...] = v_cur_ref[tm - BLOCK:tm, :]
            if kv == "compute":
                yield from project_kv(xnext_ref)
        y_mem = yield from _mem_attention(qm.astype(BF16), mkt_ref, mv_ref)
        y = jnp.concatenate([y_tok, y_mem], axis=-1).astype(BF16)
        half = D_MODEL // 2
        lo_half = x[:, :half] + _dot(y, wout_ref[:, :half])
        yield
        hi_half = x[:, half:] + _dot(y, wout_ref[:, half:])
        yield "tail"
        x = jnp.concatenate([lo_half, hi_half], axis=-1)
        x1_ref[...] = x
        h2_ref[...] = _rms(x, gffn_ref[...]).astype(BF16)
        if mixer == "conv":
            vs_ref[0:SUBLANES, :] = vs_ref[tm:tm + SUBLANES, :]

    def ffn():
        blocks = []
        for item in _ffn(x1_ref, h2_ref, xf_ref, wg_ref, wu_ref, wd_ref):
            if isinstance(item, tuple):
                cs, block = item
                if epilogue == "none":
                    out_ref[:, cs] = block
                blocks.append(block)
                yield
            else:
                yield item
        if epilogue == "final":
            out_ref[...] = _rms(jnp.concatenate(blocks, axis=-1), gfin_ref[...])

    @pl.when(t == 0)
    def _():
        _run(mix(prime=True))

    @pl.when(jnp.logical_and(t > 0, t < n_tiles))
    def _():
        n_mix = {"conv": 13, "swa": 10 + (tm // BLOCK) * N_KV_HEADS}[mixer]
        n_mix += 2 if kv == "compute" else 0
        _interleave(ffn(), 2 * (D_FF // MXU_DIM), mix(), n_mix, ffn_blocks_first=1)

    @pl.when(t == n_tiles)
    def _():
        _run(ffn())


def _const_spec(shape, index):
    return pl.BlockSpec(shape, lambda t: index, pipeline_mode=pl.Buffered(1))


def _cast_rows(n_rows, n_steps):
    tile = 2 * SUBLANES
    return min(r for r in range(tile, n_rows + 1, tile)
               if n_rows % r == 0 and n_rows // r <= n_steps)


def _layer(x, *, seq, mixer, kv, epilogue, layer, sub, w, cast_next, p):
    tokens, _ = x.shape
    tm = TOKEN_TILE
    assert seq % tm == 0 and tm % BLOCK == 0
    tpb = seq // tm
    n_tiles = tokens // tm
    proj_w = A_PROJ if mixer == "conv" else B_PROJ

    def mix_tile(t):
        return jnp.minimum(t, n_tiles - 1)

    def ffn_tile(t):
        return jnp.maximum(t - 1, 0)

    def x_in_map(t):
        return mix_tile(t), 0

    def x_out_map(t):
        return ffn_tile(t), 0

    def mem_map(t):
        return layer, mix_tile(t) // tpb, 0, 0, 0

    def kt_map(t):
        return 0, mix_tile(t)

    def v_map(t):
        return mix_tile(t), 0

    mem_spec = pl.BlockSpec((None, None, N_MEM_HEADS, MEM_WIDTH, N_MEM), mem_map)
    in_specs = [
        pl.BlockSpec((tm, D_MODEL), x_in_map),
        _const_spec((None, 1, D_MODEL), (layer, 0, 0)),
        _const_spec((D_MODEL, proj_w), (0, 0)),
        mem_spec, mem_spec,
        _const_spec((D_MODEL, D_MODEL), (0, 0)),
        _const_spec((None, 1, D_MODEL), (layer, 0, 0)),
        _const_spec((D_MODEL, D_FF), (0, 0)),
        _const_spec((D_MODEL, D_FF), (0, 0)),
        _const_spec((D_FF, D_MODEL), (0, 0)),
    ]
    args = [x, p["norm_mix"], w["in"], p["mkt"], p["mv"], w["out"],
            p["norm_ffn"], w["gate"], w["up"], w["down"]]
    scratch = [pltpu.VMEM((tm, D_MODEL), F32), pltpu.VMEM((tm, D_MODEL), BF16),
               pltpu.VMEM((tm, D_MODEL), F32)]
    if mixer == "conv":
        in_specs.append(_const_spec((None, CONV_K, CONV_WIDTH), (sub, 0, 0)))
        args.append(p["conv_w"])
        scratch.append(pltpu.VMEM((tm + SUBLANES, CONV_WIDTH), F32))
    else:
        if kv == "compute":
            in_specs += [pl.BlockSpec((tm, D_MODEL), lambda t: (jnp.minimum(t + 1, n_tiles - 1), 0)),
                         _const_spec((1, D_MODEL), (0, 0)),
                         _const_spec((KV_WIDTH, D_MODEL), (0, 0)),
                         _const_spec((D_MODEL, KV_WIDTH), (0, 0))]
            args += [x, p["kv_norm"], p["w_kt"], p["w_v"]]
        else:
            in_specs += [pl.BlockSpec((KV_WIDTH, tm), kt_map), pl.BlockSpec((tm, KV_WIDTH), v_map)]
            args += [p["kt"], p["v"]]
        in_specs += [_const_spec((N_Q_HEADS, BLOCK, 2 * BLOCK), (0, 0, 0)),
                     pl.BlockSpec(memory_space=pltpu.SMEM)]
        args += [p["bias"], p["sinks"]]
        scratch += [pltpu.VMEM((KV_WIDTH, BLOCK), BF16), pltpu.VMEM((BLOCK, KV_WIDTH), BF16)]
        if kv == "compute":
            scratch += [pltpu.VMEM((KV_WIDTH, tm), BF16), pltpu.VMEM((tm, KV_WIDTH), BF16)]
    if epilogue == "final":
        in_specs.append(_const_spec((1, D_MODEL), (0, 0)))
        args.append(p["final_norm"])

    out_specs = [pl.BlockSpec((tm, D_MODEL), x_out_map)]
    out_shape = [jax.ShapeDtypeStruct(x.shape, F32)]
    if kv == "compute":
        out_specs += [pl.BlockSpec((KV_WIDTH, tm), kt_map), pl.BlockSpec((tm, KV_WIDTH), v_map)]
        out_shape += [jax.ShapeDtypeStruct((KV_WIDTH, tokens), BF16),
                      jax.ShapeDtypeStruct((tokens, KV_WIDTH), BF16)]
    for stack, idx in cast_next:
        _, n_rows, n_cols = stack.shape
        rows = _cast_rows(n_rows, n_tiles)
        last = n_rows // rows - 1
        in_specs.append(pl.BlockSpec((None, rows, n_cols),
                                     lambda t, idx=idx, last=last: (idx, jnp.minimum(t, last), 0)))
        args.append(stack)
        out_specs.append(pl.BlockSpec((rows, n_cols),
                                      lambda t, last=last: (jnp.minimum(t, last), 0)))
        out_shape.append(jax.ShapeDtypeStruct((n_rows, n_cols), BF16))

    outs = pl.pallas_call(
        functools.partial(_layer_kernel, mixer=mixer, kv=kv, epilogue=epilogue, layer=sub, tm=tm,
                          tiles_per_batch=tpb, n_cast=len(cast_next)),
        grid=(n_tiles + 1,),
        in_specs=in_specs,
        out_specs=out_specs,
        out_shape=out_shape,
        scratch_shapes=scratch,
        compiler_params=pltpu.CompilerParams(
            dimension_semantics=("arbitrary",),
            vmem_limit_bytes=VMEM_LIMIT_BYTES),
        name=f"layer{layer}_{mixer}",
    )(*args)
    return outs


def kernel(x, mem, norm_mix, norm_ffn, a_w_in, a_conv_w, a_w_out, kv_norm, w_kv, b_w_q, b_sinks,
           b_w_out, rel_bias, mem_norm, w_mem_kv, w_gate, w_up, w_down, final_norm):
    mkt, mv = _mem_kv(mem, mem_norm, w_mem_kv)
    p = {
        "norm_mix": norm_mix.reshape(DEPTH, 1, D_MODEL),
        "norm_ffn": norm_ffn.reshape(DEPTH, 1, D_MODEL),
        "conv_w": a_conv_w,
        "mkt": mkt, "mv": mv,
        "kv_norm": kv_norm.reshape(1, D_MODEL),
        "w_kt": w_kv[:, :KV_WIDTH].T.astype(BF16),
        "w_v": w_kv[:, KV_WIDTH:].astype(BF16),
        "final_norm": final_norm.reshape(1, D_MODEL),
        "sinks": b_sinks.astype(F32),
        "bias": _band_bias(rel_bias),
    }

    def f32_weights(i):
        stack_in, stack_out, sub = (a_w_in, a_w_out, i) if i < N_A else (b_w_q, b_w_out, i - N_A)
        return {"in": (stack_in, sub), "out": (stack_out, sub),
                "gate": (w_gate, i), "up": (w_up, i), "down": (w_down, i)}

    w = {k: stack[idx].astype(BF16) for k, (stack, idx) in f32_weights(0).items()}
    batch, seq, _ = x.shape
    x = x.reshape(batch * seq, D_MODEL)
    for i in range(DEPTH):
        mixer = "conv" if i < N_A else "swa"
        sub = i if i < N_A else i - N_A
        kv = None if i < N_A else ("compute" if i == N_A else "load")
        epilogue = "final" if i == DEPTH - 1 else "none"
        nxt = f32_weights(i + 1) if i + 1 < DEPTH else {}
        outs = _layer(x, seq=seq, mixer=mixer, kv=kv, epilogue=epilogue, layer=i, sub=sub, w=w,
                      cast_next=list(nxt.values()), p=p)
        x = outs[0]
        if kv == "compute":
            p["kt"], p["v"] = outs[1], outs[2]
        w = dict(zip(nxt.keys(), outs[len(outs) - len(nxt):]))
    return x.reshape(batch, seq, D_MODEL)
```

```python
import functools
import math

import jax
import jax.numpy as jnp
from jax import lax
from jax.experimental import pallas as pl
from jax.experimental.pallas import tpu as pltpu

D_MODEL = 1024
DEPTH = 4
N_MEM = 256
HEAD_DIM = 64
N_MEM_HEADS = 4
MEM_WIDTH = N_MEM_HEADS * HEAD_DIM
CONV_WIDTH = D_MODEL - MEM_WIDTH
CONV_K = 3
N_Q_HEADS = CONV_WIDTH // HEAD_DIM
N_KV_HEADS = 4
GROUP = N_Q_HEADS // N_KV_HEADS
Q_WIDTH = N_Q_HEADS * HEAD_DIM
KV_WIDTH = N_KV_HEADS * HEAD_DIM
A_PROJ = 3 * CONV_WIDTH + MEM_WIDTH
B_PROJ = Q_WIDTH + MEM_WIDTH
WINDOW = 128
BLOCK = 128
REL_BUCKETS = 32
REL_MAX_DIST = 128
D_FF = ((8 * D_MODEL + 3 * 256 - 1) // (3 * 256)) * 256
N_A = DEPTH // 2
N_B = DEPTH - N_A
EPS = 1e-5

SCALE = HEAD_DIM ** -0.5
NEG = -1e30
LANES = 128
SUBLANES = 8
MXU_DIM = 256
TOKEN_TILE = 512
V7X_VMEM_BYTES = 64 * 1024 * 1024
VMEM_LIMIT_BYTES = V7X_VMEM_BYTES - V7X_VMEM_BYTES // 8

BF16 = jnp.bfloat16
F32 = jnp.float32


def _dot(a, b):
    return jnp.dot(a, b, preferred_element_type=F32)


def _dot_nt(a, b):
    return lax.dot_general(a, b, (((1,), (1,)), ((), ())), preferred_element_type=F32)


def _rms(x, g):
    ms = jnp.mean(x * x, axis=-1, keepdims=True)
    return x * lax.rsqrt(ms + EPS) * g


def _mem_kv_kernel(mem_ref, g_ref, w_ref, kt_ref, v_ref, *, batch):
    mem_n = _rms(mem_ref[...], g_ref[...]).astype(BF16)
    kv = _dot(mem_n, w_ref[...].astype(BF16))
    row = lax.broadcasted_iota(jnp.int32, (MEM_WIDTH, N_MEM), 0)
    col = lax.broadcasted_iota(jnp.int32, (N_MEM, MEM_WIDTH), 1)
    for b in range(batch):
        kvb = kv[b * N_MEM:(b + 1) * N_MEM]
        kt = (kvb[:, :MEM_WIDTH] * SCALE).T
        v = kvb[:, MEM_WIDTH:]
        for h in range(N_MEM_HEADS):
            lo, hi = h * HEAD_DIM, (h + 1) * HEAD_DIM
            kt_ref[b, h] = jnp.where((row >= lo) & (row < hi), kt, 0.0).astype(BF16)
            ones_lo = (lo + LANES) % MEM_WIDTH
            fill = jnp.where((col >= ones_lo) & (col < ones_lo + HEAD_DIM), 1.0, 0.0)
            v_ref[b, h] = jnp.where((col >= lo) & (col < hi), v, fill).astype(BF16)


def _mem_kv(mem, mem_norm, w_mem_kv):
    batch = mem.shape[0]
    out_sds = jax.ShapeDtypeStruct((DEPTH, batch, N_MEM_HEADS, MEM_WIDTH, N_MEM), BF16)
    out_spec = pl.BlockSpec((None, batch, N_MEM_HEADS, MEM_WIDTH, N_MEM),
                            lambda i: (i, 0, 0, 0, 0))
    return pl.pallas_call(
        functools.partial(_mem_kv_kernel, batch=batch),
        grid=(DEPTH,),
        in_specs=[
            pl.BlockSpec((batch * N_MEM, D_MODEL), lambda i: (0, 0)),
            pl.BlockSpec((1, D_MODEL), lambda i: (0, 0)),
            pl.BlockSpec((None, D_MODEL, 2 * MEM_WIDTH), lambda i: (i, 0, 0)),
        ],
        out_specs=[out_spec, out_spec],
        out_shape=[out_sds, out_sds],
        compiler_params=pltpu.CompilerParams(dimension_semantics=("arbitrary",)),
        name="mem_kv",
    )(mem.reshape(batch * N_MEM, D_MODEL), mem_norm.reshape(1, D_MODEL), w_mem_kv)


def _bias_kernel(bucket_ref, inwin_ref, table_ref, out_ref):
    bucket = bucket_ref[...]
    inwin = inwin_ref[...] > 0
    for h in range(N_Q_HEADS):
        acc = jnp.zeros((BLOCK, 2 * BLOCK), F32)
        for b in range(REL_BUCKETS):
            acc = jnp.where(bucket == b, table_ref[b, h], acc)
        out_ref[h] = jnp.where(inwin, acc, NEG)


def _rel_bucket(dist):
    max_exact = REL_BUCKETS // 2
    d = jnp.maximum(dist, 1).astype(F32)
    large = max_exact + (jnp.log(d / max_exact) / math.log(REL_MAX_DIST / max_exact)
                         * (REL_BUCKETS - max_exact)).astype(jnp.int32)
    large = jnp.minimum(large, REL_BUCKETS - 1)
    return jnp.where(dist < max_exact, dist, large)


def _band_bias(rel_bias):
    qi = jnp.arange(BLOCK, dtype=jnp.int32)[:, None]
    kj = jnp.arange(2 * BLOCK, dtype=jnp.int32)[None, :]
    dist = qi + BLOCK - kj
    inwin = ((dist >= 0) & (dist < WINDOW)).astype(jnp.int32)
    bucket = _rel_bucket(jnp.maximum(dist, 0)).astype(jnp.int32)
    return pl.pallas_call(
        _bias_kernel,
        in_specs=[
            pl.BlockSpec(memory_space=pltpu.VMEM),
            pl.BlockSpec(memory_space=pltpu.VMEM),
            pl.BlockSpec(memory_space=pltpu.SMEM),
        ],
        out_specs=pl.BlockSpec(memory_space=pltpu.VMEM),
        out_shape=jax.ShapeDtypeStruct((N_Q_HEADS, BLOCK, 2 * BLOCK), F32),
        name="band_bias",
    )(bucket, inwin, rel_bias.astype(F32))


def _mem_attention(qm, mkt_ref, mv_ref):
    logits = [_dot(qm, mkt_ref[h]) for h in range(N_MEM_HEADS)]
    yield
    lane = lax.broadcasted_iota(jnp.int32, (qm.shape[0], LANES), 1)
    halves = [None, None]
    for h in range(N_MEM_HEADS):
        m = jnp.max(logits[h], axis=-1, keepdims=True)
        p = jnp.exp(logits[h] - m).astype(BF16)
        o = _dot(p, mv_ref[h])
        grp, pos = h // 2, h % 2
        val, s = o[:, grp * LANES:(grp + 1) * LANES], o[:, (1 - grp) * LANES:(2 - grp) * LANES]
        live = (lane >= pos * HEAD_DIM) & (lane < (pos + 1) * HEAD_DIM)
        part = jnp.where(live, val / jnp.where(live, s, 1.0), 0.0)
        halves[grp] = part if halves[grp] is None else halves[grp] + part
        yield
    return jnp.concatenate(halves, axis=-1)


def _short_conv(u, b_gate, c_gate, w, vs_ref, rows):
    v = c_gate * u
    base = SUBLANES
    vs_ref[base:base + rows, :] = v
    conv = (w[0:1] * vs_ref[base - 2:base - 2 + rows, :]
            + w[1:2] * vs_ref[base - 1:base - 1 + rows, :]
            + w[2:3] * v)
    return b_gate * conv


def _swa(q, kt_prev_ref, kt_cur_ref, v_prev_ref, v_cur_ref, bias_ref, sinks_ref, layer, first):
    col = lax.broadcasted_iota(jnp.int32, (BLOCK, 2 * BLOCK), 1)
    kill_prev = jnp.logical_and(first, col < BLOCK)
    zeros = jnp.zeros((HEAD_DIM, 2 * BLOCK), BF16)
    ones = jnp.ones((2 * BLOCK, LANES), BF16)

    def scores(n, hk):
        pair = slice((hk // 2) * LANES, (hk // 2 + 1) * LANES)
        if n == 0:
            kt = jnp.concatenate([kt_prev_ref[hk * HEAD_DIM:(hk + 1) * HEAD_DIM, :],
                                  kt_cur_ref[hk * HEAD_DIM:(hk + 1) * HEAD_DIM, 0:BLOCK]], axis=1)
            v2 = jnp.concatenate([v_prev_ref[:, pair], v_cur_ref[0:BLOCK, pair]], axis=0)
        else:
            keys = slice((n - 1) * BLOCK, (n + 1) * BLOCK)
            kt = kt_cur_ref[hk * HEAD_DIM:(hk + 1) * HEAD_DIM, keys]
            v2 = v_cur_ref[keys, pair]
        k_lo = jnp.concatenate([kt, zeros], axis=0)
        k_hi = jnp.concatenate([zeros, kt], axis=0)
        logits = []
        for g in range(GROUP):
            head = hk * GROUP + g
            grp, pos = head // 2, head % 2
            qg = q[n * BLOCK:(n + 1) * BLOCK, grp * LANES:(grp + 1) * LANES]
            bias = bias_ref[head]
            if n == 0:
                bias = jnp.where(kill_prev, NEG, bias)
            logits.append(_dot(qg, k_hi if pos else k_lo) + bias)
        return logits, v2

    def attend(hk, logits, v2):
        ps, sink_terms = [], []
        for g in range(GROUP):
            sink = sinks_ref[layer, hk * GROUP + g]
            m = jnp.maximum(jnp.max(logits[g], axis=-1, keepdims=True), sink)
            ps.append(jnp.exp(logits[g] - m).astype(BF16))
            sink_terms.append(jnp.exp(sink - m))
        o3 = _dot(jnp.concatenate(ps, axis=0), jnp.concatenate([v2, ones], axis=1))
        half = slice((hk % 2) * HEAD_DIM, (hk % 2 + 1) * HEAD_DIM)
        sums = slice(LANES + half.start, LANES + half.stop)
        outs = []
        for g in range(GROUP):
            rows = slice(g * BLOCK, (g + 1) * BLOCK)
            outs.append(o3[rows, half] / (o3[rows, sums] + sink_terms[g]))
        return outs

    units = [(i, hk) for i in range(q.shape[0] // BLOCK) for hk in range(N_KV_HEADS)]
    outs = {i: [] for i, _ in units}
    nxt = scores(*units[0])
    for k, (i, hk) in enumerate(units):
        cur = nxt
        if k + 1 < len(units):
            nxt = scores(*units[k + 1])
        yield
        outs[i] += attend(hk, *cur)
    return jnp.concatenate([jnp.concatenate(outs[i], axis=-1) for i in sorted(outs)], axis=0)


def _ffn(x_ref, h_ref, xf_ref, wg_ref, wu_ref, wd_ref):
    h = h_ref[...]
    acts = []
    for c in range(D_FF // MXU_DIM):
        sl = slice(c * MXU_DIM, (c + 1) * MXU_DIM)
        gate = _dot(h, wg_ref[:, sl])
        yield
        up = _dot(h, wu_ref[:, sl])
        acts.append((gate * jax.nn.sigmoid(gate) * up).astype(BF16))
        if c < D_MODEL // MXU_DIM:
            xf_ref[:, sl] = x_ref[:, sl]
        yield
    yield "tail"
    act = jnp.concatenate(acts, axis=-1)
    for n in range(D_MODEL // MXU_DIM):
        cs = slice(n * MXU_DIM, (n + 1) * MXU_DIM)
        yield cs, xf_ref[:, cs] + _dot(act, wd_ref[:, cs])


def _advance(gen):
    try:
        return next(gen)
    except StopIteration:
        return "end"


def _run(gen):
    while _advance(gen) != "end":
        pass


def _interleave(ffn, n_ffn, mix, n_mix, ffn_blocks_first):
    advance = _advance
    f_state = m_state = None
    done = 0
    for i in range(n_ffn):
        if f_state is None:
            f_state = advance(ffn)
        while m_state is None and done * n_ffn < (i + 1) * n_mix:
            m_state = advance(mix)
            done += 1
    while f_state is None:
        f_state = advance(ffn)
    while m_state is None:
        m_state = advance(mix)
    assert (f_state, m_state) == ("tail", "tail")
    for _ in range(ffn_blocks_first):
        advance(ffn)
    assert advance(mix) == "end"
    _run(ffn)


def _layer_kernel(*refs, mixer, kv, epilogue, layer, tm, tiles_per_batch, n_cast):
    it = iter(refs)
    x_ref, gmix_ref, win_ref, mkt_ref, mv_ref, wout_ref = (next(it) for _ in range(6))
    gffn_ref, wg_ref, wu_ref, wd_ref = (next(it) for _ in range(4))
    if mixer == "conv":
        convw_ref = next(it)
    else:
        if kv == "compute":
            gkv_ref, wkt_ref, wv_ref = next(it), next(it), next(it)
        else:
            kt_in_ref, v_in_ref = next(it), next(it)
        bias_ref, sinks_ref = next(it), next(it)
    if epilogue == "final":
        gfin_ref = next(it)
    cast_in = [next(it) for _ in range(n_cast)]
    out_ref = next(it)
    if kv == "compute":
        kt_out_ref, v_out_ref = next(it), next(it)
    cast_out = [next(it) for _ in range(n_cast)]
    x1_ref, h2_ref, xf_ref = next(it), next(it), next(it)
    if mixer == "conv":
        vs_ref = next(it)
    else:
        kt_prev_ref, v_prev_ref = next(it), next(it)
        kt_cur_ref, v_cur_ref = (kt_out_ref, v_out_ref) if kv == "compute" else (kt_in_ref, v_in_ref)

    t = pl.program_id(0)
    n_tiles = pl.num_programs(0) - 1
    first = (t % tiles_per_batch) == 0

    @pl.when(first)
    def _():
        if mixer == "conv":
            vs_ref[0:SUBLANES, :] = jnp.zeros((SUBLANES, CONV_WIDTH), F32)
        else:
            kt_prev_ref[...] = jnp.zeros((KV_WIDTH, BLOCK), BF16)
            v_prev_ref[...] = jnp.zeros((BLOCK, KV_WIDTH), BF16)

    def mix():
        x = x_ref[...]
        h = _rms(x, gmix_ref[...]).astype(BF16)
        yield
        if kv == "compute":
            hkv = _rms(x, gkv_ref[...]).astype(BF16)
            kt_out_ref[...] = (_dot_nt(wkt_ref[...], hkv) * SCALE).astype(BF16)
            yield
            v_out_ref[...] = _dot(hkv, wv_ref[...]).astype(BF16)
            yield
        widths = (CONV_WIDTH,) * 3 + (MEM_WIDTH,) if mixer == "conv" else (Q_WIDTH, MEM_WIDTH)
        fields, lo = [], 0
        for w in widths:
            fields.append(_dot(h, win_ref[:, lo:lo + w]))
            lo += w
            yield
        for src, dst in zip(cast_in, cast_out):
            dst[...] = src[...].astype(BF16)
        if mixer == "conv":
            u, b_gate, c_gate, qm = fields
            y_tok = _short_conv(u, b_gate, c_gate, convw_ref[...], vs_ref, tm)
            yield
        else:
            q, qm = fields
            y_tok = yield from _swa(q.astype(BF16), kt_prev_ref, kt_cur_ref, v_prev_ref, v_cur_ref,
                                    bias_ref, sinks_ref, layer, first)
        y_mem = yield from _mem_attention(qm.astype(BF16), mkt_ref, mv_ref)
        y = jnp.concatenate([y_tok, y_mem], axis=-1).astype(BF16)
        half = D_MODEL // 2
        lo_half = x[:, :half] + _dot(y, wout_ref[:, :half])
        yield
        hi_half = x[:, half:] + _dot(y, wout_ref[:, half:])
        yield "tail"
        x = jnp.concatenate([lo_half, hi_half], axis=-1)
        x1_ref[...] = x
        h2_ref[...] = _rms(x, gffn_ref[...]).astype(BF16)
        if mixer == "conv":
            vs_ref[0:SUBLANES, :] = vs_ref[tm:tm + SUBLANES, :]
        else:
            kt_prev_ref[...] = kt_cur_ref[:, tm - BLOCK:tm]
            v_prev_ref[...] = v_cur_ref[tm - BLOCK:tm, :]

    def ffn():
        blocks = []
        for item in _ffn(x1_ref, h2_ref, xf_ref, wg_ref, wu_ref, wd_ref):
            if isinstance(item, tuple):
                cs, block = item
                if epilogue == "none":
                    out_ref[:, cs] = block
                blocks.append(block)
                yield
            else:
                yield item
        if epilogue == "final":
            out_ref[...] = _rms(jnp.concatenate(blocks, axis=-1), gfin_ref[...])

    @pl.when(t == 0)
    def _():
        _run(mix())

    @pl.when(jnp.logical_and(t > 0, t < n_tiles))
    def _():
        n_mix = {"conv": 13, "swa": 10 + (tm // BLOCK) * N_KV_HEADS}[mixer]
        n_mix += 2 if kv == "compute" else 0
        _interleave(ffn(), 2 * (D_FF // MXU_DIM), mix(), n_mix, ffn_blocks_first=1)

    @pl.when(t == n_tiles)
    def _():
        _run(ffn())


def _const_spec(shape, index):
    return pl.BlockSpec(shape, lambda t: index, pipeline_mode=pl.Buffered(1))


def _cast_rows(n_rows, n_steps):
    tile = 2 * SUBLANES
    return min(r for r in range(tile, n_rows + 1, tile)
               if n_rows % r == 0 and n_rows // r <= n_steps)


def _layer(x, *, seq, mixer, kv, epilogue, layer, sub, w, cast_next, p):
    tokens, _ = x.shape
    tm = TOKEN_TILE
    assert seq % tm == 0 and tm % BLOCK == 0
    tpb = seq // tm
    n_tiles = tokens // tm
    proj_w = A_PROJ if mixer == "conv" else B_PROJ

    def mix_tile(t):
        return jnp.minimum(t, n_tiles - 1)

    def ffn_tile(t):
        return jnp.maximum(t - 1, 0)

    def x_in_map(t):
        return mix_tile(t), 0

    def x_out_map(t):
        return ffn_tile(t), 0

    def mem_map(t):
        return layer, mix_tile(t) // tpb, 0, 0, 0

    def kt_map(t):
        return 0, mix_tile(t)

    def v_map(t):
        return mix_tile(t), 0

    mem_spec = pl.BlockSpec((None, None, N_MEM_HEADS, MEM_WIDTH, N_MEM), mem_map)
    in_specs = [
        pl.BlockSpec((tm, D_MODEL), x_in_map),
        _const_spec((None, 1, D_MODEL), (layer, 0, 0)),
        _const_spec((D_MODEL, proj_w), (0, 0)),
        mem_spec, mem_spec,
        _const_spec((D_MODEL, D_MODEL), (0, 0)),
        _const_spec((None, 1, D_MODEL), (layer, 0, 0)),
        _const_spec((D_MODEL, D_FF), (0, 0)),
        _const_spec((D_MODEL, D_FF), (0, 0)),
        _const_spec((D_FF, D_MODEL), (0, 0)),
    ]
    args = [x, p["norm_mix"], w["in"], p["mkt"], p["mv"], w["out"],
            p["norm_ffn"], w["gate"], w["up"], w["down"]]
    scratch = [pltpu.VMEM((tm, D_MODEL), F32), pltpu.VMEM((tm, D_MODEL), BF16),
               pltpu.VMEM((tm, D_MODEL), F32)]
    if mixer == "conv":
        in_specs.append(_const_spec((None, CONV_K, CONV_WIDTH), (sub, 0, 0)))
        args.append(p["conv_w"])
        scratch.append(pltpu.VMEM((tm + SUBLANES, CONV_WIDTH), F32))
    else:
        if kv == "compute":
            in_specs += [_const_spec((1, D_MODEL), (0, 0)),
                         _const_spec((KV_WIDTH, D_MODEL), (0, 0)),
                         _const_spec((D_MODEL, KV_WIDTH), (0, 0))]
            args += [p["kv_norm"], p["w_kt"], p["w_v"]]
        else:
            in_specs += [pl.BlockSpec((KV_WIDTH, tm), kt_map), pl.BlockSpec((tm, KV_WIDTH), v_map)]
            args += [p["kt"], p["v"]]
        in_specs += [_const_spec((N_Q_HEADS, BLOCK, 2 * BLOCK), (0, 0, 0)),
                     pl.BlockSpec(memory_space=pltpu.SMEM)]
        args += [p["bias"], p["sinks"]]
        scratch += [pltpu.VMEM((KV_WIDTH, BLOCK), BF16), pltpu.VMEM((BLOCK, KV_WIDTH), BF16)]
    if epilogue == "final":
        in_specs.append(_const_spec((1, D_MODEL), (0, 0)))
        args.append(p["final_norm"])

    out_specs = [pl.BlockSpec((tm, D_MODEL), x_out_map)]
    out_shape = [jax.ShapeDtypeStruct(x.shape, F32)]
    if kv == "compute":
        out_specs += [pl.BlockSpec((KV_WIDTH, tm), kt_map), pl.BlockSpec((tm, KV_WIDTH), v_map)]
        out_shape += [jax.ShapeDtypeStruct((KV_WIDTH, tokens), BF16),
                      jax.ShapeDtypeStruct((tokens, KV_WIDTH), BF16)]
    for stack, idx in cast_next:
        _, n_rows, n_cols = stack.shape
        rows = _cast_rows(n_rows, n_tiles)
        last = n_rows // rows - 1
        in_specs.append(pl.BlockSpec((None, rows, n_cols),
                                     lambda t, idx=idx, last=last: (idx, jnp.minimum(t, last), 0)))
        args.append(stack)
        out_specs.append(pl.BlockSpec((rows, n_cols),
                                      lambda t, last=last: (jnp.minimum(t, last), 0)))
        out_shape.append(jax.ShapeDtypeStruct((n_rows, n_cols), BF16))

    outs = pl.pallas_call(
        functools.partial(_layer_kernel, mixer=mixer, kv=kv, epilogue=epilogue, layer=sub, tm=tm,
                          tiles_per_batch=tpb, n_cast=len(cast_next)),
        grid=(n_tiles + 1,),
        in_specs=in_specs,
        out_specs=out_specs,
        out_shape=out_shape,
        scratch_shapes=scratch,
        compiler_params=pltpu.CompilerParams(
            dimension_semantics=("arbitrary",),
            vmem_limit_bytes=VMEM_LIMIT_BYTES),
        name=f"layer{layer}_{mixer}",
    )(*args)
    return outs


def kernel(x, mem, norm_mix, norm_ffn, a_w_in, a_conv_w, a_w_out, kv_norm, w_kv, b_w_q, b_sinks,
           b_w_out, rel_bias, mem_norm, w_mem_kv, w_gate, w_up, w_down, final_norm):
    mkt, mv = _mem_kv(mem, mem_norm, w_mem_kv)
    p = {
        "norm_mix": norm_mix.reshape(DEPTH, 1, D_MODEL),
        "norm_ffn": norm_ffn.reshape(DEPTH, 1, D_MODEL),
        "conv_w": a_conv_w,
        "mkt": mkt, "mv": mv,
        "kv_norm": kv_norm.reshape(1, D_MODEL),
        "w_kt": w_kv[:, :KV_WIDTH].T.astype(BF16),
        "w_v": w_kv[:, KV_WIDTH:].astype(BF16),
        "final_norm": final_norm.reshape(1, D_MODEL),
        "sinks": b_sinks.astype(F32),
        "bias": _band_bias(rel_bias),
    }

    def f32_weights(i):
        stack_in, stack_out, sub = (a_w_in, a_w_out, i) if i < N_A else (b_w_q, b_w_out, i - N_A)
        return {"in": (stack_in, sub), "out": (stack_out, sub),
                "gate": (w_gate, i), "up": (w_up, i), "down": (w_down, i)}

    w = {k: stack[idx].astype(BF16) for k, (stack, idx) in f32_weights(0).items()}
    batch, seq, _ = x.shape
    x = x.reshape(batch * seq, D_MODEL)
    for i in range(DEPTH):
        mixer = "conv" if i < N_A else "swa"
        sub = i if i < N_A else i - N_A
        kv = None if i < N_A else ("compute" if i == N_A else "load")
        epilogue = "final" if i == DEPTH - 1 else "none"
        nxt = f32_weights(i + 1) if i + 1 < DEPTH else {}
        outs = _layer(x, seq=seq, mixer=mixer, kv=kv, epilogue=epilogue, layer=i, sub=sub, w=w,
                      cast_next=list(nxt.values()), p=p)
        x = outs[0]
        if kv == "compute":
            p["kt"], p["v"] = outs[1], outs[2]
        w = dict(zip(nxt.keys(), outs[len(outs) - len(nxt):]))
    return x.reshape(batch, seq, D_MODEL)
```

```python
import functools
import math

import jax
import jax.numpy as jnp
from jax import lax
from jax.experimental import pallas as pl
from jax.experimental.pallas import tpu as pltpu

D_MODEL = 1024
DEPTH = 4
N_MEM = 256
HEAD_DIM = 64
N_MEM_HEADS = 4
MEM_WIDTH = N_MEM_HEADS * HEAD_DIM
CONV_WIDTH = D_MODEL - MEM_WIDTH
CONV_K = 3
N_Q_HEADS = CONV_WIDTH // HEAD_DIM
N_KV_HEADS = 4
GROUP = N_Q_HEADS // N_KV_HEADS
Q_WIDTH = N_Q_HEADS * HEAD_DIM
KV_WIDTH = N_KV_HEADS * HEAD_DIM
A_PROJ = 3 * CONV_WIDTH + MEM_WIDTH
B_PROJ = Q_WIDTH + MEM_WIDTH
WINDOW = 128
BLOCK = 128
REL_BUCKETS = 32
REL_MAX_DIST = 128
D_FF = ((8 * D_MODEL + 3 * 256 - 1) // (3 * 256)) * 256
N_A = DEPTH // 2
N_B = DEPTH - N_A
EPS = 1e-5

SCALE = HEAD_DIM ** -0.5
NEG = -1e30
LANES = 128
SUBLANES = 8
MXU_DIM = 256
TOKEN_TILE = 512
V7X_VMEM_BYTES = 64 * 1024 * 1024
VMEM_LIMIT_BYTES = V7X_VMEM_BYTES - V7X_VMEM_BYTES // 8

BF16 = jnp.bfloat16
F32 = jnp.float32


def _dot(a, b):
    return jnp.dot(a, b, preferred_element_type=F32)


def _dot_nt(a, b):
    return lax.dot_general(a, b, (((1,), (1,)), ((), ())), preferred_element_type=F32)


def _rms(x, g):
    ms = jnp.mean(x * x, axis=-1, keepdims=True)
    return x * lax.rsqrt(ms + EPS) * g


def _mem_kv_kernel(mem_ref, g_ref, w_ref, kt_ref, v_ref, *, batch):
    mem_n = _rms(mem_ref[...], g_ref[...]).astype(BF16)
    kv = _dot(mem_n, w_ref[...].astype(BF16))
    row = lax.broadcasted_iota(jnp.int32, (MEM_WIDTH, N_MEM), 0)
    col = lax.broadcasted_iota(jnp.int32, (N_MEM, MEM_WIDTH), 1)
    for b in range(batch):
        kvb = kv[b * N_MEM:(b + 1) * N_MEM]
        kt = (kvb[:, :MEM_WIDTH] * SCALE).T
        v = kvb[:, MEM_WIDTH:]
        for h in range(N_MEM_HEADS):
            lo, hi = h * HEAD_DIM, (h + 1) * HEAD_DIM
            kt_ref[b, h] = jnp.where((row >= lo) & (row < hi), kt, 0.0).astype(BF16)
            ones_lo = (lo + LANES) % MEM_WIDTH
            fill = jnp.where((col >= ones_lo) & (col < ones_lo + HEAD_DIM), 1.0, 0.0)
            v_ref[b, h] = jnp.where((col >= lo) & (col < hi), v, fill).astype(BF16)


def _mem_kv(mem, mem_norm, w_mem_kv):
    batch = mem.shape[0]
    out_sds = jax.ShapeDtypeStruct((DEPTH, batch, N_MEM_HEADS, MEM_WIDTH, N_MEM), BF16)
    out_spec = pl.BlockSpec((None, batch, N_MEM_HEADS, MEM_WIDTH, N_MEM),
                            lambda i: (i, 0, 0, 0, 0))
    return pl.pallas_call(
        functools.partial(_mem_kv_kernel, batch=batch),
        grid=(DEPTH,),
        in_specs=[
            pl.BlockSpec((batch * N_MEM, D_MODEL), lambda i: (0, 0)),
            pl.BlockSpec((1, D_MODEL), lambda i: (0, 0)),
            pl.BlockSpec((None, D_MODEL, 2 * MEM_WIDTH), lambda i: (i, 0, 0)),
        ],
        out_specs=[out_spec, out_spec],
        out_shape=[out_sds, out_sds],
        compiler_params=pltpu.CompilerParams(dimension_semantics=("arbitrary",)),
        name="mem_kv",
    )(mem.reshape(batch * N_MEM, D_MODEL), mem_norm.reshape(1, D_MODEL), w_mem_kv)


def _bias_kernel(bucket_ref, inwin_ref, table_ref, out_ref):
    bucket = bucket_ref[...]
    inwin = inwin_ref[...] > 0
    for h in range(N_Q_HEADS):
        acc = jnp.zeros((BLOCK, 2 * BLOCK), F32)
        for b in range(REL_BUCKETS):
            acc = jnp.where(bucket == b, table_ref[b, h], acc)
        out_ref[h] = jnp.where(inwin, acc, NEG)


def _rel_bucket(dist):
    max_exact = REL_BUCKETS // 2
    d = jnp.maximum(dist, 1).astype(F32)
    large = max_exact + (jnp.log(d / max_exact) / math.log(REL_MAX_DIST / max_exact)
                         * (REL_BUCKETS - max_exact)).astype(jnp.int32)
    large = jnp.minimum(large, REL_BUCKETS - 1)
    return jnp.where(dist < max_exact, dist, large)


def _band_bias(rel_bias):
    qi = jnp.arange(BLOCK, dtype=jnp.int32)[:, None]
    kj = jnp.arange(2 * BLOCK, dtype=jnp.int32)[None, :]
    dist = qi + BLOCK - kj
    inwin = ((dist >= 0) & (dist < WINDOW)).astype(jnp.int32)
    bucket = _rel_bucket(jnp.maximum(dist, 0)).astype(jnp.int32)
    return pl.pallas_call(
        _bias_kernel,
        in_specs=[
            pl.BlockSpec(memory_space=pltpu.VMEM),
            pl.BlockSpec(memory_space=pltpu.VMEM),
            pl.BlockSpec(memory_space=pltpu.SMEM),
        ],
        out_specs=pl.BlockSpec(memory_space=pltpu.VMEM),
        out_shape=jax.ShapeDtypeStruct((N_Q_HEADS, BLOCK, 2 * BLOCK), F32),
        name="band_bias",
    )(bucket, inwin, rel_bias.astype(F32))


def _mem_attention(qm, mkt_ref, mv_ref):
    logits = [_dot(qm, mkt_ref[h]) for h in range(N_MEM_HEADS)]
    yield
    lane = lax.broadcasted_iota(jnp.int32, (qm.shape[0], LANES), 1)
    halves = [None, None]
    for h in range(N_MEM_HEADS):
        m = jnp.max(logits[h], axis=-1, keepdims=True)
        p = jnp.exp(logits[h] - m).astype(BF16)
        o = _dot(p, mv_ref[h])
        grp, pos = h // 2, h % 2
        val, s = o[:, grp * LANES:(grp + 1) * LANES], o[:, (1 - grp) * LANES:(2 - grp) * LANES]
        live = (lane >= pos * HEAD_DIM) & (lane < (pos + 1) * HEAD_DIM)
        part = jnp.where(live, val / jnp.where(live, s, 1.0), 0.0)
        halves[grp] = part if halves[grp] is None else halves[grp] + part
        yield
    return jnp.concatenate(halves, axis=-1)


def _short_conv(u, b_gate, c_gate, w, vs_ref, rows):
    v = c_gate * u
    base = SUBLANES
    vs_ref[base:base + rows, :] = v
    conv = (w[0:1] * vs_ref[base - 2:base - 2 + rows, :]
            + w[1:2] * vs_ref[base - 1:base - 1 + rows, :]
            + w[2:3] * v)
    return b_gate * conv


def _swa(q, kt_prev_ref, kt_cur_ref, v_prev_ref, v_cur_ref, bias_ref, sinks_ref, layer, first):
    col = lax.broadcasted_iota(jnp.int32, (BLOCK, 2 * BLOCK), 1)
    kill_prev = jnp.logical_and(first, col < BLOCK)
    zeros = jnp.zeros((HEAD_DIM, 2 * BLOCK), BF16)
    ones = jnp.ones((2 * BLOCK, LANES), BF16)

    def scores(n, hk):
        pair = slice((hk // 2) * LANES, (hk // 2 + 1) * LANES)
        if n == 0:
            kt = jnp.concatenate([kt_prev_ref[hk * HEAD_DIM:(hk + 1) * HEAD_DIM, :],
                                  kt_cur_ref[hk * HEAD_DIM:(hk + 1) * HEAD_DIM, 0:BLOCK]], axis=1)
            v2 = jnp.concatenate([v_prev_ref[:, pair], v_cur_ref[0:BLOCK, pair]], axis=0)
        else:
            keys = slice((n - 1) * BLOCK, (n + 1) * BLOCK)
            kt = kt_cur_ref[hk * HEAD_DIM:(hk + 1) * HEAD_DIM, keys]
            v2 = v_cur_ref[keys, pair]
        k_lo = jnp.concatenate([kt, zeros], axis=0)
        k_hi = jnp.concatenate([zeros, kt], axis=0)
        logits = []
        for g in range(GROUP):
            head = hk * GROUP + g
            grp, pos = head // 2, head % 2
            qg = q[n * BLOCK:(n + 1) * BLOCK, grp * LANES:(grp + 1) * LANES]
            bias = bias_ref[head]
            if n == 0:
                bias = jnp.where(kill_prev, NEG, bias)
            logits.append(_dot(qg, k_hi if pos else k_lo) + bias)
        return logits, v2

    def attend(hk, logits, v2):
        ps, sink_terms = [], []
        for g in range(GROUP):
            sink = sinks_ref[layer, hk * GROUP + g]
            m = jnp.maximum(jnp.max(logits[g], axis=-1, keepdims=True), sink)
            ps.append(jnp.exp(logits[g] - m).astype(BF16))
            sink_terms.append(jnp.exp(sink - m))
        o3 = _dot(jnp.concatenate(ps, axis=0), jnp.concatenate([v2, ones], axis=1))
        half = slice((hk % 2) * HEAD_DIM, (hk % 2 + 1) * HEAD_DIM)
        sums = slice(LANES + half.start, LANES + half.stop)
        outs = []
        for g in range(GROUP):
            rows = slice(g * BLOCK, (g + 1) * BLOCK)
            outs.append(o3[rows, half] / (o3[rows, sums] + sink_terms[g]))
        return outs

    units = [(i, hk) for i in range(q.shape[0] // BLOCK) for hk in range(N_KV_HEADS)]
    outs = {i: [] for i, _ in units}
    nxt = scores(*units[0])
    for k, (i, hk) in enumerate(units):
        cur = nxt
        if k + 1 < len(units):
            nxt = scores(*units[k + 1])
        yield
        outs[i] += attend(hk, *cur)
    return jnp.concatenate([jnp.concatenate(outs[i], axis=-1) for i in sorted(outs)], axis=0)


def _ffn(x_ref, h_ref, xf_ref, wg_ref, wu_ref, wd_ref):
    h = h_ref[...]
    acts = []
    for c in range(D_FF // MXU_DIM):
        sl = slice(c * MXU_DIM, (c + 1) * MXU_DIM)
        gate = _dot(h, wg_ref[:, sl])
        yield
        up = _dot(h, wu_ref[:, sl])
        acts.append((gate * jax.nn.sigmoid(gate) * up).astype(BF16))
        if c < D_MODEL // MXU_DIM:
            xf_ref[:, sl] = x_ref[:, sl]
        yield
    yield "tail"
    act = jnp.concatenate(acts, axis=-1)
    for n in range(D_MODEL // MXU_DIM):
        cs = slice(n * MXU_DIM, (n + 1) * MXU_DIM)
        yield cs, xf_ref[:, cs] + _dot(act, wd_ref[:, cs])


def _advance(gen):
    try:
        return next(gen)
    except StopIteration:
        return "end"


def _run(gen):
    while _advance(gen) != "end":
        pass


def _interleave(ffn, n_ffn, mix, n_mix, ffn_blocks_first):
    advance = _advance
    f_state = m_state = None
    done = 0
    for i in range(n_ffn):
        if f_state is None:
            f_state = advance(ffn)
        while m_state is None and done * n_ffn < (i + 1) * n_mix:
            m_state = advance(mix)
            done += 1
    while f_state is None:
        f_state = advance(ffn)
    while m_state is None:
        m_state = advance(mix)
    assert (f_state, m_state) == ("tail", "tail")
    for _ in range(ffn_blocks_first):
        advance(ffn)
    assert advance(mix) == "end"
    _run(ffn)


def _layer_kernel(*refs, mixer, kv, epilogue, layer, tm, tiles_per_batch, n_cast):
    it = iter(refs)
    x_ref, gmix_ref, win_ref, mkt_ref, mv_ref, wout_ref = (next(it) for _ in range(6))
    gffn_ref, wg_hbm, wu_hbm, wd_hbm = (next(it) for _ in range(4))
    if mixer == "conv":
        convw_ref = next(it)
    else:
        if kv == "compute":
            gkv_ref, wkt_ref, wv_ref = next(it), next(it), next(it)
        else:
            kt_in_ref, v_in_ref = next(it), next(it)
        bias_ref, sinks_ref = next(it), next(it)
    if epilogue == "final":
        gfin_ref = next(it)
    cast_in = [next(it) for _ in range(n_cast)]
    out_ref = next(it)
    if kv == "compute":
        kt_out_ref, v_out_ref = next(it), next(it)
    cast_out = [next(it) for _ in range(n_cast)]
    x1_ref, h2_ref, xf_ref = next(it), next(it), next(it)
    wg_ref, wu_ref, wd_ref, w_sem = (next(it) for _ in range(4))
    if mixer == "conv":
        vs_ref = next(it)
    else:
        kt_prev_ref, v_prev_ref = next(it), next(it)
        kt_cur_ref, v_cur_ref = (kt_out_ref, v_out_ref) if kv == "compute" else (kt_in_ref, v_in_ref)

    t = pl.program_id(0)
    n_tiles = pl.num_programs(0) - 1
    first = (t % tiles_per_batch) == 0

    ffn_weight_loads = [pltpu.make_async_copy(src, dst, w_sem.at[i]) for i, (src, dst) in
                        enumerate(((wg_hbm, wg_ref), (wu_hbm, wu_ref), (wd_hbm, wd_ref)))]

    @pl.when(t == 0)
    def _():
        for load in ffn_weight_loads:
            load.start()

    @pl.when(t == 1)
    def _():
        for load in ffn_weight_loads:
            load.wait()

    @pl.when(first)
    def _():
        if mixer == "conv":
            vs_ref[0:SUBLANES, :] = jnp.zeros((SUBLANES, CONV_WIDTH), F32)
        else:
            kt_prev_ref[...] = jnp.zeros((KV_WIDTH, BLOCK), BF16)
            v_prev_ref[...] = jnp.zeros((BLOCK, KV_WIDTH), BF16)

    def mix():
        x = x_ref[...]
        h = _rms(x, gmix_ref[...]).astype(BF16)
        yield
        if kv == "compute":
            hkv = _rms(x, gkv_ref[...]).astype(BF16)
            kt_out_ref[...] = (_dot_nt(wkt_ref[...], hkv) * SCALE).astype(BF16)
            yield
            v_out_ref[...] = _dot(hkv, wv_ref[...]).astype(BF16)
            yield
        widths = (CONV_WIDTH,) * 3 + (MEM_WIDTH,) if mixer == "conv" else (Q_WIDTH, MEM_WIDTH)
        fields, lo = [], 0
        for w in widths:
            fields.append(_dot(h, win_ref[:, lo:lo + w]))
            lo += w
            yield
        for src, dst in zip(cast_in, cast_out):
            dst[...] = src[...].astype(BF16)
        if mixer == "conv":
            u, b_gate, c_gate, qm = fields
            y_tok = _short_conv(u, b_gate, c_gate, convw_ref[...], vs_ref, tm)
            yield
        else:
            q, qm = fields
            y_tok = yield from _swa(q.astype(BF16), kt_prev_ref, kt_cur_ref, v_prev_ref, v_cur_ref,
                                    bias_ref, sinks_ref, layer, first)
        y_mem = yield from _mem_attention(qm.astype(BF16), mkt_ref, mv_ref)
        y = jnp.concatenate([y_tok, y_mem], axis=-1).astype(BF16)
        half = D_MODEL // 2
        lo_half = x[:, :half] + _dot(y, wout_ref[:, :half])
        yield
        hi_half = x[:, half:] + _dot(y, wout_ref[:, half:])
        yield "tail"
        x = jnp.concatenate([lo_half, hi_half], axis=-1)
        x1_ref[...] = x
        h2_ref[...] = _rms(x, gffn_ref[...]).astype(BF16)
        if mixer == "conv":
            vs_ref[0:SUBLANES, :] = vs_ref[tm:tm + SUBLANES, :]
        else:
            kt_prev_ref[...] = kt_cur_ref[:, tm - BLOCK:tm]
            v_prev_ref[...] = v_cur_ref[tm - BLOCK:tm, :]

    def ffn():
        blocks = []
        for item in _ffn(x1_ref, h2_ref, xf_ref, wg_ref, wu_ref, wd_ref):
            if isinstance(item, tuple):
                cs, block = item
                if epilogue == "none":
                    out_ref[:, cs] = block
                blocks.append(block)
                yield
            else:
                yield item
        if epilogue == "final":
            out_ref[...] = _rms(jnp.concatenate(blocks, axis=-1), gfin_ref[...])

    @pl.when(t == 0)
    def _():
        _run(mix())

    @pl.when(jnp.logical_and(t > 0, t < n_tiles))
    def _():
        n_mix = {"conv": 13, "swa": 10 + (tm // BLOCK) * N_KV_HEADS}[mixer]
        n_mix += 2 if kv == "compute" else 0
        _interleave(ffn(), 2 * (D_FF // MXU_DIM), mix(), n_mix, ffn_blocks_first=1)

    @pl.when(t == n_tiles)
    def _():
        _run(ffn())


def _const_spec(shape, index):
    return pl.BlockSpec(shape, lambda t: index, pipeline_mode=pl.Buffered(1))


def _cast_rows(n_rows, n_steps):
    tile = 2 * SUBLANES
    return min(r for r in range(tile, n_rows + 1, tile)
               if n_rows % r == 0 and n_rows // r <= n_steps)


def _layer(x, *, seq, mixer, kv, epilogue, layer, sub, w, cast_next, p):
    tokens, _ = x.shape
    tm = TOKEN_TILE
    assert seq % tm == 0 and tm % BLOCK == 0
    tpb = seq // tm
    n_tiles = tokens // tm
    proj_w = A_PROJ if mixer == "conv" else B_PROJ

    def mix_tile(t):
        return jnp.minimum(t, n_tiles - 1)

    def ffn_tile(t):
        return jnp.maximum(t - 1, 0)

    def x_in_map(t):
        return mix_tile(t), 0

    def x_out_map(t):
        return ffn_tile(t), 0

    def mem_map(t):
        return layer, mix_tile(t) // tpb, 0, 0, 0

    def kt_map(t):
        return 0, mix_tile(t)

    def v_map(t):
        return mix_tile(t), 0

    mem_spec = pl.BlockSpec((None, None, N_MEM_HEADS, MEM_WIDTH, N_MEM), mem_map)
    in_specs = [
        pl.BlockSpec((tm, D_MODEL), x_in_map),
        _const_spec((None, 1, D_MODEL), (layer, 0, 0)),
        _const_spec((D_MODEL, proj_w), (0, 0)),
        mem_spec, mem_spec,
        _const_spec((D_MODEL, D_MODEL), (0, 0)),
        _const_spec((None, 1, D_MODEL), (layer, 0, 0)),
        pl.BlockSpec(memory_space=pl.ANY),
        pl.BlockSpec(memory_space=pl.ANY),
        pl.BlockSpec(memory_space=pl.ANY),
    ]
    args = [x, p["norm_mix"], w["in"], p["mkt"], p["mv"], w["out"],
            p["norm_ffn"], w["gate"], w["up"], w["down"]]
    scratch = [pltpu.VMEM((tm, D_MODEL), F32), pltpu.VMEM((tm, D_MODEL), BF16),
               pltpu.VMEM((tm, D_MODEL), F32),
               pltpu.VMEM((D_MODEL, D_FF), BF16), pltpu.VMEM((D_MODEL, D_FF), BF16),
               pltpu.VMEM((D_FF, D_MODEL), BF16), pltpu.SemaphoreType.DMA((3,))]
    if mixer == "conv":
        in_specs.append(_const_spec((None, CONV_K, CONV_WIDTH), (sub, 0, 0)))
        args.append(p["conv_w"])
        scratch.append(pltpu.VMEM((tm + SUBLANES, CONV_WIDTH), F32))
    else:
        if kv == "compute":
            in_specs += [_const_spec((1, D_MODEL), (0, 0)),
                         _const_spec((KV_WIDTH, D_MODEL), (0, 0)),
                         _const_spec((D_MODEL, KV_WIDTH), (0, 0))]
            args += [p["kv_norm"], p["w_kt"], p["w_v"]]
        else:
            in_specs += [pl.BlockSpec((KV_WIDTH, tm), kt_map), pl.BlockSpec((tm, KV_WIDTH), v_map)]
            args += [p["kt"], p["v"]]
        in_specs += [_const_spec((N_Q_HEADS, BLOCK, 2 * BLOCK), (0, 0, 0)),
                     pl.BlockSpec(memory_space=pltpu.SMEM)]
        args += [p["bias"], p["sinks"]]
        scratch += [pltpu.VMEM((KV_WIDTH, BLOCK), BF16), pltpu.VMEM((BLOCK, KV_WIDTH), BF16)]
    if epilogue == "final":
        in_specs.append(_const_spec((1, D_MODEL), (0, 0)))
        args.append(p["final_norm"])

    out_specs = [pl.BlockSpec((tm, D_MODEL), x_out_map)]
    out_shape = [jax.ShapeDtypeStruct(x.shape, F32)]
    if kv == "compute":
        out_specs += [pl.BlockSpec((KV_WIDTH, tm), kt_map), pl.BlockSpec((tm, KV_WIDTH), v_map)]
        out_shape += [jax.ShapeDtypeStruct((KV_WIDTH, tokens), BF16),
                      jax.ShapeDtypeStruct((tokens, KV_WIDTH), BF16)]
    for stack, idx in cast_next:
        _, n_rows, n_cols = stack.shape
        rows = _cast_rows(n_rows, n_tiles)
        last = n_rows // rows - 1
        in_specs.append(pl.BlockSpec((None, rows, n_cols),
                                     lambda t, idx=idx, last=last: (idx, jnp.minimum(t, last), 0)))
        args.append(stack)
        out_specs.append(pl.BlockSpec((rows, n_cols),
                                      lambda t, last=last: (jnp.minimum(t, last), 0)))
        out_shape.append(jax.ShapeDtypeStruct((n_rows, n_cols), BF16))

    outs = pl.pallas_call(
        functools.partial(_layer_kernel, mixer=mixer, kv=kv, epilogue=epilogue, layer=sub, tm=tm,
                          tiles_per_batch=tpb, n_cast=len(cast_next)),
        grid=(n_tiles + 1,),
        in_specs=in_specs,
        out_specs=out_specs,
        out_shape=out_shape,
        scratch_shapes=scratch,
        compiler_params=pltpu.CompilerParams(
            dimension_semantics=("arbitrary",),
            vmem_limit_bytes=VMEM_LIMIT_BYTES),
        name=f"layer{layer}_{mixer}",
    )(*args)
    return outs


def kernel(x, mem, norm_mix, norm_ffn, a_w_in, a_conv_w, a_w_out, kv_norm, w_kv, b_w_q, b_sinks,
           b_w_out, rel_bias, mem_norm, w_mem_kv, w_gate, w_up, w_down, final_norm):
    mkt, mv = _mem_kv(mem, mem_norm, w_mem_kv)
    p = {
        "norm_mix": norm_mix.reshape(DEPTH, 1, D_MODEL),
        "norm_ffn": norm_ffn.reshape(DEPTH, 1, D_MODEL),
        "conv_w": a_conv_w,
        "mkt": mkt, "mv": mv,
        "kv_norm": kv_norm.reshape(1, D_MODEL),
        "w_kt": w_kv[:, :KV_WIDTH].T.astype(BF16),
        "w_v": w_kv[:, KV_WIDTH:].astype(BF16),
        "final_norm": final_norm.reshape(1, D_MODEL),
        "sinks": b_sinks.astype(F32),
        "bias": _band_bias(rel_bias),
    }

    def f32_weights(i):
        stack_in, stack_out, sub = (a_w_in, a_w_out, i) if i < N_A else (b_w_q, b_w_out, i - N_A)
        return {"in": (stack_in, sub), "out": (stack_out, sub),
                "gate": (w_gate, i), "up": (w_up, i), "down": (w_down, i)}

    w = {k: stack[idx].astype(BF16) for k, (stack, idx) in f32_weights(0).items()}
    batch, seq, _ = x.shape
    x = x.reshape(batch * seq, D_MODEL)
    for i in range(DEPTH):
        mixer = "conv" if i < N_A else "swa"
        sub = i if i < N_A else i - N_A
        kv = None if i < N_A else ("compute" if i == N_A else "load")
        epilogue = "final" if i == DEPTH - 1 else "none"
        nxt = f32_weights(i + 1) if i + 1 < DEPTH else {}
        outs = _layer(x, seq=seq, mixer=mixer, kv=kv, epilogue=epilogue, layer=i, sub=sub, w=w,
                      cast_next=list(nxt.values()), p=p)
        x = outs[0]
        if kv == "compute":
            p["kt"], p["v"] = outs[1], outs[2]
        w = dict(zip(nxt.keys(), outs[len(outs) - len(nxt):]))
    return x.reshape(batch, seq, D_MODEL)
```

```python
import functools
import math

import jax
import jax.numpy as jnp
from jax import lax
from jax.experimental import pallas as pl
from jax.experimental.pallas import tpu as pltpu

D_MODEL = 1024
DEPTH = 4
N_MEM = 256
HEAD_DIM = 64
N_MEM_HEADS = 4
MEM_WIDTH = N_MEM_HEADS * HEAD_DIM
CONV_WIDTH = D_MODEL - MEM_WIDTH
CONV_K = 3
N_Q_HEADS = CONV_WIDTH // HEAD_DIM
N_KV_HEADS = 4
GROUP = N_Q_HEADS // N_KV_HEADS
Q_WIDTH = N_Q_HEADS * HEAD_DIM
KV_WIDTH = N_KV_HEADS * HEAD_DIM
A_PROJ = 3 * CONV_WIDTH + MEM_WIDTH
B_PROJ = Q_WIDTH + MEM_WIDTH
WINDOW = 128
BLOCK = 128
REL_BUCKETS = 32
REL_MAX_DIST = 128
D_FF = ((8 * D_MODEL + 3 * 256 - 1) // (3 * 256)) * 256
N_A = DEPTH // 2
N_B = DEPTH - N_A
EPS = 1e-5

SCALE = HEAD_DIM ** -0.5
NEG = -1e30
LANES = 128
SUBLANES = 8
MXU_DIM = 256
TOKEN_TILE = 512
V7X_VMEM_BYTES = 64 * 1024 * 1024
VMEM_LIMIT_BYTES = V7X_VMEM_BYTES - V7X_VMEM_BYTES // 8

BF16 = jnp.bfloat16
F32 = jnp.float32


def _dot(a, b):
    return jnp.dot(a, b, preferred_element_type=F32)


def _dot_nt(a, b):
    return lax.dot_general(a, b, (((1,), (1,)), ((), ())), preferred_element_type=F32)


def _rms(x, g):
    ms = jnp.mean(x * x, axis=-1, keepdims=True)
    return x * lax.rsqrt(ms + EPS) * g


def _mem_kv_kernel(mem_ref, g_ref, w_ref, kt_ref, v_ref, *, batch):
    mem_n = _rms(mem_ref[...], g_ref[...]).astype(BF16)
    kv = _dot(mem_n, w_ref[...].astype(BF16))
    row = lax.broadcasted_iota(jnp.int32, (MEM_WIDTH, N_MEM), 0)
    col = lax.broadcasted_iota(jnp.int32, (N_MEM, MEM_WIDTH), 1)
    for b in range(batch):
        kvb = kv[b * N_MEM:(b + 1) * N_MEM]
        kt = (kvb[:, :MEM_WIDTH] * SCALE).T
        v = kvb[:, MEM_WIDTH:]
        for h in range(N_MEM_HEADS):
            lo, hi = h * HEAD_DIM, (h + 1) * HEAD_DIM
            kt_ref[b, h] = jnp.where((row >= lo) & (row < hi), kt, 0.0).astype(BF16)
            ones_lo = (lo + LANES) % MEM_WIDTH
            fill = jnp.where((col >= ones_lo) & (col < ones_lo + HEAD_DIM), 1.0, 0.0)
            v_ref[b, h] = jnp.where((col >= lo) & (col < hi), v, fill).astype(BF16)


def _mem_kv(mem, mem_norm, w_mem_kv):
    batch = mem.shape[0]
    out_sds = jax.ShapeDtypeStruct((DEPTH, batch, N_MEM_HEADS, MEM_WIDTH, N_MEM), BF16)
    out_spec = pl.BlockSpec((None, batch, N_MEM_HEADS, MEM_WIDTH, N_MEM),
                            lambda i: (i, 0, 0, 0, 0))
    return pl.pallas_call(
        functools.partial(_mem_kv_kernel, batch=batch),
        grid=(DEPTH,),
        in_specs=[
            pl.BlockSpec((batch * N_MEM, D_MODEL), lambda i: (0, 0)),
            pl.BlockSpec((1, D_MODEL), lambda i: (0, 0)),
            pl.BlockSpec((None, D_MODEL, 2 * MEM_WIDTH), lambda i: (i, 0, 0)),
        ],
        out_specs=[out_spec, out_spec],
        out_shape=[out_sds, out_sds],
        compiler_params=pltpu.CompilerParams(dimension_semantics=("arbitrary",)),
        name="mem_kv",
    )(mem.reshape(batch * N_MEM, D_MODEL), mem_norm.reshape(1, D_MODEL), w_mem_kv)


def _bias_kernel(bucket_ref, inwin_ref, table_ref, out_ref):
    bucket = bucket_ref[...]
    inwin = inwin_ref[...] > 0
    for h in range(N_Q_HEADS):
        acc = jnp.zeros((BLOCK, 2 * BLOCK), F32)
        for b in range(REL_BUCKETS):
            acc = jnp.where(bucket == b, table_ref[b, h], acc)
        out_ref[h] = jnp.where(inwin, acc, NEG)


def _rel_bucket(dist):
    max_exact = REL_BUCKETS // 2
    d = jnp.maximum(dist, 1).astype(F32)
    large = max_exact + (jnp.log(d / max_exact) / math.log(REL_MAX_DIST / max_exact)
                         * (REL_BUCKETS - max_exact)).astype(jnp.int32)
    large = jnp.minimum(large, REL_BUCKETS - 1)
    return jnp.where(dist < max_exact, dist, large)


def _band_bias(rel_bias):
    qi = jnp.arange(BLOCK, dtype=jnp.int32)[:, None]
    kj = jnp.arange(2 * BLOCK, dtype=jnp.int32)[None, :]
    dist = qi + BLOCK - kj
    inwin = ((dist >= 0) & (dist < WINDOW)).astype(jnp.int32)
    bucket = _rel_bucket(jnp.maximum(dist, 0)).astype(jnp.int32)
    return pl.pallas_call(
        _bias_kernel,
        in_specs=[
            pl.BlockSpec(memory_space=pltpu.VMEM),
            pl.BlockSpec(memory_space=pltpu.VMEM),
            pl.BlockSpec(memory_space=pltpu.SMEM),
        ],
        out_specs=pl.BlockSpec(memory_space=pltpu.VMEM),
        out_shape=jax.ShapeDtypeStruct((N_Q_HEADS, BLOCK, 2 * BLOCK), F32),
        name="band_bias",
    )(bucket, inwin, rel_bias)


def _mem_attention(qm, mkt_ref, mv_ref):
    logits = [_dot(qm, mkt_ref[h]) for h in range(N_MEM_HEADS)]
    yield
    lane = lax.broadcasted_iota(jnp.int32, (qm.shape[0], LANES), 1)
    halves = [None, None]
    for h in range(N_MEM_HEADS):
        m = jnp.max(logits[h], axis=-1, keepdims=True)
        p = jnp.exp(logits[h] - m).astype(BF16)
        o = _dot(p, mv_ref[h])
        grp, pos = h // 2, h % 2
        val, s = o[:, grp * LANES:(grp + 1) * LANES], o[:, (1 - grp) * LANES:(2 - grp) * LANES]
        live = (lane >= pos * HEAD_DIM) & (lane < (pos + 1) * HEAD_DIM)
        part = jnp.where(live, val / jnp.where(live, s, 1.0), 0.0)
        halves[grp] = part if halves[grp] is None else halves[grp] + part
        yield
    return jnp.concatenate(halves, axis=-1)


def _short_conv(u, b_gate, c_gate, w, vs_ref, rows):
    v = c_gate * u
    base = SUBLANES
    vs_ref[base:base + rows, :] = v
    conv = (w[0:1] * vs_ref[base - 2:base - 2 + rows, :]
            + w[1:2] * vs_ref[base - 1:base - 1 + rows, :]
            + w[2:3] * v)
    return b_gate * conv


def _swa(q, kt_prev_ref, kt_cur_ref, v_prev_ref, v_cur_ref, bias_ref, sinks_ref, layer, first):
    col = lax.broadcasted_iota(jnp.int32, (BLOCK, 2 * BLOCK), 1)
    kill_prev = jnp.logical_and(first, col < BLOCK)
    zeros = jnp.zeros((HEAD_DIM, 2 * BLOCK), BF16)
    ones = jnp.ones((2 * BLOCK, LANES), BF16)

    def scores(n, hk):
        pair = slice((hk // 2) * LANES, (hk // 2 + 1) * LANES)
        if n == 0:
            kt = jnp.concatenate([kt_prev_ref[hk * HEAD_DIM:(hk + 1) * HEAD_DIM, :],
                                  kt_cur_ref[hk * HEAD_DIM:(hk + 1) * HEAD_DIM, 0:BLOCK]], axis=1)
            v2 = jnp.concatenate([v_prev_ref[:, pair], v_cur_ref[0:BLOCK, pair]], axis=0)
        else:
            keys = slice((n - 1) * BLOCK, (n + 1) * BLOCK)
            kt = kt_cur_ref[hk * HEAD_DIM:(hk + 1) * HEAD_DIM, keys]
            v2 = v_cur_ref[keys, pair]
        k_lo = jnp.concatenate([kt, zeros], axis=0)
        k_hi = jnp.concatenate([zeros, kt], axis=0)
        logits = []
        for g in range(GROUP):
            head = hk * GROUP + g
            grp, pos = head // 2, head % 2
            qg = q[n * BLOCK:(n + 1) * BLOCK, grp * LANES:(grp + 1) * LANES]
            bias = bias_ref[head]
            if n == 0:
                bias = jnp.where(kill_prev, NEG, bias)
            logits.append(_dot(qg, k_hi if pos else k_lo) + bias)
        return logits, v2

    def attend(hk, logits, v2):
        ps, sink_terms = [], []
        for g in range(GROUP):
            sink = sinks_ref[layer, hk * GROUP + g]
            m = jnp.maximum(jnp.max(logits[g], axis=-1, keepdims=True), sink)
            ps.append(jnp.exp(logits[g] - m).astype(BF16))
            sink_terms.append(jnp.exp(sink - m))
        o3 = _dot(jnp.concatenate(ps, axis=0), jnp.concatenate([v2, ones], axis=1))
        half = slice((hk % 2) * HEAD_DIM, (hk % 2 + 1) * HEAD_DIM)
        sums = slice(LANES + half.start, LANES + half.stop)
        outs = []
        for g in range(GROUP):
            rows = slice(g * BLOCK, (g + 1) * BLOCK)
            outs.append(o3[rows, half] / (o3[rows, sums] + sink_terms[g]))
        return outs

    units = [(i, hk) for i in range(q.shape[0] // BLOCK) for hk in range(N_KV_HEADS)]
    outs = {i: [] for i, _ in units}
    nxt = scores(*units[0])
    for k, (i, hk) in enumerate(units):
        cur = nxt
        if k + 1 < len(units):
            nxt = scores(*units[k + 1])
        yield
        outs[i] += attend(hk, *cur)
    return jnp.concatenate([jnp.concatenate(outs[i], axis=-1) for i in sorted(outs)], axis=0)


def _ffn(x_ref, h_ref, xf_ref, wg_ref, wu_ref, wd_ref):
    h = h_ref[...]
    acts = []
    for c in range(D_FF // MXU_DIM):
        sl = slice(c * MXU_DIM, (c + 1) * MXU_DIM)
        gate = _dot(h, wg_ref[:, sl])
        yield
        up = _dot(h, wu_ref[:, sl])
        acts.append((gate * jax.nn.sigmoid(gate) * up).astype(BF16))
        if c < D_MODEL // MXU_DIM:
            xf_ref[:, sl] = x_ref[:, sl]
        yield
    yield "tail"
    act = jnp.concatenate(acts, axis=-1)
    for n in range(D_MODEL // MXU_DIM):
        cs = slice(n * MXU_DIM, (n + 1) * MXU_DIM)
        yield cs, xf_ref[:, cs] + _dot(act, wd_ref[:, cs])


def _advance(gen):
    try:
        return next(gen)
    except StopIteration:
        return "end"


def _run(gen):
    while _advance(gen) != "end":
        pass


def _interleave(ffn, n_ffn, mix, n_mix, ffn_blocks_first):
    advance = _advance
    f_state = m_state = None
    done = 0
    for i in range(n_ffn):
        if f_state is None:
            f_state = advance(ffn)
        while m_state is None and done * n_ffn < (i + 1) * n_mix:
            m_state = advance(mix)
            done += 1
    while f_state is None:
        f_state = advance(ffn)
    while m_state is None:
        m_state = advance(mix)
    assert (f_state, m_state) == ("tail", "tail")
    for _ in range(ffn_blocks_first):
        advance(ffn)
    assert advance(mix) == "end"
    _run(ffn)


def _layer_kernel(*refs, mixer, kv, epilogue, layer, tm, tiles_per_batch, n_cast):
    it = iter(refs)
    x_ref, gmix_ref, win_ref, mkt_ref, mv_ref, wout_ref = (next(it) for _ in range(6))
    gffn_ref, wg_hbm, wu_hbm, wd_hbm = (next(it) for _ in range(4))
    if mixer == "conv":
        convw_ref = next(it)
    else:
        if kv == "compute":
            gkv_ref, wkt_ref, wv_ref = next(it), next(it), next(it)
        else:
            kt_in_ref, v_in_ref = next(it), next(it)
        bias_ref, sinks_ref = next(it), next(it)
    if epilogue == "final":
        gfin_ref = next(it)
    cast_in = [next(it) for _ in range(n_cast)]
    out_ref = next(it)
    if kv == "compute":
        kt_out_ref, v_out_ref = next(it), next(it)
    cast_out = [next(it) for _ in range(n_cast)]
    x1_ref, h2_ref, xf_ref = next(it), next(it), next(it)
    wg_ref, wu_ref, wd_ref, w_sem = (next(it) for _ in range(4))
    if mixer == "conv":
        vs_ref = next(it)
    else:
        kt_prev_ref, v_prev_ref = next(it), next(it)
        kt_cur_ref, v_cur_ref = (kt_out_ref, v_out_ref) if kv == "compute" else (kt_in_ref, v_in_ref)

    t = pl.program_id(0)
    n_tiles = pl.num_programs(0) - 1
    first = (t % tiles_per_batch) == 0

    ffn_weight_loads = [pltpu.make_async_copy(src, dst, w_sem.at[i]) for i, (src, dst) in
                        enumerate(((wg_hbm, wg_ref), (wu_hbm, wu_ref), (wd_hbm, wd_ref)))]

    @pl.when(first)
    def _():
        if mixer == "conv":
            vs_ref[0:SUBLANES, :] = jnp.zeros((SUBLANES, CONV_WIDTH), F32)
        else:
            kt_prev_ref[...] = jnp.zeros((KV_WIDTH, BLOCK), BF16)
            v_prev_ref[...] = jnp.zeros((BLOCK, KV_WIDTH), BF16)

    def mix():
        x = x_ref[...]
        h = _rms(x, gmix_ref[...]).astype(BF16)
        yield
        if kv == "compute":
            hkv = _rms(x, gkv_ref[...]).astype(BF16)
            kt_out_ref[...] = (_dot_nt(wkt_ref[...], hkv) * SCALE).astype(BF16)
            yield
            v_out_ref[...] = _dot(hkv, wv_ref[...]).astype(BF16)
            yield
        widths = (CONV_WIDTH,) * 3 + (MEM_WIDTH,) if mixer == "conv" else (Q_WIDTH, MEM_WIDTH)
        fields, lo = [], 0
        for w in widths:
            fields.append(_dot(h, win_ref[:, lo:lo + w]))
            lo += w
            yield
        for src, dst in zip(cast_in, cast_out):
            dst[...] = src[...].astype(BF16)
        if mixer == "conv":
            u, b_gate, c_gate, qm = fields
            y_tok = _short_conv(u, b_gate, c_gate, convw_ref[...], vs_ref, tm)
            yield
        else:
            q, qm = fields
            y_tok = yield from _swa(q.astype(BF16), kt_prev_ref, kt_cur_ref, v_prev_ref, v_cur_ref,
                                    bias_ref, sinks_ref, layer, first)
        y_mem = yield from _mem_attention(qm.astype(BF16), mkt_ref, mv_ref)
        y = jnp.concatenate([y_tok, y_mem], axis=-1).astype(BF16)
        half = D_MODEL // 2
        lo_half = x[:, :half] + _dot(y, wout_ref[:, :half])
        yield
        hi_half = x[:, half:] + _dot(y, wout_ref[:, half:])
        yield "tail"
        x = jnp.concatenate([lo_half, hi_half], axis=-1)
        x1_ref[...] = x
        h2_ref[...] = _rms(x, gffn_ref[...]).astype(BF16)
        if mixer == "conv":
            vs_ref[0:SUBLANES, :] = vs_ref[tm:tm + SUBLANES, :]
        else:
            kt_prev_ref[...] = kt_cur_ref[:, tm - BLOCK:tm]
            v_prev_ref[...] = v_cur_ref[tm - BLOCK:tm, :]

    def ffn():
        blocks = []
        for item in _ffn(x1_ref, h2_ref, xf_ref, wg_ref, wu_ref, wd_ref):
            if isinstance(item, tuple):
                cs, block = item
                if epilogue == "none":
                    out_ref[:, cs] = block
                blocks.append(block)
                yield
            else:
                yield item
        if epilogue == "final":
            out_ref[...] = _rms(jnp.concatenate(blocks, axis=-1), gfin_ref[...])

    @pl.when(t == 0)
    def _():
        for load in ffn_weight_loads:
            load.start()
        _run(mix())
        for load in ffn_weight_loads:
            load.wait()

    @pl.when(jnp.logical_and(t > 0, t < n_tiles))
    def _():
        n_mix = {"conv": 13, "swa": 10 + (tm // BLOCK) * N_KV_HEADS}[mixer]
        n_mix += 2 if kv == "compute" else 0
        _interleave(ffn(), 2 * (D_FF // MXU_DIM), mix(), n_mix, ffn_blocks_first=1)

    @pl.when(t == n_tiles)
    def _():
        _run(ffn())


def _const_spec(shape, index):
    return pl.BlockSpec(shape, lambda t: index, pipeline_mode=pl.Buffered(1))


def _cast_rows(n_rows, n_steps):
    tile = 2 * SUBLANES
    return min(r for r in range(tile, n_rows + 1, tile)
               if n_rows % r == 0 and n_rows // r <= n_steps)


def _layer(x, *, seq, mixer, kv, epilogue, layer, sub, w, cast_next, p):
    tokens, _ = x.shape
    tm = TOKEN_TILE
    assert seq % tm == 0 and tm % BLOCK == 0
    tpb = seq // tm
    n_tiles = tokens // tm
    proj_w = A_PROJ if mixer == "conv" else B_PROJ

    def mix_tile(t):
        return jnp.minimum(t, n_tiles - 1)

    def ffn_tile(t):
        return jnp.maximum(t - 1, 0)

    def x_in_map(t):
        return mix_tile(t), 0

    def x_out_map(t):
        return ffn_tile(t), 0

    def mem_map(t):
        return layer, mix_tile(t) // tpb, 0, 0, 0

    def kt_map(t):
        return 0, mix_tile(t)

    def v_map(t):
        return mix_tile(t), 0

    mem_spec = pl.BlockSpec((None, None, N_MEM_HEADS, MEM_WIDTH, N_MEM), mem_map)
    in_specs = [
        pl.BlockSpec((tm, D_MODEL), x_in_map),
        _const_spec((None, 1, D_MODEL), (layer, 0, 0)),
        _const_spec((D_MODEL, proj_w), (0, 0)),
        mem_spec, mem_spec,
        _const_spec((D_MODEL, D_MODEL), (0, 0)),
        _const_spec((None, 1, D_MODEL), (layer, 0, 0)),
        pl.BlockSpec(memory_space=pl.ANY),
        pl.BlockSpec(memory_space=pl.ANY),
        pl.BlockSpec(memory_space=pl.ANY),
    ]
    args = [x, p["norm_mix"], w["in"], p["mkt"], p["mv"], w["out"],
            p["norm_ffn"], w["gate"], w["up"], w["down"]]
    scratch = [pltpu.VMEM((tm, D_MODEL), F32), pltpu.VMEM((tm, D_MODEL), BF16),
               pltpu.VMEM((tm, D_MODEL), F32),
               pltpu.VMEM((D_MODEL, D_FF), BF16), pltpu.VMEM((D_MODEL, D_FF), BF16),
               pltpu.VMEM((D_FF, D_MODEL), BF16), pltpu.SemaphoreType.DMA((3,))]
    if mixer == "conv":
        in_specs.append(_const_spec((None, CONV_K, CONV_WIDTH), (sub, 0, 0)))
        args.append(p["conv_w"])
        scratch.append(pltpu.VMEM((tm + SUBLANES, CONV_WIDTH), F32))
    else:
        if kv == "compute":
            in_specs += [_const_spec((1, D_MODEL), (0, 0)),
                         _const_spec((KV_WIDTH, D_MODEL), (0, 0)),
                         _const_spec((D_MODEL, KV_WIDTH), (0, 0))]
            args += [p["kv_norm"], p["w_kt"], p["w_v"]]
        else:
            in_specs += [pl.BlockSpec((KV_WIDTH, tm), kt_map), pl.BlockSpec((tm, KV_WIDTH), v_map)]
            args += [p["kt"], p["v"]]
        in_specs += [_const_spec((N_Q_HEADS, BLOCK, 2 * BLOCK), (0, 0, 0)),
                     pl.BlockSpec(memory_space=pltpu.SMEM)]
        args += [p["bias"], p["sinks"]]
        scratch += [pltpu.VMEM((KV_WIDTH, BLOCK), BF16), pltpu.VMEM((BLOCK, KV_WIDTH), BF16)]
    if epilogue == "final":
        in_specs.append(_const_spec((1, D_MODEL), (0, 0)))
        args.append(p["final_norm"])

    out_specs = [pl.BlockSpec((tm, D_MODEL), x_out_map)]
    out_shape = [jax.ShapeDtypeStruct(x.shape, F32)]
    if kv == "compute":
        out_specs += [pl.BlockSpec((KV_WIDTH, tm), kt_map), pl.BlockSpec((tm, KV_WIDTH), v_map)]
        out_shape += [jax.ShapeDtypeStruct((KV_WIDTH, tokens), BF16),
                      jax.ShapeDtypeStruct((tokens, KV_WIDTH), BF16)]
    for stack, idx in cast_next:
        _, n_rows, n_cols = stack.shape
        rows = _cast_rows(n_rows, n_tiles)
        last = n_rows // rows - 1
        in_specs.append(pl.BlockSpec((None, rows, n_cols),
                                     lambda t, idx=idx, last=last: (idx, jnp.minimum(t, last), 0)))
        args.append(stack)
        out_specs.append(pl.BlockSpec((rows, n_cols),
                                      lambda t, last=last: (jnp.minimum(t, last), 0)))
        out_shape.append(jax.ShapeDtypeStruct((n_rows, n_cols), BF16))

    outs = pl.pallas_call(
        functools.partial(_layer_kernel, mixer=mixer, kv=kv, epilogue=epilogue, layer=sub, tm=tm,
                          tiles_per_batch=tpb, n_cast=len(cast_next)),
        grid=(n_tiles + 1,),
        in_specs=in_specs,
        out_specs=out_specs,
        out_shape=out_shape,
        scratch_shapes=scratch,
        compiler_params=pltpu.CompilerParams(
            dimension_semantics=("arbitrary",),
            vmem_limit_bytes=VMEM_LIMIT_BYTES),
        name=f"layer{layer}_{mixer}",
    )(*args)
    return outs


def kernel(x, mem, norm_mix, norm_ffn, a_w_in, a_conv_w, a_w_out, kv_norm, w_kv, b_w_q, b_sinks,
           b_w_out, rel_bias, mem_norm, w_mem_kv, w_gate, w_up, w_down, final_norm):
    mkt, mv = _mem_kv(mem, mem_norm, w_mem_kv)
    p = {
        "norm_mix": norm_mix.reshape(DEPTH, 1, D_MODEL),
        "norm_ffn": norm_ffn.reshape(DEPTH, 1, D_MODEL),
        "conv_w": a_conv_w,
        "mkt": mkt, "mv": mv,
        "kv_norm": kv_norm.reshape(1, D_MODEL),
        "w_kt": w_kv[:, :KV_WIDTH].T.astype(BF16),
        "w_v": w_kv[:, KV_WIDTH:].astype(BF16),
        "final_norm": final_norm.reshape(1, D_MODEL),
        "sinks": b_sinks,
        "bias": _band_bias(rel_bias),
    }

    def f32_weights(i):
        stack_in, stack_out, sub = (a_w_in, a_w_out, i) if i < N_A else (b_w_q, b_w_out, i - N_A)
        return {"in": (stack_in, sub), "out": (stack_out, sub),
                "gate": (w_gate, i), "up": (w_up, i), "down": (w_down, i)}

    w = {k: stack[idx].astype(BF16) for k, (stack, idx) in f32_weights(0).items()}
    batch, seq, _ = x.shape
    x = x.reshape(batch * seq, D_MODEL)
    for i in range(DEPTH):
        mixer = "conv" if i < N_A else "swa"
        sub = i if i < N_A else i - N_A
        kv = None if i < N_A else ("compute" if i == N_A else "load")
        epilogue = "final" if i == DEPTH - 1 else "none"
        nxt = f32_weights(i + 1) if i + 1 < DEPTH else {}
        outs = _layer(x, seq=seq, mixer=mixer, kv=kv, epilogue=epilogue, layer=i, sub=sub, w=w,
                      cast_next=list(nxt.values()), p=p)
        x = outs[0]
        if kv == "compute":
            p["kt"], p["v"] = outs[1], outs[2]
        w = dict(zip(nxt.keys(), outs[len(outs) - len(nxt):]))
    return x.reshape(batch, seq, D_MODEL)
```

```python
import functools
import math

import jax
import jax.numpy as jnp
from jax import lax
from jax.experimental import pallas as pl
from jax.experimental.pallas import tpu as pltpu

D_MODEL = 1024
DEPTH = 4
N_MEM = 256
HEAD_DIM = 64
N_MEM_HEADS = 4
MEM_WIDTH = N_MEM_HEADS * HEAD_DIM
CONV_WIDTH = D_MODEL - MEM_WIDTH
CONV_K = 3
N_Q_HEADS = CONV_WIDTH // HEAD_DIM
N_KV_HEADS = 4
GROUP = N_Q_HEADS // N_KV_HEADS
Q_WIDTH = N_Q_HEADS * HEAD_DIM
KV_WIDTH = N_KV_HEADS * HEAD_DIM
A_PROJ = 3 * CONV_WIDTH + MEM_WIDTH
B_PROJ = Q_WIDTH + MEM_WIDTH
WINDOW = 128
BLOCK = 128
REL_BUCKETS = 32
REL_MAX_DIST = 128
D_FF = ((8 * D_MODEL + 3 * 256 - 1) // (3 * 256)) * 256
N_A = DEPTH // 2
N_B = DEPTH - N_A
EPS = 1e-5

SCALE = HEAD_DIM ** -0.5
NEG = -1e30
LANES = 128
SUBLANES = 8
MXU_DIM = 256
TOKEN_TILE = 512
V7X_VMEM_BYTES = 64 * 1024 * 1024
VMEM_LIMIT_BYTES = V7X_VMEM_BYTES - V7X_VMEM_BYTES // 8

BF16 = jnp.bfloat16
F32 = jnp.float32


def _dot(a, b):
    return jnp.dot(a, b, preferred_element_type=F32)


def _dot_nt(a, b):
    return lax.dot_general(a, b, (((1,), (1,)), ((), ())), preferred_element_type=F32)


def _rms(x, g):
    ms = jnp.mean(x * x, axis=-1, keepdims=True)
    return x * lax.rsqrt(ms + EPS) * g


def _mem_kv_kernel(mem_ref, g_ref, w_ref, kt_ref, v_ref, *, batch):
    mem_n = _rms(mem_ref[...], g_ref[...]).astype(BF16)
    kv = _dot(mem_n, w_ref[...].astype(BF16))
    row = lax.broadcasted_iota(jnp.int32, (MEM_WIDTH, N_MEM), 0)
    col = lax.broadcasted_iota(jnp.int32, (N_MEM, MEM_WIDTH), 1)
    for b in range(batch):
        kvb = kv[b * N_MEM:(b + 1) * N_MEM]
        kt = (kvb[:, :MEM_WIDTH] * SCALE).T
        v = kvb[:, MEM_WIDTH:]
        for h in range(N_MEM_HEADS):
            lo, hi = h * HEAD_DIM, (h + 1) * HEAD_DIM
            kt_ref[b, h] = jnp.where((row >= lo) & (row < hi), kt, 0.0).astype(BF16)
            ones_lo = (lo + LANES) % MEM_WIDTH
            fill = jnp.where((col >= ones_lo) & (col < ones_lo + HEAD_DIM), 1.0, 0.0)
            v_ref[b, h] = jnp.where((col >= lo) & (col < hi), v, fill).astype(BF16)


def _mem_kv(mem, mem_norm, w_mem_kv):
    batch = mem.shape[0]
    out_sds = jax.ShapeDtypeStruct((DEPTH, batch, N_MEM_HEADS, MEM_WIDTH, N_MEM), BF16)
    out_spec = pl.BlockSpec((None, batch, N_MEM_HEADS, MEM_WIDTH, N_MEM),
                            lambda i: (i, 0, 0, 0, 0))
    return pl.pallas_call(
        functools.partial(_mem_kv_kernel, batch=batch),
        grid=(DEPTH,),
        in_specs=[
            pl.BlockSpec((batch * N_MEM, D_MODEL), lambda i: (0, 0)),
            pl.BlockSpec((1, D_MODEL), lambda i: (0, 0)),
            pl.BlockSpec((None, D_MODEL, 2 * MEM_WIDTH), lambda i: (i, 0, 0)),
        ],
        out_specs=[out_spec, out_spec],
        out_shape=[out_sds, out_sds],
        compiler_params=pltpu.CompilerParams(dimension_semantics=("arbitrary",)),
        name="mem_kv",
    )(mem.reshape(batch * N_MEM, D_MODEL), mem_norm.reshape(1, D_MODEL), w_mem_kv)


def _bias_kernel(bucket_ref, inwin_ref, table_ref, out_ref):
    bucket = bucket_ref[...]
    inwin = inwin_ref[...] > 0
    for h in range(N_Q_HEADS):
        acc = jnp.zeros((BLOCK, 2 * BLOCK), F32)
        for b in range(REL_BUCKETS):
            acc = jnp.where(bucket == b, table_ref[b, h], acc)
        out_ref[h] = jnp.where(inwin, acc, NEG)


def _rel_bucket(dist):
    max_exact = REL_BUCKETS // 2
    d = jnp.maximum(dist, 1).astype(F32)
    large = max_exact + (jnp.log(d / max_exact) / math.log(REL_MAX_DIST / max_exact)
                         * (REL_BUCKETS - max_exact)).astype(jnp.int32)
    large = jnp.minimum(large, REL_BUCKETS - 1)
    return jnp.where(dist < max_exact, dist, large)


def _band_bias(rel_bias):
    qi = jnp.arange(BLOCK, dtype=jnp.int32)[:, None]
    kj = jnp.arange(2 * BLOCK, dtype=jnp.int32)[None, :]
    dist = qi + BLOCK - kj
    inwin = ((dist >= 0) & (dist < WINDOW)).astype(jnp.int32)
    bucket = _rel_bucket(jnp.maximum(dist, 0)).astype(jnp.int32)
    return pl.pallas_call(
        _bias_kernel,
        in_specs=[
            pl.BlockSpec(memory_space=pltpu.VMEM),
            pl.BlockSpec(memory_space=pltpu.VMEM),
            pl.BlockSpec(memory_space=pltpu.SMEM),
        ],
        out_specs=pl.BlockSpec(memory_space=pltpu.VMEM),
        out_shape=jax.ShapeDtypeStruct((N_Q_HEADS, BLOCK, 2 * BLOCK), F32),
        name="band_bias",
    )(bucket, inwin, rel_bias)


def _mem_attention(qm, mkt_ref, mv_ref):
    logits = [_dot(qm, mkt_ref[h]) for h in range(N_MEM_HEADS)]
    yield
    lane = lax.broadcasted_iota(jnp.int32, (qm.shape[0], LANES), 1)
    halves = [None, None]
    for h in range(N_MEM_HEADS):
        m = jnp.max(logits[h], axis=-1, keepdims=True)
        p = jnp.exp(logits[h] - m).astype(BF16)
        o = _dot(p, mv_ref[h])
        grp, pos = h // 2, h % 2
        val, s = o[:, grp * LANES:(grp + 1) * LANES], o[:, (1 - grp) * LANES:(2 - grp) * LANES]
        live = (lane >= pos * HEAD_DIM) & (lane < (pos + 1) * HEAD_DIM)
        part = jnp.where(live, val / jnp.where(live, s, 1.0), 0.0)
        halves[grp] = part if halves[grp] is None else halves[grp] + part
        yield
    return jnp.concatenate(halves, axis=-1)


def _short_conv(u, b_gate, c_gate, w, vs_ref, rows):
    v = c_gate * u
    base = SUBLANES
    vs_ref[base:base + rows, :] = v
    conv = (w[0:1] * vs_ref[base - 2:base - 2 + rows, :]
            + w[1:2] * vs_ref[base - 1:base - 1 + rows, :]
            + w[2:3] * v)
    return b_gate * conv


def _swa(q, kt_prev_ref, kt_cur_ref, v_prev_ref, v_cur_ref, bias_ref, sinks_ref, layer, first):
    col = lax.broadcasted_iota(jnp.int32, (BLOCK, 2 * BLOCK), 1)
    kill_prev = jnp.logical_and(first, col < BLOCK)
    zeros = jnp.zeros((HEAD_DIM, 2 * BLOCK), BF16)
    ones = jnp.ones((2 * BLOCK, LANES), BF16)

    def scores(n, hk):
        pair = slice((hk // 2) * LANES, (hk // 2 + 1) * LANES)
        if n == 0:
            kt = jnp.concatenate([kt_prev_ref[hk * HEAD_DIM:(hk + 1) * HEAD_DIM, :],
                                  kt_cur_ref[hk * HEAD_DIM:(hk + 1) * HEAD_DIM, 0:BLOCK]], axis=1)
            v2 = jnp.concatenate([v_prev_ref[:, pair], v_cur_ref[0:BLOCK, pair]], axis=0)
        else:
            keys = slice((n - 1) * BLOCK, (n + 1) * BLOCK)
            kt = kt_cur_ref[hk * HEAD_DIM:(hk + 1) * HEAD_DIM, keys]
            v2 = v_cur_ref[keys, pair]
        k_lo = jnp.concatenate([kt, zeros], axis=0)
        k_hi = jnp.concatenate([zeros, kt], axis=0)
        logits = []
        for g in range(GROUP):
            head = hk * GROUP + g
            grp, pos = head // 2, head % 2
            qg = q[n * BLOCK:(n + 1) * BLOCK, grp * LANES:(grp + 1) * LANES]
            bias = bias_ref[head]
            if n == 0:
                bias = jnp.where(kill_prev, NEG, bias)
            logits.append(_dot(qg, k_hi if pos else k_lo) + bias)
        return logits, v2

    def attend(hk, logits, v2):
        ps, sink_terms = [], []
        for g in range(GROUP):
            sink = sinks_ref[layer, hk * GROUP + g]
            m = jnp.maximum(jnp.max(logits[g], axis=-1, keepdims=True), sink)
            ps.append(jnp.exp(logits[g] - m).astype(BF16))
            sink_terms.append(jnp.exp(sink - m))
        o3 = _dot(jnp.concatenate(ps, axis=0), jnp.concatenate([v2, ones], axis=1))
        half = slice((hk % 2) * HEAD_DIM, (hk % 2 + 1) * HEAD_DIM)
        sums = slice(LANES + half.start, LANES + half.stop)
        outs = []
        for g in range(GROUP):
            rows = slice(g * BLOCK, (g + 1) * BLOCK)
            outs.append(o3[rows, half] / (o3[rows, sums] + sink_terms[g]))
        return outs

    units = [(i, hk) for i in range(q.shape[0] // BLOCK) for hk in range(N_KV_HEADS)]
    outs = {i: [] for i, _ in units}
    ahead = 2
    pending = [scores(*u) for u in units[:ahead]]
    for k, (i, hk) in enumerate(units):
        cur = pending.pop(0)
        if k + ahead < len(units):
            pending.append(scores(*units[k + ahead]))
        yield
        outs[i] += attend(hk, *cur)
    return jnp.concatenate([jnp.concatenate(outs[i], axis=-1) for i in sorted(outs)], axis=0)


def _ffn(x_ref, h_ref, xf_ref, wg_ref, wu_ref, wd_ref):
    h = h_ref[...]
    acts = []
    for c in range(D_FF // MXU_DIM):
        sl = slice(c * MXU_DIM, (c + 1) * MXU_DIM)
        gate = _dot(h, wg_ref[:, sl])
        yield
        up = _dot(h, wu_ref[:, sl])
        acts.append((gate * jax.nn.sigmoid(gate) * up).astype(BF16))
        if c < D_MODEL // MXU_DIM:
            xf_ref[:, sl] = x_ref[:, sl]
        yield
    yield "tail"
    act = jnp.concatenate(acts, axis=-1)
    for n in range(D_MODEL // MXU_DIM):
        cs = slice(n * MXU_DIM, (n + 1) * MXU_DIM)
        yield cs, xf_ref[:, cs] + _dot(act, wd_ref[:, cs])


def _advance(gen):
    try:
        return next(gen)
    except StopIteration:
        return "end"


def _run(gen):
    while _advance(gen) != "end":
        pass


def _interleave(ffn, n_ffn, mix, n_mix, ffn_blocks_first):
    advance = _advance
    f_state = m_state = None
    done = 0
    for i in range(n_ffn):
        if f_state is None:
            f_state = advance(ffn)
        while m_state is None and done * n_ffn < (i + 1) * n_mix:
            m_state = advance(mix)
            done += 1
    while f_state is None:
        f_state = advance(ffn)
    while m_state is None:
        m_state = advance(mix)
    assert (f_state, m_state) == ("tail", "tail")
    for _ in range(ffn_blocks_first):
        advance(ffn)
    assert advance(mix) == "end"
    _run(ffn)


def _layer_kernel(*refs, mixer, kv, epilogue, layer, tm, tiles_per_batch, n_cast):
    it = iter(refs)
    x_ref, gmix_ref, win_ref, mkt_ref, mv_ref, wout_ref = (next(it) for _ in range(6))
    gffn_ref, wg_hbm, wu_hbm, wd_hbm = (next(it) for _ in range(4))
    if mixer == "conv":
        convw_ref = next(it)
    else:
        if kv == "compute":
            gkv_ref, wkt_ref, wv_ref = next(it), next(it), next(it)
        else:
            kt_in_ref, v_in_ref = next(it), next(it)
        bias_ref, sinks_ref = next(it), next(it)
    if epilogue == "final":
        gfin_ref = next(it)
    cast_in = [next(it) for _ in range(n_cast)]
    out_ref = next(it)
    if kv == "compute":
        kt_out_ref, v_out_ref = next(it), next(it)
    cast_out = [next(it) for _ in range(n_cast)]
    x1_ref, h2_ref, xf_ref = next(it), next(it), next(it)
    wg_ref, wu_ref, wd_ref, w_sem = (next(it) for _ in range(4))
    if mixer == "conv":
        vs_ref = next(it)
    else:
        kt_prev_ref, v_prev_ref = next(it), next(it)
        kt_cur_ref, v_cur_ref = (kt_out_ref, v_out_ref) if kv == "compute" else (kt_in_ref, v_in_ref)

    t = pl.program_id(0)
    n_tiles = pl.num_programs(0) - 1
    first = (t % tiles_per_batch) == 0

    ffn_weight_loads = [pltpu.make_async_copy(src, dst, w_sem.at[i]) for i, (src, dst) in
                        enumerate(((wg_hbm, wg_ref), (wu_hbm, wu_ref), (wd_hbm, wd_ref)))]

    @pl.when(first)
    def _():
        if mixer == "conv":
            vs_ref[0:SUBLANES, :] = jnp.zeros((SUBLANES, CONV_WIDTH), F32)
        else:
            kt_prev_ref[...] = jnp.zeros((KV_WIDTH, BLOCK), BF16)
            v_prev_ref[...] = jnp.zeros((BLOCK, KV_WIDTH), BF16)

    def mix():
        x = x_ref[...]
        h = _rms(x, gmix_ref[...]).astype(BF16)
        yield
        if kv == "compute":
            hkv = _rms(x, gkv_ref[...]).astype(BF16)
            kt_out_ref[...] = (_dot_nt(wkt_ref[...], hkv) * SCALE).astype(BF16)
            yield
            v_out_ref[...] = _dot(hkv, wv_ref[...]).astype(BF16)
            yield
        widths = (CONV_WIDTH,) * 3 + (MEM_WIDTH,) if mixer == "conv" else (Q_WIDTH, MEM_WIDTH)
        fields, lo = [], 0
        for w in widths:
            fields.append(_dot(h, win_ref[:, lo:lo + w]))
            lo += w
            yield
        for src, dst in zip(cast_in, cast_out):
            dst[...] = src[...].astype(BF16)
        if mixer == "conv":
            u, b_gate, c_gate, qm = fields
            y_tok = _short_conv(u, b_gate, c_gate, convw_ref[...], vs_ref, tm)
            yield
        else:
            q, qm = fields
            y_tok = yield from _swa(q.astype(BF16), kt_prev_ref, kt_cur_ref, v_prev_ref, v_cur_ref,
                                    bias_ref, sinks_ref, layer, first)
        y_mem = yield from _mem_attention(qm.astype(BF16), mkt_ref, mv_ref)
        y = jnp.concatenate([y_tok, y_mem], axis=-1).astype(BF16)
        half = D_MODEL // 2
        lo_half = x[:, :half] + _dot(y, wout_ref[:, :half])
        yield
        hi_half = x[:, half:] + _dot(y, wout_ref[:, half:])
        yield "tail"
        x = jnp.concatenate([lo_half, hi_half], axis=-1)
        x1_ref[...] = x
        h2_ref[...] = _rms(x, gffn_ref[...]).astype(BF16)
        if mixer == "conv":
            vs_ref[0:SUBLANES, :] = vs_ref[tm:tm + SUBLANES, :]
        else:
            kt_prev_ref[...] = kt_cur_ref[:, tm - BLOCK:tm]
            v_prev_ref[...] = v_cur_ref[tm - BLOCK:tm, :]

    def ffn():
        blocks = []
        for item in _ffn(x1_ref, h2_ref, xf_ref, wg_ref, wu_ref, wd_ref):
            if isinstance(item, tuple):
                cs, block = item
                if epilogue == "none":
                    out_ref[:, cs] = block
                blocks.append(block)
                yield
            else:
                yield item
        if epilogue == "final":
            out_ref[...] = _rms(jnp.concatenate(blocks, axis=-1), gfin_ref[...])

    @pl.when(t == 0)
    def _():
        for load in ffn_weight_loads:
            load.start()
        _run(mix())
        for load in ffn_weight_loads:
            load.wait()

    @pl.when(jnp.logical_and(t > 0, t < n_tiles))
    def _():
        n_mix = {"conv": 13, "swa": 10 + (tm // BLOCK) * N_KV_HEADS}[mixer]
        n_mix += 2 if kv == "compute" else 0
        _interleave(ffn(), 2 * (D_FF // MXU_DIM), mix(), n_mix, ffn_blocks_first=1)

    @pl.when(t == n_tiles)
    def _():
        _run(ffn())


def _const_spec(shape, index):
    return pl.BlockSpec(shape, lambda t: index, pipeline_mode=pl.Buffered(1))


def _cast_rows(n_rows, n_steps):
    tile = 2 * SUBLANES
    return min(r for r in range(tile, n_rows + 1, tile)
               if n_rows % r == 0 and n_rows // r <= n_steps)


def _layer(x, *, seq, mixer, kv, epilogue, layer, sub, w, cast_next, p):
    tokens, _ = x.shape
    tm = TOKEN_TILE
    assert seq % tm == 0 and tm % BLOCK == 0
    tpb = seq // tm
    n_tiles = tokens // tm
    proj_w = A_PROJ if mixer == "conv" else B_PROJ

    def mix_tile(t):
        return jnp.minimum(t, n_tiles - 1)

    def ffn_tile(t):
        return jnp.maximum(t - 1, 0)

    def x_in_map(t):
        return mix_tile(t), 0

    def x_out_map(t):
        return ffn_tile(t), 0

    def mem_map(t):
        return layer, mix_tile(t) // tpb, 0, 0, 0

    def kt_map(t):
        return 0, mix_tile(t)

    def v_map(t):
        return mix_tile(t), 0

    mem_spec = pl.BlockSpec((None, None, N_MEM_HEADS, MEM_WIDTH, N_MEM), mem_map)
    in_specs = [
        pl.BlockSpec((tm, D_MODEL), x_in_map),
        _const_spec((None, 1, D_MODEL), (layer, 0, 0)),
        _const_spec((D_MODEL, proj_w), (0, 0)),
        mem_spec, mem_spec,
        _const_spec((D_MODEL, D_MODEL), (0, 0)),
        _const_spec((None, 1, D_MODEL), (layer, 0, 0)),
        pl.BlockSpec(memory_space=pl.ANY),
        pl.BlockSpec(memory_space=pl.ANY),
        pl.BlockSpec(memory_space=pl.ANY),
    ]
    args = [x, p["norm_mix"], w["in"], p["mkt"], p["mv"], w["out"],
            p["norm_ffn"], w["gate"], w["up"], w["down"]]
    scratch = [pltpu.VMEM((tm, D_MODEL), F32), pltpu.VMEM((tm, D_MODEL), BF16),
               pltpu.VMEM((tm, D_MODEL), F32),
               pltpu.VMEM((D_MODEL, D_FF), BF16), pltpu.VMEM((D_MODEL, D_FF), BF16),
               pltpu.VMEM((D_FF, D_MODEL), BF16), pltpu.SemaphoreType.DMA((3,))]
    if mixer == "conv":
        in_specs.append(_const_spec((None, CONV_K, CONV_WIDTH), (sub, 0, 0)))
        args.append(p["conv_w"])
        scratch.append(pltpu.VMEM((tm + SUBLANES, CONV_WIDTH), F32))
    else:
        if kv == "compute":
            in_specs += [_const_spec((1, D_MODEL), (0, 0)),
                         _const_spec((KV_WIDTH, D_MODEL), (0, 0)),
                         _const_spec((D_MODEL, KV_WIDTH), (0, 0))]
            args += [p["kv_norm"], p["w_kt"], p["w_v"]]
        else:
            in_specs += [pl.BlockSpec((KV_WIDTH, tm), kt_map), pl.BlockSpec((tm, KV_WIDTH), v_map)]
            args += [p["kt"], p["v"]]
        in_specs += [_const_spec((N_Q_HEADS, BLOCK, 2 * BLOCK), (0, 0, 0)),
                     pl.BlockSpec(memory_space=pltpu.SMEM)]
        args += [p["bias"], p["sinks"]]
        scratch += [pltpu.VMEM((KV_WIDTH, BLOCK), BF16), pltpu.VMEM((BLOCK, KV_WIDTH), BF16)]
    if epilogue == "final":
        in_specs.append(_const_spec((1, D_MODEL), (0, 0)))
        args.append(p["final_norm"])

    out_specs = [pl.BlockSpec((tm, D_MODEL), x_out_map)]
    out_shape = [jax.ShapeDtypeStruct(x.shape, F32)]
    if kv == "compute":
        out_specs += [pl.BlockSpec((KV_WIDTH, tm), kt_map), pl.BlockSpec((tm, KV_WIDTH), v_map)]
        out_shape += [jax.ShapeDtypeStruct((KV_WIDTH, tokens), BF16),
                      jax.ShapeDtypeStruct((tokens, KV_WIDTH), BF16)]
    for stack, idx in cast_next:
        _, n_rows, n_cols = stack.shape
        rows = _cast_rows(n_rows, n_tiles)
        last = n_rows // rows - 1
        in_specs.append(pl.BlockSpec((None, rows, n_cols),
                                     lambda t, idx=idx, last=last: (idx, jnp.minimum(t, last), 0)))
        args.append(stack)
        out_specs.append(pl.BlockSpec((rows, n_cols),
                                      lambda t, last=last: (jnp.minimum(t, last), 0)))
        out_shape.append(jax.ShapeDtypeStruct((n_rows, n_cols), BF16))

    outs = pl.pallas_call(
        functools.partial(_layer_kernel, mixer=mixer, kv=kv, epilogue=epilogue, layer=sub, tm=tm,
                          tiles_per_batch=tpb, n_cast=len(cast_next)),
        grid=(n_tiles + 1,),
        in_specs=in_specs,
        out_specs=out_specs,
        out_shape=out_shape,
        scratch_shapes=scratch,
        compiler_params=pltpu.CompilerParams(
            dimension_semantics=("arbitrary",),
            vmem_limit_bytes=VMEM_LIMIT_BYTES),
        name=f"layer{layer}_{mixer}",
    )(*args)
    return outs


def kernel(x, mem, norm_mix, norm_ffn, a_w_in, a_conv_w, a_w_out, kv_norm, w_kv, b_w_q, b_sinks,
           b_w_out, rel_bias, mem_norm, w_mem_kv, w_gate, w_up, w_down, final_norm):
    mkt, mv = _mem_kv(mem, mem_norm, w_mem_kv)
    p = {
        "norm_mix": norm_mix.reshape(DEPTH, 1, D_MODEL),
        "norm_ffn": norm_ffn.reshape(DEPTH, 1, D_MODEL),
        "conv_w": a_conv_w,
        "mkt": mkt, "mv": mv,
        "kv_norm": kv_norm.reshape(1, D_MODEL),
        "w_kt": w_kv[:, :KV_WIDTH].T.astype(BF16),
        "w_v": w_kv[:, KV_WIDTH:].astype(BF16),
        "final_norm": final_norm.reshape(1, D_MODEL),
        "sinks": b_sinks,
        "bias": _band_bias(rel_bias),
    }

    def f32_weights(i):
        stack_in, stack_out, sub = (a_w_in, a_w_out, i) if i < N_A else (b_w_q, b_w_out, i - N_A)
        return {"in": (stack_in, sub), "out": (stack_out, sub),
                "gate": (w_gate, i), "up": (w_up, i), "down": (w_down, i)}

    w = {k: stack[idx].astype(BF16) for k, (stack, idx) in f32_weights(0).items()}
    batch, seq, _ = x.shape
    x = x.reshape(batch * seq, D_MODEL)
    for i in range(DEPTH):
        mixer = "conv" if i < N_A else "swa"
        sub = i if i < N_A else i - N_A
        kv = None if i < N_A else ("compute" if i == N_A else "load")
        epilogue = "final" if i == DEPTH - 1 else "none"
        nxt = f32_weights(i + 1) if i + 1 < DEPTH else {}
        outs = _layer(x, seq=seq, mixer=mixer, kv=kv, epilogue=epilogue, layer=i, sub=sub, w=w,
                      cast_next=list(nxt.values()), p=p)
        x = outs[0]
        if kv == "compute":
            p["kt"], p["v"] = outs[1], outs[2]
        w = dict(zip(nxt.keys(), outs[len(outs) - len(nxt):]))
    return x.reshape(batch, seq, D_MODEL)
```

```python
import functools
import math

import jax
import jax.numpy as jnp
from jax import lax
from jax.experimental import pallas as pl
from jax.experimental.pallas import tpu as pltpu

D_MODEL = 1024
DEPTH = 4
N_MEM = 256
HEAD_DIM = 64
N_MEM_HEADS = 4
MEM_WIDTH = N_MEM_HEADS * HEAD_DIM
CONV_WIDTH = D_MODEL - MEM_WIDTH
CONV_K = 3
N_Q_HEADS = CONV_WIDTH // HEAD_DIM
N_KV_HEADS = 4
GROUP = N_Q_HEADS // N_KV_HEADS
Q_WIDTH = N_Q_HEADS * HEAD_DIM
KV_WIDTH = N_KV_HEADS * HEAD_DIM
A_PROJ = 3 * CONV_WIDTH + MEM_WIDTH
B_PROJ = Q_WIDTH + MEM_WIDTH
WINDOW = 128
BLOCK = 128
REL_BUCKETS = 32
REL_MAX_DIST = 128
D_FF = ((8 * D_MODEL + 3 * 256 - 1) // (3 * 256)) * 256
N_A = DEPTH // 2
N_B = DEPTH - N_A
EPS = 1e-5

SCALE = HEAD_DIM ** -0.5
NEG = -1e30
LANES = 128
SUBLANES = 8
MXU_DIM = 256
TOKEN_TILE = 512
V7X_VMEM_BYTES = 64 * 1024 * 1024
VMEM_LIMIT_BYTES = V7X_VMEM_BYTES - V7X_VMEM_BYTES // 8

BF16 = jnp.bfloat16
F32 = jnp.float32


def _dot(a, b):
    return jnp.dot(a, b, preferred_element_type=F32)


def _dot_nt(a, b):
    return lax.dot_general(a, b, (((1,), (1,)), ((), ())), preferred_element_type=F32)


def _rms(x, g):
    ms = jnp.mean(x * x, axis=-1, keepdims=True)
    return x * lax.rsqrt(ms + EPS) * g


def _mem_kv_kernel(mem_ref, g_ref, w_ref, kt_ref, v_ref, *, batch):
    mem_n = _rms(mem_ref[...], g_ref[...]).astype(BF16)
    kv = _dot(mem_n, w_ref[...].astype(BF16))
    row = lax.broadcasted_iota(jnp.int32, (MEM_WIDTH, N_MEM), 0)
    col = lax.broadcasted_iota(jnp.int32, (N_MEM, MEM_WIDTH), 1)
    for b in range(batch):
        kvb = kv[b * N_MEM:(b + 1) * N_MEM]
        kt = (kvb[:, :MEM_WIDTH] * SCALE).T
        v = kvb[:, MEM_WIDTH:]
        for h in range(N_MEM_HEADS):
            lo, hi = h * HEAD_DIM, (h + 1) * HEAD_DIM
            kt_ref[b, h] = jnp.where((row >= lo) & (row < hi), kt, 0.0).astype(BF16)
            ones_lo = (lo + LANES) % MEM_WIDTH
            fill = jnp.where((col >= ones_lo) & (col < ones_lo + HEAD_DIM), 1.0, 0.0)
            v_ref[b, h] = jnp.where((col >= lo) & (col < hi), v, fill).astype(BF16)


def _mem_kv(mem, mem_norm, w_mem_kv):
    batch = mem.shape[0]
    out_sds = jax.ShapeDtypeStruct((DEPTH, batch, N_MEM_HEADS, MEM_WIDTH, N_MEM), BF16)
    out_spec = pl.BlockSpec((None, batch, N_MEM_HEADS, MEM_WIDTH, N_MEM),
                            lambda i: (i, 0, 0, 0, 0))
    return pl.pallas_call(
        functools.partial(_mem_kv_kernel, batch=batch),
        grid=(DEPTH,),
        in_specs=[
            pl.BlockSpec((batch * N_MEM, D_MODEL), lambda i: (0, 0)),
            pl.BlockSpec((1, D_MODEL), lambda i: (0, 0)),
            pl.BlockSpec((None, D_MODEL, 2 * MEM_WIDTH), lambda i: (i, 0, 0)),
        ],
        out_specs=[out_spec, out_spec],
        out_shape=[out_sds, out_sds],
        compiler_params=pltpu.CompilerParams(dimension_semantics=("arbitrary",)),
        name="mem_kv",
    )(mem.reshape(batch * N_MEM, D_MODEL), mem_norm.reshape(1, D_MODEL), w_mem_kv)


def _bias_kernel(bucket_ref, inwin_ref, table_ref, out_ref):
    bucket = bucket_ref[...]
    inwin = inwin_ref[...] > 0
    for h in range(N_Q_HEADS):
        acc = jnp.zeros((BLOCK, 2 * BLOCK), F32)
        for b in range(REL_BUCKETS):
            acc = jnp.where(bucket == b, table_ref[b, h], acc)
        out_ref[h] = jnp.where(inwin, acc, NEG)


def _rel_bucket(dist):
    max_exact = REL_BUCKETS // 2
    d = jnp.maximum(dist, 1).astype(F32)
    large = max_exact + (jnp.log(d / max_exact) / math.log(REL_MAX_DIST / max_exact)
                         * (REL_BUCKETS - max_exact)).astype(jnp.int32)
    large = jnp.minimum(large, REL_BUCKETS - 1)
    return jnp.where(dist < max_exact, dist, large)


def _band_bias(rel_bias):
    qi = jnp.arange(BLOCK, dtype=jnp.int32)[:, None]
    kj = jnp.arange(2 * BLOCK, dtype=jnp.int32)[None, :]
    dist = qi + BLOCK - kj
    inwin = ((dist >= 0) & (dist < WINDOW)).astype(jnp.int32)
    bucket = _rel_bucket(jnp.maximum(dist, 0)).astype(jnp.int32)
    return pl.pallas_call(
        _bias_kernel,
        in_specs=[
            pl.BlockSpec(memory_space=pltpu.VMEM),
            pl.BlockSpec(memory_space=pltpu.VMEM),
            pl.BlockSpec(memory_space=pltpu.SMEM),
        ],
        out_specs=pl.BlockSpec(memory_space=pltpu.VMEM),
        out_shape=jax.ShapeDtypeStruct((N_Q_HEADS, BLOCK, 2 * BLOCK), F32),
        name="band_bias",
    )(bucket, inwin, rel_bias)


def _mem_attention(qm, mkt_ref, mv_ref):
    logits = [_dot(qm, mkt_ref[h]) for h in range(N_MEM_HEADS)]
    yield
    lane = lax.broadcasted_iota(jnp.int32, (qm.shape[0], LANES), 1)
    halves = [None, None]
    for h in range(N_MEM_HEADS):
        m = jnp.max(logits[h], axis=-1, keepdims=True)
        p = jnp.exp(logits[h] - m).astype(BF16)
        o = _dot(p, mv_ref[h])
        grp, pos = h // 2, h % 2
        val, s = o[:, grp * LANES:(grp + 1) * LANES], o[:, (1 - grp) * LANES:(2 - grp) * LANES]
        live = (lane >= pos * HEAD_DIM) & (lane < (pos + 1) * HEAD_DIM)
        part = jnp.where(live, val / jnp.where(live, s, 1.0), 0.0)
        halves[grp] = part if halves[grp] is None else halves[grp] + part
        yield
    return jnp.concatenate(halves, axis=-1)


def _short_conv(u, b_gate, c_gate, w, vs_ref, rows):
    v = c_gate * u
    base = SUBLANES
    vs_ref[base:base + rows, :] = v
    conv = (w[0:1] * vs_ref[base - 2:base - 2 + rows, :]
            + w[1:2] * vs_ref[base - 1:base - 1 + rows, :]
            + w[2:3] * v)
    return b_gate * conv


def _swa(q, kt_prev_ref, kt_cur_ref, v_prev_ref, v_cur_ref, bias_ref, sinks_ref, layer, first):
    col = lax.broadcasted_iota(jnp.int32, (BLOCK, 2 * BLOCK), 1)
    kill_prev = jnp.logical_and(first, col < BLOCK)
    zeros = jnp.zeros((HEAD_DIM, 2 * BLOCK), BF16)
    ones = jnp.ones((2 * BLOCK, LANES), BF16)

    def scores(n, hk):
        pair = slice((hk // 2) * LANES, (hk // 2 + 1) * LANES)
        if n == 0:
            kt = jnp.concatenate([kt_prev_ref[hk * HEAD_DIM:(hk + 1) * HEAD_DIM, :],
                                  kt_cur_ref[hk * HEAD_DIM:(hk + 1) * HEAD_DIM, 0:BLOCK]], axis=1)
            v2 = jnp.concatenate([v_prev_ref[:, pair], v_cur_ref[0:BLOCK, pair]], axis=0)
        else:
            keys = slice((n - 1) * BLOCK, (n + 1) * BLOCK)
            kt = kt_cur_ref[hk * HEAD_DIM:(hk + 1) * HEAD_DIM, keys]
            v2 = v_cur_ref[keys, pair]
        k_lo = jnp.concatenate([kt, zeros], axis=0)
        k_hi = jnp.concatenate([zeros, kt], axis=0)
        heads = range(hk * GROUP, (hk + 1) * GROUP)
        raw = {}
        for grp in sorted({head // 2 for head in heads}):
            qg = q[n * BLOCK:(n + 1) * BLOCK, grp * LANES:(grp + 1) * LANES]
            members = [head for head in heads if head // 2 == grp]
            if len(members) == 2:
                both = _dot(qg, jnp.concatenate([k_lo, k_hi], axis=1))
                raw[members[0]], raw[members[1]] = both[:, :2 * BLOCK], both[:, 2 * BLOCK:]
            else:
                raw[members[0]] = _dot(qg, k_hi if members[0] % 2 else k_lo)
        logits = []
        for head in heads:
            bias = bias_ref[head]
            if n == 0:
                bias = jnp.where(kill_prev, NEG, bias)
            logits.append(raw[head] + bias)
        return logits, v2

    def attend(hk, logits, v2):
        ps, sink_terms = [], []
        for g in range(GROUP):
            sink = sinks_ref[layer, hk * GROUP + g]
            m = jnp.maximum(jnp.max(logits[g], axis=-1, keepdims=True), sink)
            ps.append(jnp.exp(logits[g] - m).astype(BF16))
            sink_terms.append(jnp.exp(sink - m))
        o3 = _dot(jnp.concatenate(ps, axis=0), jnp.concatenate([v2, ones], axis=1))
        half = slice((hk % 2) * HEAD_DIM, (hk % 2 + 1) * HEAD_DIM)
        sums = slice(LANES + half.start, LANES + half.stop)
        outs = []
        for g in range(GROUP):
            rows = slice(g * BLOCK, (g + 1) * BLOCK)
            outs.append(o3[rows, half] / (o3[rows, sums] + sink_terms[g]))
        return outs

    units = [(i, hk) for i in range(q.shape[0] // BLOCK) for hk in range(N_KV_HEADS)]
    outs = {i: [] for i, _ in units}
    nxt = scores(*units[0])
    for k, (i, hk) in enumerate(units):
        cur = nxt
        if k + 1 < len(units):
            nxt = scores(*units[k + 1])
        yield
        outs[i] += attend(hk, *cur)
    return jnp.concatenate([jnp.concatenate(outs[i], axis=-1) for i in sorted(outs)], axis=0)


def _ffn(x_ref, h_ref, xf_ref, wg_ref, wu_ref, wd_ref):
    h = h_ref[...]
    acts = []
    for c in range(D_FF // MXU_DIM):
        sl = slice(c * MXU_DIM, (c + 1) * MXU_DIM)
        gate = _dot(h, wg_ref[:, sl])
        yield
        up = _dot(h, wu_ref[:, sl])
        acts.append((gate * jax.nn.sigmoid(gate) * up).astype(BF16))
        if c < D_MODEL // MXU_DIM:
            xf_ref[:, sl] = x_ref[:, sl]
        yield
    yield "tail"
    act = jnp.concatenate(acts, axis=-1)
    for n in range(D_MODEL // MXU_DIM):
        cs = slice(n * MXU_DIM, (n + 1) * MXU_DIM)
        yield cs, xf_ref[:, cs] + _dot(act, wd_ref[:, cs])


def _advance(gen):
    try:
        return next(gen)
    except StopIteration:
        return "end"


def _run(gen):
    while _advance(gen) != "end":
        pass


def _interleave(ffn, n_ffn, mix, n_mix, ffn_blocks_first):
    advance = _advance
    f_state = m_state = None
    done = 0
    for i in range(n_ffn):
        if f_state is None:
            f_state = advance(ffn)
        while m_state is None and done * n_ffn < (i + 1) * n_mix:
            m_state = advance(mix)
            done += 1
    while f_state is None:
        f_state = advance(ffn)
    while m_state is None:
        m_state = advance(mix)
    assert (f_state, m_state) == ("tail", "tail")
    for _ in range(ffn_blocks_first):
        advance(ffn)
    assert advance(mix) == "end"
    _run(ffn)


def _layer_kernel(*refs, mixer, kv, epilogue, layer, tm, tiles_per_batch, n_cast):
    it = iter(refs)
    x_ref, gmix_ref, win_ref, mkt_ref, mv_ref, wout_ref = (next(it) for _ in range(6))
    gffn_ref, wg_hbm, wu_hbm, wd_hbm = (next(it) for _ in range(4))
    if mixer == "conv":
        convw_ref = next(it)
    else:
        if kv == "compute":
            gkv_ref, wkt_ref, wv_ref = next(it), next(it), next(it)
        else:
            kt_in_ref, v_in_ref = next(it), next(it)
        bias_ref, sinks_ref = next(it), next(it)
    if epilogue == "final":
        gfin_ref = next(it)
    cast_in = [next(it) for _ in range(n_cast)]
    out_ref = next(it)
    if kv == "compute":
        kt_out_ref, v_out_ref = next(it), next(it)
    cast_out = [next(it) for _ in range(n_cast)]
    x1_ref, h2_ref, xf_ref = next(it), next(it), next(it)
    wg_ref, wu_ref, wd_ref, w_sem = (next(it) for _ in range(4))
    if mixer == "conv":
        vs_ref = next(it)
    else:
        kt_prev_ref, v_prev_ref = next(it), next(it)
        kt_cur_ref, v_cur_ref = (kt_out_ref, v_out_ref) if kv == "compute" else (kt_in_ref, v_in_ref)

    t = pl.program_id(0)
    n_tiles = pl.num_programs(0) - 1
    first = (t % tiles_per_batch) == 0

    ffn_weight_loads = [pltpu.make_async_copy(src, dst, w_sem.at[i]) for i, (src, dst) in
                        enumerate(((wg_hbm, wg_ref), (wu_hbm, wu_ref), (wd_hbm, wd_ref)))]

    @pl.when(first)
    def _():
        if mixer == "conv":
            vs_ref[0:SUBLANES, :] = jnp.zeros((SUBLANES, CONV_WIDTH), F32)
        else:
            kt_prev_ref[...] = jnp.zeros((KV_WIDTH, BLOCK), BF16)
            v_prev_ref[...] = jnp.zeros((BLOCK, KV_WIDTH), BF16)

    def mix():
        x = x_ref[...]
        h = _rms(x, gmix_ref[...]).astype(BF16)
        yield
        if kv == "compute":
            hkv = _rms(x, gkv_ref[...]).astype(BF16)
            kt_out_ref[...] = (_dot_nt(wkt_ref[...], hkv) * SCALE).astype(BF16)
            yield
            v_out_ref[...] = _dot(hkv, wv_ref[...]).astype(BF16)
            yield
        widths = (CONV_WIDTH,) * 3 + (MEM_WIDTH,) if mixer == "conv" else (Q_WIDTH, MEM_WIDTH)
        fields, lo = [], 0
        for w in widths:
            fields.append(_dot(h, win_ref[:, lo:lo + w]))
            lo += w
            yield
        for src, dst in zip(cast_in, cast_out):
            dst[...] = src[...].astype(BF16)
        if mixer == "conv":
            u, b_gate, c_gate, qm = fields
            y_tok = _short_conv(u, b_gate, c_gate, convw_ref[...], vs_ref, tm)
            yield
        else:
            q, qm = fields
            y_tok = yield from _swa(q.astype(BF16), kt_prev_ref, kt_cur_ref, v_prev_ref, v_cur_ref,
                                    bias_ref, sinks_ref, layer, first)
        y_mem = yield from _mem_attention(qm.astype(BF16), mkt_ref, mv_ref)
        y = jnp.concatenate([y_tok, y_mem], axis=-1).astype(BF16)
        half = D_MODEL // 2
        lo_half = x[:, :half] + _dot(y, wout_ref[:, :half])
        yield
        hi_half = x[:, half:] + _dot(y, wout_ref[:, half:])
        yield "tail"
        x = jnp.concatenate([lo_half, hi_half], axis=-1)
        x1_ref[...] = x
        h2_ref[...] = _rms(x, gffn_ref[...]).astype(BF16)
        if mixer == "conv":
            vs_ref[0:SUBLANES, :] = vs_ref[tm:tm + SUBLANES, :]
        else:
            kt_prev_ref[...] = kt_cur_ref[:, tm - BLOCK:tm]
            v_prev_ref[...] = v_cur_ref[tm - BLOCK:tm, :]

    def ffn():
        blocks = []
        for item in _ffn(x1_ref, h2_ref, xf_ref, wg_ref, wu_ref, wd_ref):
            if isinstance(item, tuple):
                cs, block = item
                if epilogue == "none":
                    out_ref[:, cs] = block
                blocks.append(block)
                yield
            else:
                yield item
        if epilogue == "final":
            out_ref[...] = _rms(jnp.concatenate(blocks, axis=-1), gfin_ref[...])

    @pl.when(t == 0)
    def _():
        for load in ffn_weight_loads:
            load.start()
        _run(mix())
        for load in ffn_weight_loads:
            load.wait()

    @pl.when(jnp.logical_and(t > 0, t < n_tiles))
    def _():
        n_mix = {"conv": 13, "swa": 10 + (tm // BLOCK) * N_KV_HEADS}[mixer]
        n_mix += 2 if kv == "compute" else 0
        _interleave(ffn(), 2 * (D_FF // MXU_DIM), mix(), n_mix, ffn_blocks_first=1)

    @pl.when(t == n_tiles)
    def _():
        _run(ffn())


def _const_spec(shape, index):
    return pl.BlockSpec(shape, lambda t: index, pipeline_mode=pl.Buffered(1))


def _cast_rows(n_rows, n_steps):
    tile = 2 * SUBLANES
    return min(r for r in range(tile, n_rows + 1, tile)
               if n_rows % r == 0 and n_rows // r <= n_steps)


def _layer(x, *, seq, mixer, kv, epilogue, layer, sub, w, cast_next, p):
    tokens, _ = x.shape
    tm = TOKEN_TILE
    assert seq % tm == 0 and tm % BLOCK == 0
    tpb = seq // tm
    n_tiles = tokens // tm
    proj_w = A_PROJ if mixer == "conv" else B_PROJ

    def mix_tile(t):
        return jnp.minimum(t, n_tiles - 1)

    def ffn_tile(t):
        return jnp.maximum(t - 1, 0)

    def x_in_map(t):
        return mix_tile(t), 0

    def x_out_map(t):
        return ffn_tile(t), 0

    def mem_map(t):
        return layer, mix_tile(t) // tpb, 0, 0, 0

    def kt_map(t):
        return 0, mix_tile(t)

    def v_map(t):
        return mix_tile(t), 0

    mem_spec = pl.BlockSpec((None, None, N_MEM_HEADS, MEM_WIDTH, N_MEM), mem_map)
    in_specs = [
        pl.BlockSpec((tm, D_MODEL), x_in_map),
        _const_spec((None, 1, D_MODEL), (layer, 0, 0)),
        _const_spec((D_MODEL, proj_w), (0, 0)),
        mem_spec, mem_spec,
        _const_spec((D_MODEL, D_MODEL), (0, 0)),
        _const_spec((None, 1, D_MODEL), (layer, 0, 0)),
        pl.BlockSpec(memory_space=pl.ANY),
        pl.BlockSpec(memory_space=pl.ANY),
        pl.BlockSpec(memory_space=pl.ANY),
    ]
    args = [x, p["norm_mix"], w["in"], p["mkt"], p["mv"], w["out"],
            p["norm_ffn"], w["gate"], w["up"], w["down"]]
    scratch = [pltpu.VMEM((tm, D_MODEL), F32), pltpu.VMEM((tm, D_MODEL), BF16),
               pltpu.VMEM((tm, D_MODEL), F32),
               pltpu.VMEM((D_MODEL, D_FF), BF16), pltpu.VMEM((D_MODEL, D_FF), BF16),
               pltpu.VMEM((D_FF, D_MODEL), BF16), pltpu.SemaphoreType.DMA((3,))]
    if mixer == "conv":
        in_specs.append(_const_spec((None, CONV_K, CONV_WIDTH), (sub, 0, 0)))
        args.append(p["conv_w"])
        scratch.append(pltpu.VMEM((tm + SUBLANES, CONV_WIDTH), F32))
    else:
        if kv == "compute":
            in_specs += [_const_spec((1, D_MODEL), (0, 0)),
                         _const_spec((KV_WIDTH, D_MODEL), (0, 0)),
                         _const_spec((D_MODEL, KV_WIDTH), (0, 0))]
            args += [p["kv_norm"], p["w_kt"], p["w_v"]]
        else:
            in_specs += [pl.BlockSpec((KV_WIDTH, tm), kt_map), pl.BlockSpec((tm, KV_WIDTH), v_map)]
            args += [p["kt"], p["v"]]
        in_specs += [_const_spec((N_Q_HEADS, BLOCK, 2 * BLOCK), (0, 0, 0)),
                     pl.BlockSpec(memory_space=pltpu.SMEM)]
        args += [p["bias"], p["sinks"]]
        scratch += [pltpu.VMEM((KV_WIDTH, BLOCK), BF16), pltpu.VMEM((BLOCK, KV_WIDTH), BF16)]
    if epilogue == "final":
        in_specs.append(_const_spec((1, D_MODEL), (0, 0)))
        args.append(p["final_norm"])

    out_specs = [pl.BlockSpec((tm, D_MODEL), x_out_map)]
    out_shape = [jax.ShapeDtypeStruct(x.shape, F32)]
    if kv == "compute":
        out_specs += [pl.BlockSpec((KV_WIDTH, tm), kt_map), pl.BlockSpec((tm, KV_WIDTH), v_map)]
        out_shape += [jax.ShapeDtypeStruct((KV_WIDTH, tokens), BF16),
                      jax.ShapeDtypeStruct((tokens, KV_WIDTH), BF16)]
    for stack, idx in cast_next:
        _, n_rows, n_cols = stack.shape
        rows = _cast_rows(n_rows, n_tiles)
        last = n_rows // rows - 1
        in_specs.append(pl.BlockSpec((None, rows, n_cols),
                                     lambda t, idx=idx, last=last: (idx, jnp.minimum(t, last), 0)))
        args.append(stack)
        out_specs.append(pl.BlockSpec((rows, n_cols),
                                      lambda t, last=last: (jnp.minimum(t, last), 0)))
        out_shape.append(jax.ShapeDtypeStruct((n_rows, n_cols), BF16))

    outs = pl.pallas_call(
        functools.partial(_layer_kernel, mixer=mixer, kv=kv, epilogue=epilogue, layer=sub, tm=tm,
                          tiles_per_batch=tpb, n_cast=len(cast_next)),
        grid=(n_tiles + 1,),
        in_specs=in_specs,
        out_specs=out_specs,
        out_shape=out_shape,
        scratch_shapes=scratch,
        compiler_params=pltpu.CompilerParams(
            dimension_semantics=("arbitrary",),
            vmem_limit_bytes=VMEM_LIMIT_BYTES),
        name=f"layer{layer}_{mixer}",
    )(*args)
    return outs


def kernel(x, mem, norm_mix, norm_ffn, a_w_in, a_conv_w, a_w_out, kv_norm, w_kv, b_w_q, b_sinks,
           b_w_out, rel_bias, mem_norm, w_mem_kv, w_gate, w_up, w_down, final_norm):
    mkt, mv = _mem_kv(mem, mem_norm, w_mem_kv)
    p = {
        "norm_mix": norm_mix.reshape(DEPTH, 1, D_MODEL),
        "norm_ffn": norm_ffn.reshape(DEPTH, 1, D_MODEL),
        "conv_w": a_conv_w,
        "mkt": mkt, "mv": mv,
        "kv_norm": kv_norm.reshape(1, D_MODEL),
        "w_kt": w_kv[:, :KV_WIDTH].T.astype(BF16),
        "w_v": w_kv[:, KV_WIDTH:].astype(BF16),
        "final_norm": final_norm.reshape(1, D_MODEL),
        "sinks": b_sinks,
        "bias": _band_bias(rel_bias),
    }

    def f32_weights(i):
        stack_in, stack_out, sub = (a_w_in, a_w_out, i) if i < N_A else (b_w_q, b_w_out, i - N_A)
        return {"in": (stack_in, sub), "out": (stack_out, sub),
                "gate": (w_gate, i), "up": (w_up, i), "down": (w_down, i)}

    w = {k: stack[idx].astype(BF16) for k, (stack, idx) in f32_weights(0).items()}
    batch, seq, _ = x.shape
    x = x.reshape(batch * seq, D_MODEL)
    for i in range(DEPTH):
        mixer = "conv" if i < N_A else "swa"
        sub = i if i < N_A else i - N_A
        kv = None if i < N_A else ("compute" if i == N_A else "load")
        epilogue = "final" if i == DEPTH - 1 else "none"
        nxt = f32_weights(i + 1) if i + 1 < DEPTH else {}
        outs = _layer(x, seq=seq, mixer=mixer, kv=kv, epilogue=epilogue, layer=i, sub=sub, w=w,
                      cast_next=list(nxt.values()), p=p)
        x = outs[0]
        if kv == "compute":
            p["kt"], p["v"] = outs[1], outs[2]
        w = dict(zip(nxt.keys(), outs[len(outs) - len(nxt):]))
    return x.reshape(batch, seq, D_MODEL)
```

```python
import functools
import math

import jax
import jax.numpy as jnp
from jax import lax
from jax.experimental import pallas as pl
from jax.experimental.pallas import tpu as pltpu

D_MODEL = 1024
DEPTH = 4
N_MEM = 256
HEAD_DIM = 64
N_MEM_HEADS = 4
MEM_WIDTH = N_MEM_HEADS * HEAD_DIM
CONV_WIDTH = D_MODEL - MEM_WIDTH
CONV_K = 3
N_Q_HEADS = CONV_WIDTH // HEAD_DIM
N_KV_HEADS = 4
GROUP = N_Q_HEADS // N_KV_HEADS
Q_WIDTH = N_Q_HEADS * HEAD_DIM
KV_WIDTH = N_KV_HEADS * HEAD_DIM
A_PROJ = 3 * CONV_WIDTH + MEM_WIDTH
B_PROJ = Q_WIDTH + MEM_WIDTH
WINDOW = 128
BLOCK = 128
REL_BUCKETS = 32
REL_MAX_DIST = 128
D_FF = ((8 * D_MODEL + 3 * 256 - 1) // (3 * 256)) * 256
N_A = DEPTH // 2
N_B = DEPTH - N_A
EPS = 1e-5

SCALE = HEAD_DIM ** -0.5
NEG = -1e30
LANES = 128
SUBLANES = 8
MXU_DIM = 256
TOKEN_TILE = 512
V7X_VMEM_BYTES = 64 * 1024 * 1024
VMEM_LIMIT_BYTES = V7X_VMEM_BYTES - V7X_VMEM_BYTES // 8

BF16 = jnp.bfloat16
F32 = jnp.float32


def _dot(a, b):
    return jnp.dot(a, b, preferred_element_type=F32)


def _dot_nt(a, b):
    return lax.dot_general(a, b, (((1,), (1,)), ((), ())), preferred_element_type=F32)


def _rms(x, g):
    ms = jnp.mean(x * x, axis=-1, keepdims=True)
    return x * lax.rsqrt(ms + EPS) * g


def _mem_kv_kernel(mem_ref, g_ref, w_ref, kt_ref, v_ref, *, batch):
    mem_n = _rms(mem_ref[...], g_ref[...]).astype(BF16)
    kv = _dot(mem_n, w_ref[...].astype(BF16))
    row = lax.broadcasted_iota(jnp.int32, (MEM_WIDTH, N_MEM), 0)
    col = lax.broadcasted_iota(jnp.int32, (N_MEM, MEM_WIDTH), 1)
    for b in range(batch):
        kvb = kv[b * N_MEM:(b + 1) * N_MEM]
        kt = (kvb[:, :MEM_WIDTH] * SCALE).T
        v = kvb[:, MEM_WIDTH:]
        for h in range(N_MEM_HEADS):
            lo, hi = h * HEAD_DIM, (h + 1) * HEAD_DIM
            kt_ref[b, h] = jnp.where((row >= lo) & (row < hi), kt, 0.0).astype(BF16)
            ones_lo = (lo + LANES) % MEM_WIDTH
            fill = jnp.where((col >= ones_lo) & (col < ones_lo + HEAD_DIM), 1.0, 0.0)
            v_ref[b, h] = jnp.where((col >= lo) & (col < hi), v, fill).astype(BF16)


def _mem_kv(mem, mem_norm, w_mem_kv):
    batch = mem.shape[0]
    out_sds = jax.ShapeDtypeStruct((DEPTH, batch, N_MEM_HEADS, MEM_WIDTH, N_MEM), BF16)
    out_spec = pl.BlockSpec((None, batch, N_MEM_HEADS, MEM_WIDTH, N_MEM),
                            lambda i: (i, 0, 0, 0, 0))
    return pl.pallas_call(
        functools.partial(_mem_kv_kernel, batch=batch),
        grid=(DEPTH,),
        in_specs=[
            pl.BlockSpec((batch * N_MEM, D_MODEL), lambda i: (0, 0)),
            pl.BlockSpec((1, D_MODEL), lambda i: (0, 0)),
            pl.BlockSpec((None, D_MODEL, 2 * MEM_WIDTH), lambda i: (i, 0, 0)),
        ],
        out_specs=[out_spec, out_spec],
        out_shape=[out_sds, out_sds],
        compiler_params=pltpu.CompilerParams(dimension_semantics=("arbitrary",)),
        name="mem_kv",
    )(mem.reshape(batch * N_MEM, D_MODEL), mem_norm.reshape(1, D_MODEL), w_mem_kv)


def _bias_kernel(bucket_ref, inwin_ref, table_ref, out_ref):
    bucket = bucket_ref[...]
    inwin = inwin_ref[...] > 0
    for h in range(N_Q_HEADS):
        acc = jnp.zeros((BLOCK, 2 * BLOCK), F32)
        for b in range(REL_BUCKETS):
            acc = jnp.where(bucket == b, table_ref[b, h], acc)
        out_ref[h] = jnp.where(inwin, acc, NEG)


def _rel_bucket(dist):
    max_exact = REL_BUCKETS // 2
    d = jnp.maximum(dist, 1).astype(F32)
    large = max_exact + (jnp.log(d / max_exact) / math.log(REL_MAX_DIST / max_exact)
                         * (REL_BUCKETS - max_exact)).astype(jnp.int32)
    large = jnp.minimum(large, REL_BUCKETS - 1)
    return jnp.where(dist < max_exact, dist, large)


def _band_bias(rel_bias):
    qi = jnp.arange(BLOCK, dtype=jnp.int32)[:, None]
    kj = jnp.arange(2 * BLOCK, dtype=jnp.int32)[None, :]
    dist = qi + BLOCK - kj
    inwin = ((dist >= 0) & (dist < WINDOW)).astype(jnp.int32)
    bucket = _rel_bucket(jnp.maximum(dist, 0)).astype(jnp.int32)
    return pl.pallas_call(
        _bias_kernel,
        in_specs=[
            pl.BlockSpec(memory_space=pltpu.VMEM),
            pl.BlockSpec(memory_space=pltpu.VMEM),
            pl.BlockSpec(memory_space=pltpu.SMEM),
        ],
        out_specs=pl.BlockSpec(memory_space=pltpu.VMEM),
        out_shape=jax.ShapeDtypeStruct((N_Q_HEADS, BLOCK, 2 * BLOCK), F32),
        name="band_bias",
    )(bucket, inwin, rel_bias)


def _mem_attention(qm, mkt_ref, mv_ref):
    logits = [_dot(qm, mkt_ref[h]) for h in range(N_MEM_HEADS)]
    yield
    lane = lax.broadcasted_iota(jnp.int32, (qm.shape[0], LANES), 1)
    halves = [None, None]
    for h in range(N_MEM_HEADS):
        m = jnp.max(logits[h], axis=-1, keepdims=True)
        p = jnp.exp(logits[h] - m).astype(BF16)
        o = _dot(p, mv_ref[h])
        grp, pos = h // 2, h % 2
        val, s = o[:, grp * LANES:(grp + 1) * LANES], o[:, (1 - grp) * LANES:(2 - grp) * LANES]
        live = (lane >= pos * HEAD_DIM) & (lane < (pos + 1) * HEAD_DIM)
        part = jnp.where(live, val / jnp.where(live, s, 1.0), 0.0)
        halves[grp] = part if halves[grp] is None else halves[grp] + part
        yield
    return jnp.concatenate(halves, axis=-1)


def _short_conv(u, b_gate, c_gate, w, vs_ref, rows):
    v = c_gate * u
    base = SUBLANES
    vs_ref[base:base + rows, :] = v
    conv = (w[0:1] * vs_ref[base - 2:base - 2 + rows, :]
            + w[1:2] * vs_ref[base - 1:base - 1 + rows, :]
            + w[2:3] * v)
    return b_gate * conv


def _swa(q, kt_prev_ref, kt_cur_ref, v_prev_ref, v_cur_ref, bias_ref, sinks_ref, layer, first):
    col = lax.broadcasted_iota(jnp.int32, (BLOCK, 2 * BLOCK), 1)
    kill_prev = jnp.logical_and(first, col < BLOCK)
    zeros = jnp.zeros((HEAD_DIM, 2 * BLOCK), BF16)
    ones = jnp.ones((2 * BLOCK, LANES), BF16)

    def scores(n, hk):
        pair = slice((hk // 2) * LANES, (hk // 2 + 1) * LANES)
        if n == 0:
            kt = jnp.concatenate([kt_prev_ref[hk * HEAD_DIM:(hk + 1) * HEAD_DIM, :],
                                  kt_cur_ref[hk * HEAD_DIM:(hk + 1) * HEAD_DIM, 0:BLOCK]], axis=1)
            v2 = jnp.concatenate([v_prev_ref[:, pair], v_cur_ref[0:BLOCK, pair]], axis=0)
        else:
            keys = slice((n - 1) * BLOCK, (n + 1) * BLOCK)
            kt = kt_cur_ref[hk * HEAD_DIM:(hk + 1) * HEAD_DIM, keys]
            v2 = v_cur_ref[keys, pair]
        k_lo = jnp.concatenate([kt, zeros], axis=0)
        k_hi = jnp.concatenate([zeros, kt], axis=0)
        logits = []
        for g in range(GROUP):
            head = hk * GROUP + g
            grp, pos = head // 2, head % 2
            qg = q[n * BLOCK:(n + 1) * BLOCK, grp * LANES:(grp + 1) * LANES]
            bias = bias_ref[head]
            if n == 0:
                bias = jnp.where(kill_prev, NEG, bias)
            logits.append(_dot(qg, k_hi if pos else k_lo) + bias)
        return logits, v2

    def attend(hk, logits, v2):
        ps, sink_terms = [], []
        for g in range(GROUP):
            sink = sinks_ref[layer, hk * GROUP + g]
            m = jnp.maximum(jnp.max(logits[g], axis=-1, keepdims=True), sink)
            ps.append(jnp.exp(logits[g] - m).astype(BF16))
            sink_terms.append(jnp.exp(sink - m))
        o3 = _dot(jnp.concatenate(ps, axis=0), jnp.concatenate([v2, ones], axis=1))
        half = slice((hk % 2) * HEAD_DIM, (hk % 2 + 1) * HEAD_DIM)
        sums = slice(LANES + half.start, LANES + half.stop)
        outs = []
        for g in range(GROUP):
            rows = slice(g * BLOCK, (g + 1) * BLOCK)
            outs.append(o3[rows, half] / (o3[rows, sums] + sink_terms[g]))
        return outs

    units = [(i, hk) for i in range(q.shape[0] // BLOCK) for hk in range(N_KV_HEADS)]
    outs = {i: [] for i, _ in units}
    nxt = scores(*units[0])
    for k, (i, hk) in enumerate(units):
        cur = nxt
        if k + 1 < len(units):
            nxt = scores(*units[k + 1])
        yield
        outs[i] += attend(hk, *cur)
    return jnp.concatenate([jnp.concatenate(outs[i], axis=-1) for i in sorted(outs)], axis=0)


def _ffn(x_ref, h_ref, xf_ref, wg_ref, wu_ref, wd_ref):
    h = h_ref[...]
    acts = []
    for c in range(D_FF // MXU_DIM):
        sl = slice(c * MXU_DIM, (c + 1) * MXU_DIM)
        gate = _dot(h, wg_ref[:, sl])
        yield
        up = _dot(h, wu_ref[:, sl])
        acts.append((gate * jax.nn.sigmoid(gate) * up).astype(BF16))
        if c < D_MODEL // MXU_DIM:
            xf_ref[:, sl] = x_ref[:, sl]
        yield
    yield "tail"
    act = jnp.concatenate(acts, axis=-1)
    for n in range(D_MODEL // MXU_DIM):
        cs = slice(n * MXU_DIM, (n + 1) * MXU_DIM)
        yield cs, xf_ref[:, cs] + _dot(act, wd_ref[:, cs])


def _advance(gen):
    try:
        return next(gen)
    except StopIteration:
        return "end"


def _run(gen):
    while _advance(gen) != "end":
        pass


def _interleave(ffn, n_ffn, mix, n_mix, ffn_blocks_first):
    advance = _advance
    f_state = m_state = None
    done = 0
    for i in range(n_ffn):
        if f_state is None:
            f_state = advance(ffn)
        while m_state is None and done * n_ffn < (i + 1) * n_mix:
            m_state = advance(mix)
            done += 1
    while f_state is None:
        f_state = advance(ffn)
    while m_state is None:
        m_state = advance(mix)
    assert (f_state, m_state) == ("tail", "tail")
    for _ in range(ffn_blocks_first):
        advance(ffn)
    assert advance(mix) == "end"
    _run(ffn)


def _layer_kernel(*refs, mixer, kv, epilogue, layer, tm, tiles_per_batch, n_cast):
    it = iter(refs)
    x_ref, gmix_ref, win_ref, mkt_ref, mv_ref, wout_ref = (next(it) for _ in range(6))
    gffn_ref, wg_hbm, wu_hbm, wd_hbm = (next(it) for _ in range(4))
    if mixer == "conv":
        convw_ref = next(it)
    else:
        if kv == "compute":
            gkv_ref, wkt_ref, wv_ref = next(it), next(it), next(it)
        else:
            kt_in_ref, v_in_ref = next(it), next(it)
        bias_ref, sinks_ref = next(it), next(it)
    if epilogue == "final":
        gfin_ref = next(it)
    cast_in = [next(it) for _ in range(n_cast)]
    out_ref = next(it)
    if kv == "compute":
        kt_out_ref, v_out_ref = next(it), next(it)
    cast_out = [next(it) for _ in range(n_cast)]
    x1_ref, h2_ref, xf_ref = next(it), next(it), next(it)
    wg_ref, wu_ref, wd_ref, w_sem = (next(it) for _ in range(4))
    if mixer == "conv":
        vs_ref = next(it)
    else:
        kt_prev_ref, v_prev_ref = next(it), next(it)
        kt_cur_ref, v_cur_ref = (kt_out_ref, v_out_ref) if kv == "compute" else (kt_in_ref, v_in_ref)

    t = pl.program_id(0)
    n_tiles = pl.num_programs(0) - 1
    first = (t % tiles_per_batch) == 0

    ffn_weight_loads = [pltpu.make_async_copy(src, dst, w_sem.at[i]) for i, (src, dst) in
                        enumerate(((wg_hbm, wg_ref), (wu_hbm, wu_ref), (wd_hbm, wd_ref)))]

    @pl.when(first)
    def _():
        if mixer == "conv":
            vs_ref[0:SUBLANES, :] = jnp.zeros((SUBLANES, CONV_WIDTH), F32)
        else:
            kt_prev_ref[...] = jnp.zeros((KV_WIDTH, BLOCK), BF16)
            v_prev_ref[...] = jnp.zeros((BLOCK, KV_WIDTH), BF16)

    def mix():
        x = x_ref[...]
        h = _rms(x, gmix_ref[...]).astype(BF16)
        yield
        if kv == "compute":
            hkv = _rms(x, gkv_ref[...]).astype(BF16)
            kt_out_ref[...] = (_dot_nt(wkt_ref[...], hkv) * SCALE).astype(BF16)
            yield
            v_out_ref[...] = _dot(hkv, wv_ref[...]).astype(BF16)
            yield
        widths = (CONV_WIDTH,) * 3 + (MEM_WIDTH,) if mixer == "conv" else (Q_WIDTH, MEM_WIDTH)
        fields, lo = [], 0
        for w in widths:
            fields.append(_dot(h, win_ref[:, lo:lo + w]))
            lo += w
            yield
        for src, dst in zip(cast_in, cast_out):
            dst[...] = src[...].astype(BF16)
        if mixer == "conv":
            u, b_gate, c_gate, qm = fields
            y_tok = _short_conv(u, b_gate, c_gate, convw_ref[...], vs_ref, tm)
            yield
        else:
            q, qm = fields
            y_tok = yield from _swa(q.astype(BF16), kt_prev_ref, kt_cur_ref, v_prev_ref, v_cur_ref,
                                    bias_ref, sinks_ref, layer, first)
        y_mem = yield from _mem_attention(qm.astype(BF16), mkt_ref, mv_ref)
        y = jnp.concatenate([y_tok, y_mem], axis=-1).astype(BF16)
        half = D_MODEL // 2
        lo_half = x[:, :half] + _dot(y, wout_ref[:, :half])
        yield
        hi_half = x[:, half:] + _dot(y, wout_ref[:, half:])
        yield "tail"
        x = jnp.concatenate([lo_half, hi_half], axis=-1)
        x1_ref[...] = x
        h2_ref[...] = _rms(x, gffn_ref[...]).astype(BF16)
        if mixer == "conv":
            vs_ref[0:SUBLANES, :] = vs_ref[tm:tm + SUBLANES, :]
        else:
            kt_prev_ref[...] = kt_cur_ref[:, tm - BLOCK:tm]
            v_prev_ref[...] = v_cur_ref[tm - BLOCK:tm, :]

    def ffn():
        blocks = []
        for item in _ffn(x1_ref, h2_ref, xf_ref, wg_ref, wu_ref, wd_ref):
            if isinstance(item, tuple):
                cs, block = item
                if epilogue == "none":
                    out_ref[:, cs] = block
                blocks.append(block)
                yield
            else:
                yield item
        if epilogue == "final":
            out_ref[...] = _rms(jnp.concatenate(blocks, axis=-1), gfin_ref[...])

    @pl.when(t == 0)
    def _():
        for load in ffn_weight_loads:
            load.start()
        _run(mix())
        for load in ffn_weight_loads:
            load.wait()

    @pl.when(jnp.logical_and(t > 0, t < n_tiles))
    def _():
        n_mix = {"conv": 13, "swa": 10 + (tm // BLOCK) * N_KV_HEADS}[mixer]
        n_mix += 2 if kv == "compute" else 0
        n_ffn = 2 * (D_FF // MXU_DIM) + (3 if mixer == "swa" else 0)
        _interleave(ffn(), n_ffn, mix(), n_mix, ffn_blocks_first=1)

    @pl.when(t == n_tiles)
    def _():
        _run(ffn())


def _const_spec(shape, index):
    return pl.BlockSpec(shape, lambda t: index, pipeline_mode=pl.Buffered(1))


def _cast_rows(n_rows, n_steps):
    tile = 2 * SUBLANES
    return min(r for r in range(tile, n_rows + 1, tile)
               if n_rows % r == 0 and n_rows // r <= n_steps)


def _layer(x, *, seq, mixer, kv, epilogue, layer, sub, w, cast_next, p):
    tokens, _ = x.shape
    tm = TOKEN_TILE
    assert seq % tm == 0 and tm % BLOCK == 0
    tpb = seq // tm
    n_tiles = tokens // tm
    proj_w = A_PROJ if mixer == "conv" else B_PROJ

    def mix_tile(t):
        return jnp.minimum(t, n_tiles - 1)

    def ffn_tile(t):
        return jnp.maximum(t - 1, 0)

    def x_in_map(t):
        return mix_tile(t), 0

    def x_out_map(t):
        return ffn_tile(t), 0

    def mem_map(t):
        return layer, mix_tile(t) // tpb, 0, 0, 0

    def kt_map(t):
        return 0, mix_tile(t)

    def v_map(t):
        return mix_tile(t), 0

    mem_spec = pl.BlockSpec((None, None, N_MEM_HEADS, MEM_WIDTH, N_MEM), mem_map)
    in_specs = [
        pl.BlockSpec((tm, D_MODEL), x_in_map),
        _const_spec((None, 1, D_MODEL), (layer, 0, 0)),
        _const_spec((D_MODEL, proj_w), (0, 0)),
        mem_spec, mem_spec,
        _const_spec((D_MODEL, D_MODEL), (0, 0)),
        _const_spec((None, 1, D_MODEL), (layer, 0, 0)),
        pl.BlockSpec(memory_space=pl.ANY),
        pl.BlockSpec(memory_space=pl.ANY),
        pl.BlockSpec(memory_space=pl.ANY),
    ]
    args = [x, p["norm_mix"], w["in"], p["mkt"], p["mv"], w["out"],
            p["norm_ffn"], w["gate"], w["up"], w["down"]]
    scratch = [pltpu.VMEM((tm, D_MODEL), F32), pltpu.VMEM((tm, D_MODEL), BF16),
               pltpu.VMEM((tm, D_MODEL), F32),
               pltpu.VMEM((D_MODEL, D_FF), BF16), pltpu.VMEM((D_MODEL, D_FF), BF16),
               pltpu.VMEM((D_FF, D_MODEL), BF16), pltpu.SemaphoreType.DMA((3,))]
    if mixer == "conv":
        in_specs.append(_const_spec((None, CONV_K, CONV_WIDTH), (sub, 0, 0)))
        args.append(p["conv_w"])
        scratch.append(pltpu.VMEM((tm + SUBLANES, CONV_WIDTH), F32))
    else:
        if kv == "compute":
            in_specs += [_const_spec((1, D_MODEL), (0, 0)),
                         _const_spec((KV_WIDTH, D_MODEL), (0, 0)),
                         _const_spec((D_MODEL, KV_WIDTH), (0, 0))]
            args += [p["kv_norm"], p["w_kt"], p["w_v"]]
        else:
            in_specs += [pl.BlockSpec((KV_WIDTH, tm), kt_map), pl.BlockSpec((tm, KV_WIDTH), v_map)]
            args += [p["kt"], p["v"]]
        in_specs += [_const_spec((N_Q_HEADS, BLOCK, 2 * BLOCK), (0, 0, 0)),
                     pl.BlockSpec(memory_space=pltpu.SMEM)]
        args += [p["bias"], p["sinks"]]
        scratch += [pltpu.VMEM((KV_WIDTH, BLOCK), BF16), pltpu.VMEM((BLOCK, KV_WIDTH), BF16)]
    if epilogue == "final":
        in_specs.append(_const_spec((1, D_MODEL), (0, 0)))
        args.append(p["final_norm"])

    out_specs = [pl.BlockSpec((tm, D_MODEL), x_out_map)]
    out_shape = [jax.ShapeDtypeStruct(x.shape, F32)]
    if kv == "compute":
        out_specs += [pl.BlockSpec((KV_WIDTH, tm), kt_map), pl.BlockSpec((tm, KV_WIDTH), v_map)]
        out_shape += [jax.ShapeDtypeStruct((KV_WIDTH, tokens), BF16),
                      jax.ShapeDtypeStruct((tokens, KV_WIDTH), BF16)]
    for stack, idx in cast_next:
        _, n_rows, n_cols = stack.shape
        rows = _cast_rows(n_rows, n_tiles)
        last = n_rows // rows - 1
        in_specs.append(pl.BlockSpec((None, rows, n_cols),
                                     lambda t, idx=idx, last=last: (idx, jnp.minimum(t, last), 0)))
        args.append(stack)
        out_specs.append(pl.BlockSpec((rows, n_cols),
                                      lambda t, last=last: (jnp.minimum(t, last), 0)))
        out_shape.append(jax.ShapeDtypeStruct((n_rows, n_cols), BF16))

    outs = pl.pallas_call(
        functools.partial(_layer_kernel, mixer=mixer, kv=kv, epilogue=epilogue, layer=sub, tm=tm,
                          tiles_per_batch=tpb, n_cast=len(cast_next)),
        grid=(n_tiles + 1,),
        in_specs=in_specs,
        out_specs=out_specs,
        out_shape=out_shape,
        scratch_shapes=scratch,
        compiler_params=pltpu.CompilerParams(
            dimension_semantics=("arbitrary",),
            vmem_limit_bytes=VMEM_LIMIT_BYTES),
        name=f"layer{layer}_{mixer}",
    )(*args)
    return outs


def kernel(x, mem, norm_mix, norm_ffn, a_w_in, a_conv_w, a_w_out, kv_norm, w_kv, b_w_q, b_sinks,
           b_w_out, rel_bias, mem_norm, w_mem_kv, w_gate, w_up, w_down, final_norm):
    mkt, mv = _mem_kv(mem, mem_norm, w_mem_kv)
    p = {
        "norm_mix": norm_mix.reshape(DEPTH, 1, D_MODEL),
        "norm_ffn": norm_ffn.reshape(DEPTH, 1, D_MODEL),
        "conv_w": a_conv_w,
        "mkt": mkt, "mv": mv,
        "kv_norm": kv_norm.reshape(1, D_MODEL),
        "w_kt": w_kv[:, :KV_WIDTH].T.astype(BF16),
        "w_v": w_kv[:, KV_WIDTH:].astype(BF16),
        "final_norm": final_norm.reshape(1, D_MODEL),
        "sinks": b_sinks,
        "bias": _band_bias(rel_bias),
    }

    def f32_weights(i):
        stack_in, stack_out, sub = (a_w_in, a_w_out, i) if i < N_A else (b_w_q, b_w_out, i - N_A)
        return {"in": (stack_in, sub), "out": (stack_out, sub),
                "gate": (w_gate, i), "up": (w_up, i), "down": (w_down, i)}

    w = {k: stack[idx].astype(BF16) for k, (stack, idx) in f32_weights(0).items()}
    batch, seq, _ = x.shape
    x = x.reshape(batch * seq, D_MODEL)
    for i in range(DEPTH):
        mixer = "conv" if i < N_A else "swa"
        sub = i if i < N_A else i - N_A
        kv = None if i < N_A else ("compute" if i == N_A else "load")
        epilogue = "final" if i == DEPTH - 1 else "none"
        nxt = f32_weights(i + 1) if i + 1 < DEPTH else {}
        outs = _layer(x, seq=seq, mixer=mixer, kv=kv, epilogue=epilogue, layer=i, sub=sub, w=w,
                      cast_next=list(nxt.values()), p=p)
        x = outs[0]
        if kv == "compute":
            p["kt"], p["v"] = outs[1], outs[2]
        w = dict(zip(nxt.keys(), outs[len(outs) - len(nxt):]))
    return x.reshape(batch, seq, D_MODEL)
```
